```python
import jax, jax.numpy as jnp
from jax import lax
import numpy as np

D_MODEL = 1024
BATCH = 4
SEQ = 4096
DEPTH = 2
DEC_BATCH = 16
DEC_SEQ = 16
PAST_LEN = 4096

CHUNK = 64
N_META = 16
A_HEADS = 8
A_HEAD_DIM = 64
A_WIDTH = A_HEADS * A_HEAD_DIM
A_DECAY_LORA = 64
A_AAA_LORA = 64
A_GATE_LORA = 128
A_COLS = 3 * A_WIDTH + A_DECAY_LORA + A_AAA_LORA + A_GATE_LORA
B_HEADS = 4
B_HEAD_DIM = 128
B_WIDTH = B_HEADS * B_HEAD_DIM
B_CONV = 4
B_COLS = 4 * B_WIDTH + 2 * B_HEADS
D_MIX = A_WIDTH + B_WIDTH
D_IN = A_COLS + B_COLS
D_FF = 2816
FFN_CONV = 3
RMS_EPS = 1e-6
GN_EPS = 64e-5

kernel_name = "hymba_rwkv7_mlstm_convffn_stream_step"


def rmsnorm(x, g):
    xf = x.astype(jnp.float32)
    y = xf * lax.rsqrt(jnp.mean(xf * xf, axis=-1, keepdims=True) + RMS_EPS)
    return (y * g.astype(jnp.float32)).astype(x.dtype)


def causal_dwconv(x, buf, w, b):
    K = w.shape[0]
    T = x.shape[1]
    xp = jnp.concatenate([buf.astype(x.dtype), x], axis=1)
    y = sum(w[j] * xp[:, j:j + T] for j in range(K)) + b
    return y, xp[:, T:]


def rwkv7_mix(p, shift_prev, S0, mu, w0, w2, a0, a2, g2, k_k, k_a, r_k, ln_w, ln_b):
    Bn, T, _ = p.shape
    prev = jnp.concatenate([shift_prev[:, None].astype(p.dtype), p[:, :-1]], axis=1)
    pm = (p + (prev - p) * mu).astype(jnp.float32)
    i1, i2, i3 = A_WIDTH, 2 * A_WIDTH, 3 * A_WIDTH
    i4, i5 = i3 + A_DECAY_LORA, i3 + A_DECAY_LORA + A_AAA_LORA
    r, k, v = pm[..., :i1], pm[..., i1:i2], pm[..., i2:i3]
    wl, al, gl = pm[..., i3:i4], pm[..., i4:i5], pm[..., i5:]
    w = -jax.nn.softplus(-(w0 + jnp.tanh(wl) @ w2)) - 0.5
    decay = jnp.exp(-jnp.exp(w))
    a = jax.nn.sigmoid(a0 + al @ a2)
    g = jax.nn.sigmoid(gl) @ g2
    hd = lambda t: t.reshape(Bn, T, A_HEADS, A_HEAD_DIM)
    r, k, v, decay, a = hd(r), hd(k), hd(v), hd(decay), hd(a)
    kk = k * k_k.reshape(A_HEADS, A_HEAD_DIM)
    kk = kk * lax.rsqrt(jnp.sum(kk * kk, axis=-1, keepdims=True) + 1e-12)
    k = k * (1.0 + (a - 1.0) * k_a.reshape(A_HEADS, A_HEAD_DIM))

    def step(S, inp):
        r_t, k_t, v_t, w_t, kk_t, a_t = inp
        sa = jnp.einsum('bhvk,bhk->bhv', S, -kk_t)
        S = (S * w_t[:, :, None, :] + sa[..., None] * (kk_t * a_t)[:, :, None, :]
             + v_t[..., None] * k_t[:, :, None, :])
        return S, jnp.einsum('bhvk,bhk->bhv', S, r_t)

    xs = tuple(jnp.moveaxis(t, 1, 0) for t in (r, k, v, decay, kk, a))
    S_new, y = lax.scan(step, S0.astype(jnp.float32), xs)
    y = jnp.moveaxis(y, 0, 1)
    mean = jnp.mean(y, axis=-1, keepdims=True)
    var = jnp.mean(jnp.square(y - mean), axis=-1, keepdims=True)
    y = ((y - mean) * lax.rsqrt(var + GN_EPS)).reshape(Bn, T, A_WIDTH) * ln_w + ln_b
    bonus = jnp.sum(r * k * r_k.reshape(A_HEADS, A_HEAD_DIM), axis=-1, keepdims=True) * v
    y = (y + bonus.reshape(Bn, T, A_WIDTH)) * g
    return y, p[:, -1], S_new


def mlstm_block(state, blk):
    C, n, m = state
    q, k, v, li, lf = blk
    L = q.shape[2]
    b = jnp.cumsum(lf, axis=-1)
    causal = jnp.tril(jnp.ones((L, L), dtype=bool))
    Dm = jnp.where(causal, b[..., :, None] - b[..., None, :] + li[..., None, :], -jnp.inf)
    inter = b + m[..., None]
    mt = jnp.maximum(inter, jnp.max(Dm, axis=-1))
    wi = jnp.exp(Dm - mt[..., None])
    wo = jnp.exp(inter - mt)
    s = jnp.einsum('bhtd,bhjd->bhtj', q, k) * wi
    num = wo[..., None] * jnp.einsum('bhtd,bhde->bhte', q, C) + jnp.einsum('bhtj,bhje->bhte', s, v)
    den = wo * jnp.einsum('bhtd,bhd->bht', q, n) + jnp.sum(s, axis=-1)
    h = num / jnp.maximum(jnp.abs(den), jnp.exp(-mt))[..., None]
    m_new = mt[..., -1]
    ws = jnp.exp(b[..., -1:] - b + li - m_new[..., None])
    dec = jnp.exp(b[..., -1] + m - m_new)
    C_new = dec[..., None, None] * C + jnp.einsum('bhj,bhjd,bhje->bhde', ws, k, v)
    n_new = dec[..., None] * n + jnp.einsum('bhj,bhjd->bhd', ws, k)
    return (C_new, n_new, m_new), h


def mlstm_seq(state, q, k, v, li, lf, lead):
    Bn, H, T, d = q.shape
    state, h0 = mlstm_block(state, (q[:, :, :lead], k[:, :, :lead], v[:, :, :lead], li[:, :, :lead], lf[:, :, :lead]))
    rest = T - lead
    if rest == 0:
        return state, h0
    nb = rest // CHUNK

    def blocks(t):
        t = t[:, :, lead:]
        t = t.reshape(t.shape[:2] + (nb, CHUNK) + t.shape[3:])
        return jnp.moveaxis(t, 2, 0)

    state, hs = lax.scan(mlstm_block, state, (blocks(q), blocks(k), blocks(v), blocks(li), blocks(lf)))
    hs = jnp.moveaxis(hs, 0, 2).reshape(Bn, H, rest, d)
    return state, jnp.concatenate([h0, hs], axis=2)


def mlstm_mix(p, conv_buf, C0, n0, m0, conv_w, conv_b, i_bias, f_bias, hn_w, lead):
    Bn, T, _ = p.shape
    qk, conv_new = causal_dwconv(p[..., :2 * B_WIDTH], conv_buf, conv_w, conv_b)
    qk = jax.nn.silu(qk.astype(jnp.float32))
    pf = p.astype(jnp.float32)
    v = pf[..., 2 * B_WIDTH:3 * B_WIDTH]
    o = pf[..., 3 * B_WIDTH:4 * B_WIDTH]
    gates = pf[..., 4 * B_WIDTH:]
    li = gates[..., :B_HEADS] + i_bias
    lf = jax.nn.log_sigmoid(gates[..., B_HEADS:] + f_bias)
    hd = lambda t: jnp.moveaxis(t.reshape(Bn, T, B_HEADS, B_HEAD_DIM), 2, 1)
    q = hd(qk[..., :B_WIDTH])
    k = hd(qk[..., B_WIDTH:]) * (B_HEAD_DIM ** -0.5)
    v = hd(v)
    li, lf = jnp.moveaxis(li, 2, 1), jnp.moveaxis(lf, 2, 1)
    (C, n, m), h = mlstm_seq((C0.astype(jnp.float32), n0.astype(jnp.float32), m0.astype(jnp.float32)),
                             q, k, v, li, lf, lead)
    h = jnp.moveaxis(h, 1, 2)
    h = h * lax.rsqrt(jnp.mean(h * h, axis=-1, keepdims=True) + RMS_EPS)
    h = h.reshape(Bn, T, B_WIDTH) * hn_w * jax.nn.sigmoid(o)
    return h, conv_new, C, n, m


def layer(x, st, w, lead):
    shift0, S0, bconv0, C0, n0, m0, fconv0 = st
    (norm_mix, w_in, a_mu, a_w0, a_w2, a_a0, a_a2, a_g2, a_k_k, a_k_a, a_r_k, a_ln_w, a_ln_b,
     b_conv_w, b_conv_b, b_i_bias, b_f_bias, b_hn_w, w_out, norm_ffn, w_up, ffn_conv_w, ffn_conv_b, w_down) = w
    h = rmsnorm(x, norm_mix)
    p = h @ w_in
    ya, shift1, S1 = rwkv7_mix(p[..., :A_COLS], shift0, S0, a_mu, a_w0, a_w2, a_a0, a_a2, a_g2,
                               a_k_k, a_k_a, a_r_k, a_ln_w, a_ln_b)
    yb, bconv1, C1, n1, m1 = mlstm_mix(p[..., A_COLS:], bconv0, C0, n0, m0, b_conv_w, b_conv_b,
                                       b_i_bias, b_f_bias, b_hn_w, lead)
    x = x + jnp.concatenate([ya, yb], axis=-1).astype(x.dtype) @ w_out
    h2 = rmsnorm(x, norm_ffn)
    ug = h2 @ w_up
    u, fconv1 = causal_dwconv(ug[..., :D_FF], fconv0, ffn_conv_w, ffn_conv_b)
    x = x + (jax.nn.silu(u) * ug[..., D_FF:]) @ w_down
    return x, (shift1, S1, bconv1, C1, n1, m1, fconv1)


def zero_states(b, dtype):
    f32 = jnp.float32
    return (jnp.zeros((b, A_COLS), dtype), jnp.zeros((b, A_HEADS, A_HEAD_DIM, A_HEAD_DIM), f32),
            jnp.zeros((b, B_CONV - 1, 2 * B_WIDTH), dtype), jnp.zeros((b, B_HEADS, B_HEAD_DIM, B_HEAD_DIM), f32),
            jnp.zeros((b, B_HEADS, B_HEAD_DIM), f32), jnp.zeros((b, B_HEADS), f32),
            jnp.zeros((b, FFN_CONV - 1, D_FF), dtype))


def trunk(x, states, layer_weights, lead):
    outs = []
    for l in range(DEPTH):
        st = zero_states(x.shape[0], x.dtype) if states is None else tuple(s[l] for s in states)
        x, new_st = layer(x, st, tuple(t[l] for t in layer_weights), lead)
        outs.append(new_st)
    return x, tuple(jnp.stack([o[i] for o in outs]) for i in range(7))


def setup_inputs(seed: int = 0) -> dict:
    key = jax.random.key(seed)
    ks = iter(jax.random.split(key, 40))
    nrm = lambda shape, s=1.0: s * jax.random.normal(next(ks), shape, jnp.float32)
    uni = lambda shape, lo, hi: jax.random.uniform(next(ks), shape, jnp.float32, lo, hi)
    L = DEPTH
    return {
        "x_prompt": nrm((BATCH, SEQ, D_MODEL)),
        "x_sample": nrm((DEC_BATCH, DEC_SEQ, D_MODEL)),
        "state_rwkv_shift": nrm((L, DEC_BATCH, A_COLS)),
        "state_rwkv_wkv": nrm((L, DEC_BATCH, A_HEADS, A_HEAD_DIM, A_HEAD_DIM), 0.3),
        "state_mlstm_conv": nrm((L, DEC_BATCH, B_CONV - 1, 2 * B_WIDTH)),
        "state_mlstm_C": nrm((L, DEC_BATCH, B_HEADS, B_HEAD_DIM, B_HEAD_DIM), 0.1),
        "state_mlstm_n": nrm((L, DEC_BATCH, B_HEADS, B_HEAD_DIM), 0.3),
        "state_mlstm_m": nrm((L, DEC_BATCH, B_HEADS), 1.0),
        "state_ffn_conv": nrm((L, DEC_BATCH, FFN_CONV - 1, D_FF)),
        "meta_tokens": nrm((N_META, D_MODEL)),
        "norm_mix": 1.0 + nrm((L, D_MODEL), 0.02),
        "w_in": nrm((L, D_MODEL, D_IN), D_MODEL ** -0.5),
        "a_mu": uni((L, A_COLS), 0.0, 1.0),
        "a_w0": uni((L, A_WIDTH), -6.0, -1.0),
        "a_w2": nrm((L, A_DECAY_LORA, A_WIDTH), 0.1),
        "a_a0": nrm((L, A_WIDTH), 0.5),
        "a_a2": nrm((L, A_AAA_LORA, A_WIDTH), A_AAA_LORA ** -0.5),
        "a_g2": nrm((L, A_GATE_LORA, A_WIDTH), A_GATE_LORA ** -0.5),
        "a_k_k": 0.85 + nrm((L, A_WIDTH), 0.05),
        "a_k_a": 1.0 + nrm((L, A_WIDTH), 0.05),
        "a_r_k": nrm((L, A_WIDTH), 0.1),
        "a_ln_w": 1.0 + nrm((L, A_WIDTH), 0.02),
        "a_ln_b": nrm((L, A_WIDTH), 0.02),
        "b_conv_w": nrm((L, B_CONV, 2 * B_WIDTH), B_CONV ** -0.5),
        "b_conv_b": nrm((L, 2 * B_WIDTH), 0.02),
        "b_i_bias": nrm((L, B_HEADS), 0.1),
        "b_f_bias": 3.0 + nrm((L, B_HEADS), 0.5),
        "b_hn_w": 1.0 + nrm((L, B_WIDTH), 0.02),
        "w_out": nrm((L, D_MIX, D_MODEL), D_MIX ** -0.5),
        "norm_ffn": 1.0 + nrm((L, D_MODEL), 0.02),
        "w_up": nrm((L, D_MODEL, 2 * D_FF), D_MODEL ** -0.5),
        "ffn_conv_w": nrm((L, FFN_CONV, D_FF), FFN_CONV ** -0.5),
        "ffn_conv_b": nrm((L, D_FF), 0.02),
        "w_down": nrm((L, D_FF, D_MODEL), D_FF ** -0.5),
        "norm_final": 1.0 + nrm((D_MODEL,), 0.02),
    }


def reference(x_prompt, x_sample, state_rwkv_shift, state_rwkv_wkv, state_mlstm_conv, state_mlstm_C,
              state_mlstm_n, state_mlstm_m, state_ffn_conv, meta_tokens, norm_mix, w_in, a_mu, a_w0, a_w2,
              a_a0, a_a2, a_g2, a_k_k, a_k_a, a_r_k, a_ln_w, a_ln_b, b_conv_w, b_conv_b, b_i_bias, b_f_bias,
              b_hn_w, w_out, norm_ffn, w_up, ffn_conv_w, ffn_conv_b, w_down, norm_final):
    layer_weights = (norm_mix, w_in, a_mu, a_w0, a_w2, a_a0, a_a2, a_g2, a_k_k, a_k_a, a_r_k, a_ln_w, a_ln_b,
                     b_conv_w, b_conv_b, b_i_bias, b_f_bias, b_hn_w, w_out, norm_ffn, w_up, ffn_conv_w,
                     ffn_conv_b, w_down)
    meta = jnp.broadcast_to(meta_tokens[None].astype(x_prompt.dtype), (x_prompt.shape[0], N_META, D_MODEL))
    xp = jnp.concatenate([meta, x_prompt], axis=1)
    hp, (p_rwkv_shift, p_rwkv_wkv, p_mlstm_conv, p_mlstm_C, p_mlstm_n, p_mlstm_m, p_ffn_conv) = trunk(
        xp, None, layer_weights, N_META)
    y_prompt = rmsnorm(hp, norm_final)[:, N_META:]
    states = (state_rwkv_shift, state_rwkv_wkv, state_mlstm_conv, state_mlstm_C, state_mlstm_n,
              state_mlstm_m, state_ffn_conv)
    hs, (s_rwkv_shift, s_rwkv_wkv, s_mlstm_conv, s_mlstm_C, s_mlstm_n, s_mlstm_m, s_ffn_conv) = trunk(
        x_sample, states, layer_weights, x_sample.shape[1])
    y_sample = rmsnorm(hs, norm_final)
    return (y_prompt, y_sample,
            p_rwkv_shift, p_rwkv_wkv, p_mlstm_conv, p_mlstm_C, p_mlstm_n, p_mlstm_m, p_ffn_conv,
            s_rwkv_shift, s_rwkv_wkv, s_mlstm_conv, s_mlstm_C, s_mlstm_n, s_mlstm_m, s_ffn_conv)
```

```python
import functools

import jax
import jax.numpy as jnp
from jax import lax
from jax.experimental import pallas as pl
from jax.experimental.pallas import tpu as pltpu

D_MODEL = 1024
DEPTH = 2
N_META = 16
A_HEADS = 8
A_HEAD_DIM = 64
A_WIDTH = 512
A_DECAY_LORA = 64
A_AAA_LORA = 64
A_GATE_LORA = 128
A_COLS = 1792
B_HEADS = 4
B_HEAD_DIM = 128
B_WIDTH = 512
B_CONV = 4
B_MAIN = 4 * B_WIDTH
GATE_PAD = 128
D_FF = 2816
FFN_CONV = 3
RMS_EPS = 1e-6
GN_EPS = 64e-5
CARRY = 8
VMEM_LIMIT = 56 * 1024 * 1024

F32 = jnp.float32
BF16 = jnp.bfloat16
HI = lax.Precision.HIGHEST


def _dot_hi(a, b):
    return jnp.dot(a, b, precision=HI, preferred_element_type=F32)


def _einsum_hi(spec, a, b):
    return jnp.einsum(spec, a, b, precision=HI, preferred_element_type=F32)


def _dot(a, b):
    return jnp.dot(a, b, preferred_element_type=F32)


def _sigmoid(x):
    return 1.0 / (1.0 + jnp.exp(-x))


def _softplus(x):
    return jnp.maximum(x, 0.0) + jnp.log(1.0 + jnp.exp(-jnp.abs(x)))


def _rms(x, g):
    return x * lax.rsqrt(jnp.mean(x * x, axis=-1, keepdims=True) + RMS_EPS) * g


def _tril(n, strict):
    row = lax.broadcasted_iota(jnp.int32, (n, n), 0)
    col = lax.broadcasted_iota(jnp.int32, (n, n), 1)
    return (col < row) if strict else (col <= row)


def _in_proj_kernel(x_ref, g_ref, w_ref, oa_ref, ob_ref, og_ref):
    h = _rms(x_ref[...], g_ref[...]).astype(BF16)
    oa_ref[...] = _dot(h, w_ref[:, 0:A_COLS])
    ob_ref[...] = _dot(h, w_ref[:, A_COLS:A_COLS + B_MAIN])
    og_ref[...] = _dot(h, w_ref[:, A_COLS + B_MAIN:])


def _in_proj(x2d, g, w, tm):
    m = x2d.shape[0]
    n = w.shape[1]
    const = lambda i: (0, 0)
    return pl.pallas_call(
        _in_proj_kernel,
        grid=(m // tm,),
        in_specs=[
            pl.BlockSpec((tm, D_MODEL), lambda i: (i, 0)),
            pl.BlockSpec((1, D_MODEL), const),
            pl.BlockSpec((D_MODEL, n), const, pipeline_mode=pl.Buffered(1)),
        ],
        out_specs=[
            pl.BlockSpec((tm, A_COLS), lambda i: (i, 0)),
            pl.BlockSpec((tm, B_MAIN), lambda i: (i, 0)),
            pl.BlockSpec((tm, GATE_PAD), lambda i: (i, 0)),
        ],
        out_shape=[
            jax.ShapeDtypeStruct((m, A_COLS), F32),
            jax.ShapeDtypeStruct((m, B_MAIN), F32),
            jax.ShapeDtypeStruct((m, GATE_PAD), F32),
        ],
        compiler_params=pltpu.CompilerParams(
            dimension_semantics=("arbitrary",), vmem_limit_bytes=VMEM_LIMIT),
        name="in_proj",
    )(x2d, g, w)


def _rwkv_kernel(p_ref, shift_ref, s0_ref, mu_ref, w0_ref, w2_ref, a0_ref, a2_ref, g2_ref,
                 kk_ref, ka_ref, rk_ref, lnw_ref, lnb_ref, seg_ref,
                 y_ref, shift_out_ref, s_out_ref, pbuf, s_sc, *, chunk):
    L = chunk
    c = pl.program_id(1)

    @pl.when(c == 0)
    def _():
        pbuf[CARRY - 1:CARRY, :] = shift_ref[0]
        s_sc[...] = s0_ref[0]

    p = p_ref[0]
    pbuf[CARRY:CARRY + L, :] = p
    prev = pbuf[CARRY - 1:CARRY - 1 + L, :]
    pm = p + (prev - p) * mu_ref[...]
    pbuf[CARRY - 1:CARRY, :] = p[L - 1:L, :]
    shift_out_ref[0] = p[L - 1:L, :]

    i1, i2, i3 = A_WIDTH, 2 * A_WIDTH, 3 * A_WIDTH
    i4, i5 = i3 + A_DECAY_LORA, i3 + A_DECAY_LORA + A_AAA_LORA
    r, k, v = pm[:, :i1], pm[:, i1:i2], pm[:, i2:i3]
    wl, al, gl = pm[:, i3:i4], pm[:, i4:i5], pm[:, i5:]
    seg = seg_ref[...]
    segsum = lambda t: _dot_hi(t, seg)

    w = -_softplus(-(w0_ref[...] + _dot_hi(jnp.tanh(wl), w2_ref[...]))) - 0.5
    lw = -jnp.exp(w)
    a = _sigmoid(a0_ref[...] + _dot_hi(al, a2_ref[...]))
    g = _dot_hi(_sigmoid(gl), g2_ref[...])
    kk = k * kk_ref[...]
    kk = kk * lax.rsqrt(segsum(kk * kk) + 1e-12)
    kmod = k * (1.0 + (a - 1.0) * ka_ref[...])
    b = kk * a

    cum = _dot_hi(_tril(L, False).astype(F32), lw)
    cum_last = cum[L - 1:L, :]
    g_in = jnp.exp(cum)
    g_inv = jnp.exp(-cum)
    g_prev = jnp.exp(cum - lw)
    g_tail = jnp.exp(cum_last - cum)
    g_last = jnp.exp(cum_last)

    heads = lambda t: jnp.stack(
        [t[:, h * A_HEAD_DIM:(h + 1) * A_HEAD_DIM] for h in range(A_HEADS)], axis=0)
    kkg, rg = heads(kk * g_prev), heads(r * g_in)
    kd, bd = heads(kmod * g_inv), heads(b * g_inv)
    kt, bt = heads(kmod * g_tail), heads(b * g_tail)
    v3 = heads(v)
    gl3 = heads(g_last)

    strict = _tril(L, True)[None]
    incl = _tril(L, False)[None]
    nt = 'hlk,hmk->hlm'
    a_kk = jnp.where(strict, _einsum_hi(nt, kkg, kd), 0.0)
    a_kb = jnp.where(strict, _einsum_hi(nt, kkg, bd), 0.0)
    a_rk = jnp.where(incl, _einsum_hi(nt, rg, kd), 0.0)
    a_rb = jnp.where(incl, _einsum_hi(nt, rg, bd), 0.0)

    mm = 'hlm,hmv->hlv'
    eye = (lax.broadcasted_iota(jnp.int32, (L, L), 0)
           == lax.broadcasted_iota(jnp.int32, (L, L), 1)).astype(F32)[None]
    npow = -a_kb
    tinv = eye + npow
    span = 2
    while span < L:
        npow = _einsum_hi(mm, npow, npow)
        tinv = tinv + _einsum_hi(mm, tinv, npow)
        span *= 2

    s0 = s_sc[...]
    rhs = _einsum_hi('hlk,hvk->hlv', kkg, s0) + _einsum_hi(mm, a_kk, v3)
    u = _einsum_hi(mm, tinv, rhs)
    y3 = (_einsum_hi('hlk,hvk->hlv', rg, s0) + _einsum_hi(mm, a_rk, v3)
          - _einsum_hi(mm, a_rb, u))
    tn = 'hlv,hlk->hvk'
    s_new = s0 * gl3 + _einsum_hi(tn, v3, kt) - _einsum_hi(tn, u, bt)
    s_sc[...] = s_new
    s_out_ref[0] = s_new

    y = jnp.concatenate([y3[h] for h in range(A_HEADS)], axis=-1)
    inv_d = 1.0 / A_HEAD_DIM
    yc = y - segsum(y) * inv_d
    var = segsum(yc * yc) * inv_d
    yn = yc * lax.rsqrt(var + GN_EPS) * lnw_ref[...] + lnb_ref[...]
    bonus = segsum(r * kmod * rk_ref[...]) * v
    y_ref[0] = (yn + bonus) * g


def _rwkv(pa, shift0, s0, wts, seg, chunk):
    bn, t, _ = pa.shape
    row = lambda n: pl.BlockSpec((1, n), lambda b, c: (0, 0))
    mat = lambda m, n: pl.BlockSpec((m, n), lambda b, c: (0, 0))
    state = pl.BlockSpec((1, A_HEADS, A_HEAD_DIM, A_HEAD_DIM), lambda b, c: (b, 0, 0, 0))
    shift = pl.BlockSpec((1, 1, A_COLS), lambda b, c: (b, 0, 0))
    return pl.pallas_call(
        functools.partial(_rwkv_kernel, chunk=chunk),
        grid=(bn, t // chunk),
        in_specs=[
            pl.BlockSpec((1, chunk, A_COLS), lambda b, c: (b, c, 0)),
            shift, state,
            row(A_COLS), row(A_WIDTH), mat(A_DECAY_LORA, A_WIDTH), row(A_WIDTH),
            mat(A_AAA_LORA, A_WIDTH), mat(A_GATE_LORA, A_WIDTH),
            row(A_WIDTH), row(A_WIDTH), row(A_WIDTH), row(A_WIDTH), row(A_WIDTH),
            mat(A_WIDTH, A_WIDTH),
        ],
        out_specs=[
            pl.BlockSpec((1, chunk, A_WIDTH), lambda b, c: (b, c, 0)),
            shift, state,
        ],
        out_shape=[
            jax.ShapeDtypeStruct((bn, t, A_WIDTH), F32),
            jax.ShapeDtypeStruct((bn, 1, A_COLS), F32),
            jax.ShapeDtypeStruct((bn, A_HEADS, A_HEAD_DIM, A_HEAD_DIM), F32),
        ],
        scratch_shapes=[
            pltpu.VMEM((CARRY + chunk, A_COLS), F32),
            pltpu.VMEM((A_HEADS, A_HEAD_DIM, A_HEAD_DIM), F32),
        ],
        compiler_params=pltpu.CompilerParams(
            dimension_semantics=("arbitrary", "arbitrary"), vmem_limit_bytes=VMEM_LIMIT),
        name="rwkv_mix",
    )(pa, shift0, s0, *wts, seg)


def _mlstm_kernel(pb_ref, pg_ref, conv0_ref, c0_ref, n0_ref, m0_ref,
                  cw_ref, cb_ref, gb_ref, hnw_ref,
                  y_ref, conv_out_ref, c_out_ref, n_out_ref, m_out_ref,
                  xbuf, c_sc, n_sc, m_sc, *, chunk):
    L = chunk
    c = pl.program_id(1)
    hist = B_CONV - 1

    @pl.when(c == 0)
    def _():
        xbuf[CARRY - hist:CARRY, :] = conv0_ref[0]
        c_sc[...] = c0_ref[0]
        n_sc[...] = n0_ref[0]
        m_sc[...] = m0_ref[0]

    x = pb_ref[0, :, 0:2 * B_WIDTH]
    xbuf[CARRY:CARRY + L, :] = x
    qk = cb_ref[...] + cw_ref[hist:hist + 1, :] * x
    for j in range(hist):
        qk = qk + cw_ref[j:j + 1, :] * xbuf[CARRY - hist + j:CARRY - hist + j + L, :]
    tail = xbuf[CARRY + L - hist:CARRY + L, :]
    xbuf[CARRY - hist:CARRY, :] = tail
    conv_out_ref[0] = tail
    qk = qk * _sigmoid(qk)

    gates = pg_ref[0] + gb_ref[...]
    lane = lax.broadcasted_iota(jnp.int32, (L, GATE_PAD), 1)
    lf = -_softplus(-gates)
    cum = _dot_hi(_tril(L, False).astype(F32), lf)
    gcols = jnp.where(lane < B_HEADS, gates, cum)
    grows = gcols.T
    causal = _tril(L, False)

    for h in range(B_HEADS):
        lo = h * B_HEAD_DIM
        q = qk[:, lo:lo + B_HEAD_DIM]
        k = qk[:, B_WIDTH + lo:B_WIDTH + lo + B_HEAD_DIM] * (B_HEAD_DIM ** -0.5)
        v = pb_ref[0, :, 2 * B_WIDTH + lo:2 * B_WIDTH + lo + B_HEAD_DIM]
        o = pb_ref[0, :, 3 * B_WIDTH + lo:3 * B_WIDTH + lo + B_HEAD_DIM]
        li_col = gcols[:, h:h + 1]
        b_col = gcols[:, B_HEADS + h:B_HEADS + h + 1]
        li_row = grows[h:h + 1, :]
        b_row = grows[B_HEADS + h:B_HEADS + h + 1, :]
        c_prev = c_sc[h]
        n_prev = n_sc[h:h + 1, :]
        m_prev = m_sc[h:h + 1, 0:1]

        dm = jnp.where(causal, b_col - b_row + li_row, -jnp.inf)
        inter = b_col + m_prev
        mt = jnp.maximum(inter, jnp.max(dm, axis=-1, keepdims=True))
        wi = jnp.exp(dm - mt)
        wo = jnp.exp(inter - mt)
        s = lax.dot_general(q, k, (((1,), (1,)), ((), ())), preferred_element_type=F32) * wi
        num = wo * _dot(q, c_prev) + _dot(s, v)
        den = wo * jnp.sum(q * n_prev, axis=-1, keepdims=True) + jnp.sum(s, axis=-1, keepdims=True)
        hh = num / jnp.maximum(jnp.abs(den), jnp.exp(-mt))

        m_new = mt[L - 1:L, :]
        b_last = b_col[L - 1:L, :]
        ws = jnp.exp(b_last - b_col + li_col - m_new)
        dec = jnp.exp(b_last + m_prev - m_new)
        kw = k * ws
        c_sc[h] = dec * c_prev + lax.dot_general(
            kw, v, (((0,), (0,)), ((), ())), preferred_element_type=F32)
        n_sc[h:h + 1, :] = dec * n_prev + jnp.sum(kw, axis=0, keepdims=True)
        m_sc[h:h + 1, :] = jnp.broadcast_to(m_new, (1, B_HEAD_DIM))

        hh = hh * lax.rsqrt(jnp.mean(hh * hh, axis=-1, keepdims=True) + RMS_EPS)
        y_ref[0, :, lo:lo + B_HEAD_DIM] = hh * hnw_ref[:, lo:lo + B_HEAD_DIM] * _sigmoid(o)

    c_out_ref[0] = c_sc[...]
    n_out_ref[0] = n_sc[...]
    m_out_ref[0] = m_sc[...]


def _mlstm(pb, pg, conv0, c0, n0, m0, wts, chunk):
    bn, t, _ = pb.shape
    hist = B_CONV - 1
    const = lambda b, c: (0, 0)
    per_b3 = lambda b, c: (b, 0, 0)
    per_b4 = lambda b, c: (b, 0, 0, 0)
    conv_spec = pl.BlockSpec((1, hist, 2 * B_WIDTH), per_b3)
    c_spec = pl.BlockSpec((1, B_HEADS, B_HEAD_DIM, B_HEAD_DIM), per_b4)
    nm_spec = pl.BlockSpec((1, B_HEADS, B_HEAD_DIM), per_b3)
    return pl.pallas_call(
        functools.partial(_mlstm_kernel, chunk=chunk),
        grid=(bn, t // chunk),
        in_specs=[
            pl.BlockSpec((1, chunk, B_MAIN), lambda b, c: (b, c, 0)),
            pl.BlockSpec((1, chunk, GATE_PAD), lambda b, c: (b, c, 0)),
            conv_spec, c_spec, nm_spec, nm_spec,
            pl.BlockSpec((B_CONV, 2 * B_WIDTH), const),
            pl.BlockSpec((1, 2 * B_WIDTH), const),
            pl.BlockSpec((1, GATE_PAD), const),
            pl.BlockSpec((1, B_WIDTH), const),
        ],
        out_specs=[
            pl.BlockSpec((1, chunk, B_WIDTH), lambda b, c: (b, c, 0)),
            conv_spec, c_spec, nm_spec, nm_spec,
        ],
        out_shape=[
            jax.ShapeDtypeStruct((bn, t, B_WIDTH), F32),
            jax.ShapeDtypeStruct((bn, hist, 2 * B_WIDTH), F32),
            jax.ShapeDtypeStruct((bn, B_HEADS, B_HEAD_DIM, B_HEAD_DIM), F32),
            jax.ShapeDtypeStruct((bn, B_HEADS, B_HEAD_DIM), F32),
            jax.ShapeDtypeStruct((bn, B_HEADS, B_HEAD_DIM), F32),
        ],
        scratch_shapes=[
            pltpu.VMEM((CARRY + chunk, 2 * B_WIDTH), F32),
            pltpu.VMEM((B_HEADS, B_HEAD_DIM, B_HEAD_DIM), F32),
            pltpu.VMEM((B_HEADS, B_HEAD_DIM), F32),
            pltpu.VMEM((B_HEADS, B_HEAD_DIM), F32),
        ],
        compiler_params=pltpu.CompilerParams(
            dimension_semantics=("arbitrary", "arbitrary"), vmem_limit_bytes=VMEM_LIMIT),
        name="mlstm_mix",
    )(pb, pg, conv0, c0, n0, m0, *wts)


def _ffn_kernel(x_ref, ya_ref, yb_ref, f0_ref, wout_ref, g_ref, wup_ref, cw_ref, cb_ref,
                wdown_ref, gfin_ref, o_ref, fout_ref, ubuf, *, tile, final):
    t = pl.program_id(1)
    hist = FFN_CONV - 1

    @pl.when(t == 0)
    def _():
        ubuf[CARRY - hist:CARRY, :] = f0_ref[0]

    x1 = (x_ref[0]
          + _dot(ya_ref[0].astype(BF16), wout_ref[0:A_WIDTH, :])
          + _dot(yb_ref[0].astype(BF16), wout_ref[A_WIDTH:, :]))
    h2 = _rms(x1, g_ref[...]).astype(BF16)
    u = _dot(h2, wup_ref[:, 0:D_FF])
    gate = _dot(h2, wup_ref[:, D_FF:])
    ubuf[CARRY:CARRY + tile, :] = u
    conv = cb_ref[...] + cw_ref[hist:hist + 1, :] * u
    for j in range(hist):
        conv = conv + cw_ref[j:j + 1, :] * ubuf[CARRY - hist + j:CARRY - hist + j + tile, :]
    tail = ubuf[CARRY + tile - hist:CARRY + tile, :]
    ubuf[CARRY - hist:CARRY, :] = tail
    fout_ref[0] = tail
    act = conv * _sigmoid(conv) * gate
    x2 = x1 + _dot(act.astype(BF16), wdown_ref[...])
    o_ref[0] = _rms(x2, gfin_ref[...]) if final else x2


def _ffn(x, ya, yb, f0, wts, tile, final):
    bn, t, _ = x.shape
    hist = FFN_CONV - 1
    wout, g, wup, cw, cb, wdown, gfin = wts
    const = lambda b, i: (0, 0)
    tok = lambda n: pl.BlockSpec((1, tile, n), lambda b, i: (b, i, 0))
    resident = lambda a: pl.BlockSpec(a.shape, const, pipeline_mode=pl.Buffered(1))
    fspec = pl.BlockSpec((1, hist, D_FF), lambda b, i: (b, 0, 0))
    return pl.pallas_call(
        functools.partial(_ffn_kernel, tile=tile, final=final),
        grid=(bn, t // tile),
        in_specs=[
            tok(D_MODEL), tok(A_WIDTH), tok(B_WIDTH), fspec,
            resident(wout), resident(g), resident(wup), resident(cw), resident(cb),
            resident(wdown), resident(gfin),
        ],
        out_specs=[tok(D_MODEL), fspec],
        out_shape=[
            jax.ShapeDtypeStruct((bn, t, D_MODEL), F32),
            jax.ShapeDtypeStruct((bn, hist, D_FF), F32),
        ],
        scratch_shapes=[pltpu.VMEM((CARRY + tile, D_FF), F32)],
        compiler_params=pltpu.CompilerParams(
            dimension_semantics=("arbitrary", "arbitrary"), vmem_limit_bytes=VMEM_LIMIT),
        name="out_ffn",
    )(x, ya, yb, f0, wout, g, wup, cw, cb, wdown, gfin)


def _layer(x, st, lw, *, in_tile, mix_chunk, ffn_tile, final):
    shift0, s0, bconv0, c0, n0, m0, fconv0 = st
    bn, t, _ = x.shape
    pa, pb, pg = _in_proj(x.reshape(bn * t, D_MODEL), lw["norm_mix"], lw["w_in"], in_tile)
    pa = pa.reshape(bn, t, A_COLS)
    pb = pb.reshape(bn, t, B_MAIN)
    pg = pg.reshape(bn, t, GATE_PAD)
    ya, shift1, s1 = _rwkv(pa, shift0, s0, lw["rwkv"], lw["seg"], mix_chunk)
    yb, bconv1, c1, n1, m1 = _mlstm(pb, pg, bconv0, c0, n0, m0, lw["mlstm"], mix_chunk)
    x, fconv1 = _ffn(x, ya, yb, fconv0, lw["ffn"], ffn_tile, final)
    return x, (shift1, s1, bconv1, c1, n1, m1, fconv1)


def kernel(x_prompt, x_sample, state_rwkv_shift, state_rwkv_wkv, state_mlstm_conv, state_mlstm_C,
           state_mlstm_n, state_mlstm_m, state_ffn_conv, meta_tokens, norm_mix, w_in, a_mu, a_w0, a_w2,
           a_a0, a_a2, a_g2, a_k_k, a_k_a, a_r_k, a_ln_w, a_ln_b, b_conv_w, b_conv_b, b_i_bias, b_f_bias,
           b_hn_w, w_out, norm_ffn, w_up, ffn_conv_w, ffn_conv_b, w_down, norm_final):
    n_prompt = x_prompt.shape[0]
    n_sample = x_sample.shape[0]
    assert x_sample.shape[1] == N_META

    head_of_lane = jnp.arange(A_WIDTH) // A_HEAD_DIM
    seg = (head_of_lane[:, None] == head_of_lane[None, :]).astype(F32)
    row = lambda a: a.reshape(1, -1)

    def layer_weights(l):
        gate_w = jnp.zeros((D_MODEL, GATE_PAD), F32).at[:, :2 * B_HEADS].set(w_in[l][:, A_COLS + B_MAIN:])
        gate_b = jnp.zeros((1, GATE_PAD), F32).at[0, :B_HEADS].set(b_i_bias[l])
        gate_b = gate_b.at[0, B_HEADS:2 * B_HEADS].set(b_f_bias[l])
        return {
            "norm_mix": row(norm_mix[l]),
            "w_in": jnp.concatenate([w_in[l][:, :A_COLS + B_MAIN], gate_w], axis=1).astype(BF16),
            "rwkv": (row(a_mu[l]), row(a_w0[l]), a_w2[l], row(a_a0[l]), a_a2[l], a_g2[l],
                     row(a_k_k[l]), row(a_k_a[l]), row(a_r_k[l]), row(a_ln_w[l]), row(a_ln_b[l])),
            "seg": seg,
            "mlstm": (b_conv_w[l], row(b_conv_b[l]), gate_b, row(b_hn_w[l])),
            "ffn": (w_out[l].astype(BF16), row(norm_ffn[l]), w_up[l].astype(BF16), ffn_conv_w[l],
                    row(ffn_conv_b[l]), w_down[l].astype(BF16), row(norm_final)),
        }

    def lead_state(s, l):
        return jnp.concatenate([jnp.zeros((n_prompt,) + s.shape[2:], s.dtype), s[l]], axis=0)

    meta = jnp.broadcast_to(meta_tokens[None].astype(x_prompt.dtype), (n_prompt, N_META, D_MODEL))
    x_lead = jnp.concatenate([meta, x_sample], axis=0)
    x_main = x_prompt
    n_lead = n_prompt + n_sample

    prompt_states, sample_states = [], []
    for l in range(DEPTH):
        lw = layer_weights(l)
        final = l == DEPTH - 1
        st_lead = (
            lead_state(state_rwkv_shift, l)[:, None, :],
            lead_state(state_rwkv_wkv, l),
            lead_state(state_mlstm_conv, l),
            lead_state(state_mlstm_C, l),
            lead_state(state_mlstm_n, l),
            jnp.broadcast_to(lead_state(state_mlstm_m, l)[..., None], (n_lead, B_HEADS, B_HEAD_DIM)),
            lead_state(state_ffn_conv, l),
        )
        x_lead, st_lead = _layer(x_lead, st_lead, lw, in_tile=n_lead * N_META, mix_chunk=N_META,
                                 ffn_tile=N_META, final=final)
        st_main = tuple(s[:n_prompt] for s in st_lead)
        x_main, st_main = _layer(x_main, st_main, lw, in_tile=256, mix_chunk=64, ffn_tile=256, final=final)
        prompt_states.append(st_main)
        sample_states.append(tuple(s[n_prompt:] for s in st_lead))

    def collect(per_layer):
        shift, wkv, bconv, c, n, m, fconv = (jnp.stack([st[i] for st in per_layer]) for i in range(7))
        return (shift[:, :, 0, :], wkv, bconv, c, n, m[..., 0], fconv)

    return (x_main, x_lead[n_prompt:]) + collect(prompt_states) + collect(sample_states)
```

```python
import functools

import numpy as np

import jax
import jax.numpy as jnp
from jax import lax
from jax.experimental import pallas as pl
from jax.experimental.pallas import tpu as pltpu

D_MODEL = 1024
DEPTH = 2
N_META = 16
A_HEADS = 8
A_HEAD_DIM = 64
A_WIDTH = 512
A_DECAY_LORA = 64
A_AAA_LORA = 64
A_GATE_LORA = 128
A_COLS = 1792
B_HEADS = 4
B_HEAD_DIM = 128
B_WIDTH = 512
B_CONV = 4
B_MAIN = 4 * B_WIDTH
GATE_PAD = 128
D_FF = 2816
FFN_CONV = 3
RMS_EPS = 1e-6
GN_EPS = 64e-5
CARRY = 8
MXU_DIM = 256
RWKV_STREAMS = 4
VMEM_LIMIT = 56 * 1024 * 1024

F32 = jnp.float32
BF16 = jnp.bfloat16
HI = lax.Precision.HIGHEST
NT_DIMS = (((1,), (1,)), ((), ()))
TN_DIMS = (((0,), (0,)), ((), ()))


def _dot_hi(a, b):
    return jnp.dot(a, b, precision=HI, preferred_element_type=F32)


def _dot(a, b):
    return jnp.dot(a, b, preferred_element_type=F32)


def _dot_nt(a, b):
    return lax.dot_general(a, b, NT_DIMS, preferred_element_type=F32)


def _dot_tn(a, b):
    return lax.dot_general(a, b, TN_DIMS, preferred_element_type=F32)


def _sigmoid(x):
    return 1.0 / (1.0 + jnp.exp(-x))


def _softplus(x):
    return jnp.maximum(x, 0.0) + jnp.log(1.0 + jnp.exp(-jnp.abs(x)))


def _rms(x, g):
    return x * lax.rsqrt(jnp.mean(x * x, axis=-1, keepdims=True) + RMS_EPS) * g


def _tril(n, strict):
    row = lax.broadcasted_iota(jnp.int32, (n, n), 0)
    col = lax.broadcasted_iota(jnp.int32, (n, n), 1)
    return (col < row) if strict else (col <= row)


def _in_proj_kernel(x_ref, g_ref, w_ref, oa_ref, ob_ref, og_ref):
    h = _rms(x_ref[...], g_ref[...]).astype(BF16)
    oa_ref[...] = _dot(h, w_ref[:, 0:A_COLS])
    ob_ref[...] = _dot(h, w_ref[:, A_COLS:A_COLS + B_MAIN])
    og_ref[...] = _dot(h, w_ref[:, A_COLS + B_MAIN:])


def _in_proj(x2d, g, w, tm):
    m = x2d.shape[0]
    n = w.shape[1]
    const = lambda i: (0, 0)
    return pl.pallas_call(
        _in_proj_kernel,
        grid=(m // tm,),
        in_specs=[
            pl.BlockSpec((tm, D_MODEL), lambda i: (i, 0)),
            pl.BlockSpec((1, D_MODEL), const),
            pl.BlockSpec((D_MODEL, n), const, pipeline_mode=pl.Buffered(1)),
        ],
        out_specs=[
            pl.BlockSpec((tm, A_COLS), lambda i: (i, 0)),
            pl.BlockSpec((tm, B_MAIN), lambda i: (i, 0)),
            pl.BlockSpec((tm, GATE_PAD), lambda i: (i, 0)),
        ],
        out_shape=[
            jax.ShapeDtypeStruct((m, A_COLS), F32),
            jax.ShapeDtypeStruct((m, B_MAIN), F32),
            jax.ShapeDtypeStruct((m, GATE_PAD), F32),
        ],
        compiler_params=pltpu.CompilerParams(
            dimension_semantics=("arbitrary",), vmem_limit_bytes=VMEM_LIMIT),
        name="in_proj",
    )(x2d, g, w)


def _rwkv_heads_per_group(chunk):
    return min(A_HEADS, MXU_DIM // chunk)


def _rwkv_consts(chunk, streams):
    hg = _rwkv_heads_per_group(chunk)
    hl, w, rows = hg * chunk, hg * A_HEAD_DIM, streams * chunk
    ix = lambda n: (np.arange(n)[:, None], np.arange(n)[None, :])
    r, c = ix(MXU_DIM)
    seg = r // A_HEAD_DIM == c // A_HEAD_DIM
    r, c = ix(rows)
    tril = (r // chunk == c // chunk) & (c <= r)
    r, c = np.arange(2 * chunk)[:, None], np.arange(hl)[None, :]
    amask = np.where(r < chunk, c % chunk < r, c % chunk <= r - chunk)
    r, c = np.arange(chunk)[:, None], np.arange(hl)[None, :]
    eye = c % chunk == r
    r, c = np.arange(hl)[:, None], np.arange(w)[None, :]
    bdl = r // chunk == c // A_HEAD_DIM
    r, c = ix(hl)
    bdsq = r // chunk == c // chunk
    r, c = ix(w)
    smask = r // A_HEAD_DIM == c // A_HEAD_DIM
    return (jnp.asarray(seg, BF16), jnp.asarray(tril, BF16), jnp.asarray(amask, F32), jnp.asarray(eye, F32),
            jnp.asarray(bdl, BF16), jnp.asarray(bdsq, BF16), jnp.asarray(smask, F32))


def _rwkv_kernel(p_ref, shift_ref, s0_ref, mu_ref, w0_ref, w2_ref, a0_ref, a2_ref, g2_ref,
                 kk_ref, ka_ref, rk_ref, lnw_ref, lnb_ref,
                 seg_ref, tril_ref, amask_ref, eye_ref, bdl_ref, bdsq_ref, smask_ref,
                 y_ref, shift_out_ref, s_out_ref, pbuf, bds, *, chunk, hg):
    L = chunk
    nb = p_ref.shape[0]
    rows = nb * L
    hl = hg * L
    wid = hg * A_HEAD_DIM
    groups = A_HEADS // hg
    c = pl.program_id(1)
    diag = lambda h: (h // hg, slice((h % hg) * A_HEAD_DIM, (h % hg + 1) * A_HEAD_DIM))

    @pl.when(c == 0)
    def _():
        bds[...] = jnp.zeros(bds.shape, F32)
        for b in range(nb):
            pbuf[b, CARRY - 1:CARRY, :] = shift_ref[b]
            for h in range(A_HEADS):
                g, sl = diag(h)
                bds[b, g, sl, sl] = s0_ref[b, h]

    ps, prevs = [], []
    for b in range(nb):
        pb = p_ref[b]
        pbuf[b, CARRY:CARRY + L, :] = pb
        prevs.append(pbuf[b, CARRY - 1:CARRY - 1 + L, :])
        pbuf[b, CARRY - 1:CARRY, :] = pb[L - 1:L, :]
        shift_out_ref[b] = pb[L - 1:L, :]
        ps.append(pb)
    p = jnp.concatenate(ps, axis=0)
    pm = p + (jnp.concatenate(prevs, axis=0) - p) * mu_ref[...]

    i1, i2, i3 = A_WIDTH, 2 * A_WIDTH, 3 * A_WIDTH
    i4, i5 = i3 + A_DECAY_LORA, i3 + A_DECAY_LORA + A_AAA_LORA
    r, k, v = pm[:, :i1], pm[:, i1:i2], pm[:, i2:i3]
    wl, al, gl = pm[:, i3:i4], pm[:, i4:i5], pm[:, i5:]
    seg = seg_ref[...]

    def segsum(t):
        half = A_WIDTH // 2
        stacked = jnp.concatenate([t[:, :half], t[:, half:]], axis=0).astype(BF16)
        out = _dot(stacked, seg)
        return jnp.concatenate([out[:rows], out[rows:]], axis=1)

    w = -_softplus(-(w0_ref[...] + _dot(jnp.tanh(wl).astype(BF16), w2_ref[...]))) - 0.5
    lw = -jnp.exp(w)
    a = _sigmoid(a0_ref[...] + _dot(al.astype(BF16), a2_ref[...]))
    gate = _dot(_sigmoid(gl).astype(BF16), g2_ref[...])
    kk = k * kk_ref[...]
    kk = kk * lax.rsqrt(segsum(kk * kk) + 1e-12)
    kmod = k * (1.0 + (a - 1.0) * ka_ref[...])
    bvec = kk * a

    lw_hi = lw.astype(BF16)
    lw_lo = (lw - lw_hi.astype(F32)).astype(BF16)
    tril = tril_ref[...]
    cum = _dot(tril, lw_hi) + _dot(tril, lw_lo)
    g_inv = jnp.exp(-cum)
    kkg_all = (kk * jnp.exp(cum - lw)).astype(BF16)
    rg_all = (r * jnp.exp(cum)).astype(BF16)
    kd_all = (kmod * g_inv).astype(BF16)
    bd_all = (bvec * g_inv).astype(BF16)
    v_all = v.astype(BF16)

    keep = amask_ref[...] > 0.5
    eye = eye_ref[...]
    bdl = bdl_ref[...]
    bdsq = bdsq_ref[...]
    smask = smask_ref[...]
    lane_bd = lambda t: jnp.concatenate([t] * hg, axis=0) * bdl
    sq_bd = lambda t: jnp.concatenate([t] * hg, axis=0) * bdsq

    chains = [(b, g) for b in range(nb) for g in range(groups)]
    rs = lambda b: slice(b * L, (b + 1) * L)
    ls = lambda g: slice(g * wid, (g + 1) * wid)
    each = lambda f: [f(i, b, g) for i, (b, g) in enumerate(chains)]

    cum_last = [cum[(b + 1) * L - 1:(b + 1) * L] for b in range(nb)]
    g_tail = [jnp.exp(cum_last[b] - cum[rs(b)]) for b in range(nb)]
    kt = [(kmod[rs(b)] * g_tail[b]).astype(BF16) for b in range(nb)]
    bt = [(bvec[rs(b)] * g_tail[b]).astype(BF16) for b in range(nb)]
    g_last = [jnp.exp(cum_last[b]) for b in range(nb)]

    kkg = each(lambda i, b, g: kkg_all[rs(b), ls(g)])
    rg = each(lambda i, b, g: rg_all[rs(b), ls(g)])
    v16 = each(lambda i, b, g: v_all[rs(b), ls(g)])
    lhs = each(lambda i, b, g: jnp.concatenate([kkg[i], rg[i]], axis=0))
    a_k = each(lambda i, b, g: jnp.where(keep, _dot_nt(lhs[i], lane_bd(kd_all[rs(b), ls(g)])), 0.0))
    a_b = each(lambda i, b, g: jnp.where(keep, _dot_nt(lhs[i], lane_bd(bd_all[rs(b), ls(g)])), 0.0))
    a_kk = each(lambda i, b, g: a_k[i][:L].astype(BF16))
    a_rk = each(lambda i, b, g: a_k[i][L:].astype(BF16))
    a_rb = each(lambda i, b, g: a_b[i][L:].astype(BF16))

    npow = each(lambda i, b, g: -a_b[i][:L])
    tinv = each(lambda i, b, g: eye + npow[i])
    n16 = each(lambda i, b, g: npow[i].astype(BF16))
    npow = each(lambda i, b, g: _dot(n16[i], sq_bd(n16[i])))
    span = 2
    while span < L:
        n16 = each(lambda i, b, g: npow[i].astype(BF16))
        t_bd = each(lambda i, b, g: sq_bd(tinv[i].astype(BF16)))
        if 2 * span < L:
            both = each(lambda i, b, g: _dot(n16[i], jnp.concatenate([sq_bd(n16[i]), t_bd[i]], axis=1)))
            npow = each(lambda i, b, g: both[i][:, :hl])
            prod = each(lambda i, b, g: both[i][:, hl:])
        else:
            prod = each(lambda i, b, g: _dot(n16[i], t_bd[i]))
        tinv = each(lambda i, b, g: tinv[i] + prod[i])
        span *= 2

    s16 = each(lambda i, b, g: bds[b, g].astype(BF16))
    v_bd = each(lambda i, b, g: lane_bd(v16[i]))
    rhs = each(lambda i, b, g: (_dot_nt(kkg[i], s16[i]) + _dot(a_kk[i], v_bd[i])).astype(BF16))
    u16 = each(lambda i, b, g: _dot(tinv[i].astype(BF16), lane_bd(rhs[i])).astype(BF16))
    yc_ = each(lambda i, b, g: _dot_nt(rg[i], s16[i]) + _dot(a_rk[i], v_bd[i]) - _dot(a_rb[i], lane_bd(u16[i])))
    upd = each(lambda i, b, g: _dot_tn(jnp.concatenate([v16[i], -u16[i]], axis=0),
                                       jnp.concatenate([kt[b][:, ls(g)], bt[b][:, ls(g)]], axis=0)))
    for i, (b, g) in enumerate(chains):
        bds[b, g] = bds[b, g] * g_last[b][:, ls(g)] + upd[i] * smask
    ys = [jnp.concatenate(yc_[b * groups:(b + 1) * groups], axis=1) for b in range(nb)]
    y = jnp.concatenate(ys, axis=0)

    inv_d = 1.0 / A_HEAD_DIM
    yc = y - segsum(y) * inv_d
    var = segsum(yc * yc) * inv_d
    yn = yc * lax.rsqrt(var + GN_EPS) * lnw_ref[...] + lnb_ref[...]
    bonus = segsum(r * kmod * rk_ref[...]) * v
    out = (yn + bonus) * gate
    for b in range(nb):
        y_ref[b] = out[b * L:(b + 1) * L]

    @pl.when(c == pl.num_programs(1) - 1)
    def _():
        for b in range(nb):
            for h in range(A_HEADS):
                g, sl = diag(h)
                s_out_ref[b, h] = bds[b, g, sl, sl]


def _rwkv(pa, shift0, s0, wts, chunk):
    bn, t, _ = pa.shape
    nb = RWKV_STREAMS
    hg = _rwkv_heads_per_group(chunk)
    consts = _rwkv_consts(chunk, nb)
    full = lambda a: pl.BlockSpec(a.shape, lambda i, c: (0,) * a.ndim)
    state = pl.BlockSpec((nb, A_HEADS, A_HEAD_DIM, A_HEAD_DIM), lambda i, c: (i, 0, 0, 0))
    shift = pl.BlockSpec((nb, 1, A_COLS), lambda i, c: (i, 0, 0))
    return pl.pallas_call(
        functools.partial(_rwkv_kernel, chunk=chunk, hg=hg),
        grid=(bn // nb, t // chunk),
        in_specs=[pl.BlockSpec((nb, chunk, A_COLS), lambda i, c: (i, c, 0)), shift, state]
        + [full(a) for a in wts] + [full(a) for a in consts],
        out_specs=[pl.BlockSpec((nb, chunk, A_WIDTH), lambda i, c: (i, c, 0)), shift, state],
        out_shape=[
            jax.ShapeDtypeStruct((bn, t, A_WIDTH), F32),
            jax.ShapeDtypeStruct((bn, 1, A_COLS), F32),
            jax.ShapeDtypeStruct((bn, A_HEADS, A_HEAD_DIM, A_HEAD_DIM), F32),
        ],
        scratch_shapes=[
            pltpu.VMEM((nb, CARRY + chunk, A_COLS), F32),
            pltpu.VMEM((nb, A_HEADS // hg, hg * A_HEAD_DIM, hg * A_HEAD_DIM), F32),
        ],
        compiler_params=pltpu.CompilerParams(
            dimension_semantics=("arbitrary", "arbitrary"), vmem_limit_bytes=VMEM_LIMIT),
        name="rwkv_mix",
    )(pa, shift0, s0, *wts, *consts)


def _mlstm_kernel(pb_ref, pg_ref, conv0_ref, c0_ref, n0_ref, m0_ref,
                  cw_ref, cb_ref, gb_ref, hnw_ref,
                  y_ref, conv_out_ref, c_out_ref, n_out_ref, m_out_ref,
                  xbuf, c_sc, n_sc, m_sc, *, chunk):
    L = chunk
    c = pl.program_id(1)
    hist = B_CONV - 1

    @pl.when(c == 0)
    def _():
        xbuf[CARRY - hist:CARRY, :] = conv0_ref[0]
        c_sc[...] = c0_ref[0]
        n_sc[...] = n0_ref[0]
        m_sc[...] = m0_ref[0]

    x = pb_ref[0, :, 0:2 * B_WIDTH]
    xbuf[CARRY:CARRY + L, :] = x
    qk = cb_ref[...] + cw_ref[hist:hist + 1, :] * x
    for j in range(hist):
        qk = qk + cw_ref[j:j + 1, :] * xbuf[CARRY - hist + j:CARRY - hist + j + L, :]
    tail = xbuf[CARRY + L - hist:CARRY + L, :]
    xbuf[CARRY - hist:CARRY, :] = tail
    conv_out_ref[0] = tail
    qk = qk * _sigmoid(qk)

    gates = pg_ref[0] + gb_ref[...]
    lane = lax.broadcasted_iota(jnp.int32, (L, GATE_PAD), 1)
    lf = -_softplus(-gates)
    cum = _dot_hi(_tril(L, False).astype(F32), lf)
    gcols = jnp.where(lane < B_HEADS, gates, cum)
    grows = gcols.T
    causal = _tril(L, False)

    for h in range(B_HEADS):
        lo = h * B_HEAD_DIM
        q = qk[:, lo:lo + B_HEAD_DIM]
        k = qk[:, B_WIDTH + lo:B_WIDTH + lo + B_HEAD_DIM] * (B_HEAD_DIM ** -0.5)
        v = pb_ref[0, :, 2 * B_WIDTH + lo:2 * B_WIDTH + lo + B_HEAD_DIM]
        o = pb_ref[0, :, 3 * B_WIDTH + lo:3 * B_WIDTH + lo + B_HEAD_DIM]
        li_col = gcols[:, h:h + 1]
        b_col = gcols[:, B_HEADS + h:B_HEADS + h + 1]
        li_row = grows[h:h + 1, :]
        b_row = grows[B_HEADS + h:B_HEADS + h + 1, :]
        c_prev = c_sc[h]
        n_prev = n_sc[h:h + 1, :]
        m_prev = m_sc[h:h + 1, 0:1]

        dm = jnp.where(causal, b_col - b_row + li_row, -jnp.inf)
        inter = b_col + m_prev
        mt = jnp.maximum(inter, jnp.max(dm, axis=-1, keepdims=True))
        wi = jnp.exp(dm - mt)
        wo = jnp.exp(inter - mt)
        s = _dot_nt(q, k) * wi
        num = wo * _dot(q, c_prev) + _dot(s, v)
        den = wo * jnp.sum(q * n_prev, axis=-1, keepdims=True) + jnp.sum(s, axis=-1, keepdims=True)
        hh = num / jnp.maximum(jnp.abs(den), jnp.exp(-mt))

        m_new = mt[L - 1:L, :]
        b_last = b_col[L - 1:L, :]
        ws = jnp.exp(b_last - b_col + li_col - m_new)
        dec = jnp.exp(b_last + m_prev - m_new)
        kw = k * ws
        c_sc[h] = dec * c_prev + _dot_tn(kw, v)
        n_sc[h:h + 1, :] = dec * n_prev + jnp.sum(kw, axis=0, keepdims=True)
        m_sc[h:h + 1, :] = jnp.broadcast_to(m_new, (1, B_HEAD_DIM))

        hh = hh * lax.rsqrt(jnp.mean(hh * hh, axis=-1, keepdims=True) + RMS_EPS)
        y_ref[0, :, lo:lo + B_HEAD_DIM] = hh * hnw_ref[:, lo:lo + B_HEAD_DIM] * _sigmoid(o)

    c_out_ref[0] = c_sc[...]
    n_out_ref[0] = n_sc[...]
    m_out_ref[0] = m_sc[...]


def _mlstm(pb, pg, conv0, c0, n0, m0, wts, chunk):
    bn, t, _ = pb.shape
    hist = B_CONV - 1
    const = lambda b, c: (0, 0)
    per_b3 = lambda b, c: (b, 0, 0)
    per_b4 = lambda b, c: (b, 0, 0, 0)
    conv_spec = pl.BlockSpec((1, hist, 2 * B_WIDTH), per_b3)
    c_spec = pl.BlockSpec((1, B_HEADS, B_HEAD_DIM, B_HEAD_DIM), per_b4)
    nm_spec = pl.BlockSpec((1, B_HEADS, B_HEAD_DIM), per_b3)
    return pl.pallas_call(
        functools.partial(_mlstm_kernel, chunk=chunk),
        grid=(bn, t // chunk),
        in_specs=[
            pl.BlockSpec((1, chunk, B_MAIN), lambda b, c: (b, c, 0)),
            pl.BlockSpec((1, chunk, GATE_PAD), lambda b, c: (b, c, 0)),
            conv_spec, c_spec, nm_spec, nm_spec,
            pl.BlockSpec((B_CONV, 2 * B_WIDTH), const),
            pl.BlockSpec((1, 2 * B_WIDTH), const),
            pl.BlockSpec((1, GATE_PAD), const),
            pl.BlockSpec((1, B_WIDTH), const),
        ],
        out_specs=[
            pl.BlockSpec((1, chunk, B_WIDTH), lambda b, c: (b, c, 0)),
            conv_spec, c_spec, nm_spec, nm_spec,
        ],
        out_shape=[
            jax.ShapeDtypeStruct((bn, t, B_WIDTH), F32),
            jax.ShapeDtypeStruct((bn, hist, 2 * B_WIDTH), F32),
            jax.ShapeDtypeStruct((bn, B_HEADS, B_HEAD_DIM, B_HEAD_DIM), F32),
            jax.ShapeDtypeStruct((bn, B_HEADS, B_HEAD_DIM), F32),
            jax.ShapeDtypeStruct((bn, B_HEADS, B_HEAD_DIM), F32),
        ],
        scratch_shapes=[
            pltpu.VMEM((CARRY + chunk, 2 * B_WIDTH), F32),
            pltpu.VMEM((B_HEADS, B_HEAD_DIM, B_HEAD_DIM), F32),
            pltpu.VMEM((B_HEADS, B_HEAD_DIM), F32),
            pltpu.VMEM((B_HEADS, B_HEAD_DIM), F32),
        ],
        compiler_params=pltpu.CompilerParams(
            dimension_semantics=("arbitrary", "arbitrary"), vmem_limit_bytes=VMEM_LIMIT),
        name="mlstm_mix",
    )(pb, pg, conv0, c0, n0, m0, *wts)


def _ffn_kernel(x_ref, ya_ref, yb_ref, f0_ref, wout_ref, g_ref, wup_ref, cw_ref, cb_ref,
                wdown_ref, gfin_ref, o_ref, fout_ref, ubuf, *, tile, final):
    t = pl.program_id(1)
    hist = FFN_CONV - 1

    @pl.when(t == 0)
    def _():
        ubuf[CARRY - hist:CARRY, :] = f0_ref[0]

    x1 = (x_ref[0]
          + _dot(ya_ref[0].astype(BF16), wout_ref[0:A_WIDTH, :])
          + _dot(yb_ref[0].astype(BF16), wout_ref[A_WIDTH:, :]))
    h2 = _rms(x1, g_ref[...]).astype(BF16)
    u = _dot(h2, wup_ref[:, 0:D_FF])
    gate = _dot(h2, wup_ref[:, D_FF:])
    ubuf[CARRY:CARRY + tile, :] = u
    conv = cb_ref[...] + cw_ref[hist:hist + 1, :] * u
    for j in range(hist):
        conv = conv + cw_ref[j:j + 1, :] * ubuf[CARRY - hist + j:CARRY - hist + j + tile, :]
    tail = ubuf[CARRY + tile - hist:CARRY + tile, :]
    ubuf[CARRY - hist:CARRY, :] = tail
    fout_ref[0] = tail
    act = conv * _sigmoid(conv) * gate
    x2 = x1 + _dot(act.astype(BF16), wdown_ref[...])
    o_ref[0] = _rms(x2, gfin_ref[...]) if final else x2


def _ffn(x, ya, yb, f0, wts, tile, final):
    bn, t, _ = x.shape
    hist = FFN_CONV - 1
    wout, g, wup, cw, cb, wdown, gfin = wts
    const = lambda b, i: (0, 0)
    tok = lambda n: pl.BlockSpec((1, tile, n), lambda b, i: (b, i, 0))
    resident = lambda a: pl.BlockSpec(a.shape, const, pipeline_mode=pl.Buffered(1))
    fspec = pl.BlockSpec((1, hist, D_FF), lambda b, i: (b, 0, 0))
    return pl.pallas_call(
        functools.partial(_ffn_kernel, tile=tile, final=final),
        grid=(bn, t // tile),
        in_specs=[
            tok(D_MODEL), tok(A_WIDTH), tok(B_WIDTH), fspec,
            resident(wout), resident(g), resident(wup), resident(cw), resident(cb),
            resident(wdown), resident(gfin),
        ],
        out_specs=[tok(D_MODEL), fspec],
        out_shape=[
            jax.ShapeDtypeStruct((bn, t, D_MODEL), F32),
            jax.ShapeDtypeStruct((bn, hist, D_FF), F32),
        ],
        scratch_shapes=[pltpu.VMEM((CARRY + tile, D_FF), F32)],
        compiler_params=pltpu.CompilerParams(
            dimension_semantics=("arbitrary", "arbitrary"), vmem_limit_bytes=VMEM_LIMIT),
        name="out_ffn",
    )(x, ya, yb, f0, wout, g, wup, cw, cb, wdown, gfin)


def _layer(x, st, lw, *, in_tile, mix_chunk, ffn_tile, final):
    shift0, s0, bconv0, c0, n0, m0, fconv0 = st
    bn, t, _ = x.shape
    pa, pb, pg = _in_proj(x.reshape(bn * t, D_MODEL), lw["norm_mix"], lw["w_in"], in_tile)
    pa = pa.reshape(bn, t, A_COLS)
    pb = pb.reshape(bn, t, B_MAIN)
    pg = pg.reshape(bn, t, GATE_PAD)
    ya, shift1, s1 = _rwkv(pa, shift0, s0, lw["rwkv"], mix_chunk)
    yb, bconv1, c1, n1, m1 = _mlstm(pb, pg, bconv0, c0, n0, m0, lw["mlstm"], mix_chunk)
    x, fconv1 = _ffn(x, ya, yb, fconv0, lw["ffn"], ffn_tile, final)
    return x, (shift1, s1, bconv1, c1, n1, m1, fconv1)


def kernel(x_prompt, x_sample, state_rwkv_shift, state_rwkv_wkv, state_mlstm_conv, state_mlstm_C,
           state_mlstm_n, state_mlstm_m, state_ffn_conv, meta_tokens, norm_mix, w_in, a_mu, a_w0, a_w2,
           a_a0, a_a2, a_g2, a_k_k, a_k_a, a_r_k, a_ln_w, a_ln_b, b_conv_w, b_conv_b, b_i_bias, b_f_bias,
           b_hn_w, w_out, norm_ffn, w_up, ffn_conv_w, ffn_conv_b, w_down, norm_final):
    n_prompt = x_prompt.shape[0]
    n_sample = x_sample.shape[0]
    assert x_sample.shape[1] == N_META
    assert n_prompt % RWKV_STREAMS == 0 and (n_prompt + n_sample) % RWKV_STREAMS == 0

    row = lambda a: a.reshape(1, -1)

    def layer_weights(l):
        gate_w = jnp.zeros((D_MODEL, GATE_PAD), F32).at[:, :2 * B_HEADS].set(w_in[l][:, A_COLS + B_MAIN:])
        gate_b = jnp.zeros((1, GATE_PAD), F32).at[0, :B_HEADS].set(b_i_bias[l])
        gate_b = gate_b.at[0, B_HEADS:2 * B_HEADS].set(b_f_bias[l])
        return {
            "norm_mix": row(norm_mix[l]),
            "w_in": jnp.concatenate([w_in[l][:, :A_COLS + B_MAIN], gate_w], axis=1).astype(BF16),
            "rwkv": (row(a_mu[l]), row(a_w0[l]), a_w2[l].astype(BF16), row(a_a0[l]), a_a2[l].astype(BF16),
                     a_g2[l].astype(BF16), row(a_k_k[l]), row(a_k_a[l]), row(a_r_k[l]), row(a_ln_w[l]),
                     row(a_ln_b[l])),
            "mlstm": (b_conv_w[l], row(b_conv_b[l]), gate_b, row(b_hn_w[l])),
            "ffn": (w_out[l].astype(BF16), row(norm_ffn[l]), w_up[l].astype(BF16), ffn_conv_w[l],
                    row(ffn_conv_b[l]), w_down[l].astype(BF16), row(norm_final)),
        }

    def lead_state(s, l):
        return jnp.concatenate([jnp.zeros((n_prompt,) + s.shape[2:], s.dtype), s[l]], axis=0)

    meta = jnp.broadcast_to(meta_tokens[None].astype(x_prompt.dtype), (n_prompt, N_META, D_MODEL))
    x_lead = jnp.concatenate([meta, x_sample], axis=0)
    x_main = x_prompt
    n_lead = n_prompt + n_sample

    prompt_states, sample_states = [], []
    for l in range(DEPTH):
        lw = layer_weights(l)
        final = l == DEPTH - 1
        st_lead = (
            lead_state(state_rwkv_shift, l)[:, None, :],
            lead_state(state_rwkv_wkv, l),
            lead_state(state_mlstm_conv, l),
            lead_state(state_mlstm_C, l),
            lead_state(state_mlstm_n, l),
            jnp.broadcast_to(lead_state(state_mlstm_m, l)[..., None], (n_lead, B_HEADS, B_HEAD_DIM)),
            lead_state(state_ffn_conv, l),
        )
        x_lead, st_lead = _layer(x_lead, st_lead, lw, in_tile=n_lead * N_META, mix_chunk=N_META,
                                 ffn_tile=N_META, final=final)
        st_main = tuple(s[:n_prompt] for s in st_lead)
        x_main, st_main = _layer(x_main, st_main, lw, in_tile=256, mix_chunk=64, ffn_tile=256, final=final)
        prompt_states.append(st_main)
        sample_states.append(tuple(s[n_prompt:] for s in st_lead))

    def collect(per_layer):
        shift, wkv, bconv, c, n, m, fconv = (jnp.stack([st[i] for st in per_layer]) for i in range(7))
        return (shift[:, :, 0, :], wkv, bconv, c, n, m[..., 0], fconv)

    return (x_main, x_lead[n_prompt:]) + collect(prompt_states) + collect(sample_states)
```

```python
import functools

import numpy as np

import jax
import jax.numpy as jnp
from jax import lax
from jax.experimental import pallas as pl
from jax.experimental.pallas import tpu as pltpu

D_MODEL = 1024
DEPTH = 2
N_META = 16
A_HEADS = 8
A_HEAD_DIM = 64
A_WIDTH = 512
A_DECAY_LORA = 64
A_AAA_LORA = 64
A_GATE_LORA = 128
A_COLS = 1792
B_HEADS = 4
B_HEAD_DIM = 128
B_WIDTH = 512
B_CONV = 4
B_MAIN = 4 * B_WIDTH
GATE_PAD = 128
D_FF = 2816
FFN_CONV = 3
RMS_EPS = 1e-6
GN_EPS = 64e-5
CARRY = 8
MXU_DIM = 256
MIX_STREAMS = 4
VMEM_LIMIT = 56 * 1024 * 1024

F32 = jnp.float32
BF16 = jnp.bfloat16
HI = lax.Precision.HIGHEST
NT_DIMS = (((1,), (1,)), ((), ()))
TN_DIMS = (((0,), (0,)), ((), ()))


def _dot_hi(a, b):
    return jnp.dot(a, b, precision=HI, preferred_element_type=F32)


def _dot(a, b):
    return jnp.dot(a, b, preferred_element_type=F32)


def _dot_nt(a, b):
    return lax.dot_general(a, b, NT_DIMS, preferred_element_type=F32)


def _dot_tn(a, b):
    return lax.dot_general(a, b, TN_DIMS, preferred_element_type=F32)


def _sigmoid(x):
    return 1.0 / (1.0 + jnp.exp(-x))


def _softplus(x):
    return jnp.maximum(x, 0.0) + jnp.log(1.0 + jnp.exp(-jnp.abs(x)))


def _rms(x, g):
    return x * lax.rsqrt(jnp.mean(x * x, axis=-1, keepdims=True) + RMS_EPS) * g


def _tril(n, strict):
    row = lax.broadcasted_iota(jnp.int32, (n, n), 0)
    col = lax.broadcasted_iota(jnp.int32, (n, n), 1)
    return (col < row) if strict else (col <= row)


def _in_proj_kernel(x_ref, g_ref, w_ref, oa_ref, ob_ref, og_ref):
    h = _rms(x_ref[...], g_ref[...]).astype(BF16)
    oa_ref[...] = _dot(h, w_ref[:, 0:A_COLS])
    ob_ref[...] = _dot(h, w_ref[:, A_COLS:A_COLS + B_MAIN])
    og_ref[...] = _dot(h, w_ref[:, A_COLS + B_MAIN:])


def _in_proj(x2d, g, w, tm):
    m = x2d.shape[0]
    n = w.shape[1]
    const = lambda i: (0, 0)
    return pl.pallas_call(
        _in_proj_kernel,
        grid=(m // tm,),
        in_specs=[
            pl.BlockSpec((tm, D_MODEL), lambda i: (i, 0)),
            pl.BlockSpec((1, D_MODEL), const),
            pl.BlockSpec((D_MODEL, n), const, pipeline_mode=pl.Buffered(1)),
        ],
        out_specs=[
            pl.BlockSpec((tm, A_COLS), lambda i: (i, 0)),
            pl.BlockSpec((tm, B_MAIN), lambda i: (i, 0)),
            pl.BlockSpec((tm, GATE_PAD), lambda i: (i, 0)),
        ],
        out_shape=[
            jax.ShapeDtypeStruct((m, A_COLS), F32),
            jax.ShapeDtypeStruct((m, B_MAIN), F32),
            jax.ShapeDtypeStruct((m, GATE_PAD), F32),
        ],
        compiler_params=pltpu.CompilerParams(
            dimension_semantics=("arbitrary",), vmem_limit_bytes=VMEM_LIMIT),
        name="in_proj",
    )(x2d, g, w)


def _rwkv_heads_per_group(chunk):
    return min(A_HEADS, MXU_DIM // chunk)


def _rwkv_consts(chunk, streams):
    hg = _rwkv_heads_per_group(chunk)
    hl, w, rows = hg * chunk, hg * A_HEAD_DIM, streams * chunk
    ix = lambda n: (np.arange(n)[:, None], np.arange(n)[None, :])
    r, c = ix(MXU_DIM)
    seg = r // A_HEAD_DIM == c // A_HEAD_DIM
    r, c = ix(rows)
    tril = (r // chunk == c // chunk) & (c <= r)
    r, c = np.arange(2 * chunk)[:, None], np.arange(hl)[None, :]
    amask = np.where(r < chunk, c % chunk < r, c % chunk <= r - chunk)
    r, c = np.arange(chunk)[:, None], np.arange(hl)[None, :]
    eye = c % chunk == r
    r, c = np.arange(hl)[:, None], np.arange(w)[None, :]
    bdl = r // chunk == c // A_HEAD_DIM
    r, c = ix(hl)
    bdsq = r // chunk == c // chunk
    r, c = ix(w)
    smask = r // A_HEAD_DIM == c // A_HEAD_DIM
    return (jnp.asarray(seg, BF16), jnp.asarray(tril, BF16), jnp.asarray(amask, F32), jnp.asarray(eye, F32),
            jnp.asarray(bdl, BF16), jnp.asarray(bdsq, BF16), jnp.asarray(smask, F32))


def _rwkv_kernel(p_ref, shift_ref, s0_ref, mu_ref, w0_ref, w2_ref, a0_ref, a2_ref, g2_ref,
                 kk_ref, ka_ref, rk_ref, lnw_ref, lnb_ref,
                 seg_ref, tril_ref, amask_ref, eye_ref, bdl_ref, bdsq_ref, smask_ref,
                 y_ref, shift_out_ref, s_out_ref, pbuf, bds, *, chunk, hg):
    L = chunk
    nb = p_ref.shape[0]
    rows = nb * L
    hl = hg * L
    wid = hg * A_HEAD_DIM
    groups = A_HEADS // hg
    c = pl.program_id(1)
    diag = lambda h: (h // hg, slice((h % hg) * A_HEAD_DIM, (h % hg + 1) * A_HEAD_DIM))

    @pl.when(c == 0)
    def _():
        bds[...] = jnp.zeros(bds.shape, F32)
        for b in range(nb):
            pbuf[b, CARRY - 1:CARRY, :] = shift_ref[b]
            for h in range(A_HEADS):
                g, sl = diag(h)
                bds[b, g, sl, sl] = s0_ref[b, h]

    ps, prevs = [], []
    for b in range(nb):
        pb = p_ref[b]
        pbuf[b, CARRY:CARRY + L, :] = pb
        prevs.append(pbuf[b, CARRY - 1:CARRY - 1 + L, :])
        pbuf[b, CARRY - 1:CARRY, :] = pb[L - 1:L, :]
        shift_out_ref[b] = pb[L - 1:L, :]
        ps.append(pb)
    p = jnp.concatenate(ps, axis=0)
    pm = p + (jnp.concatenate(prevs, axis=0) - p) * mu_ref[...]

    i1, i2, i3 = A_WIDTH, 2 * A_WIDTH, 3 * A_WIDTH
    i4, i5 = i3 + A_DECAY_LORA, i3 + A_DECAY_LORA + A_AAA_LORA
    r, k, v = pm[:, :i1], pm[:, i1:i2], pm[:, i2:i3]
    wl, al, gl = pm[:, i3:i4], pm[:, i4:i5], pm[:, i5:]
    seg = seg_ref[...]

    def segsum(t):
        half = A_WIDTH // 2
        stacked = jnp.concatenate([t[:, :half], t[:, half:]], axis=0).astype(BF16)
        out = _dot(stacked, seg)
        return jnp.concatenate([out[:rows], out[rows:]], axis=1)

    w = -_softplus(-(w0_ref[...] + _dot(jnp.tanh(wl).astype(BF16), w2_ref[...]))) - 0.5
    lw = -jnp.exp(w)
    a = _sigmoid(a0_ref[...] + _dot(al.astype(BF16), a2_ref[...]))
    gate = _dot(_sigmoid(gl).astype(BF16), g2_ref[...])
    kk = k * kk_ref[...]
    kk = kk * lax.rsqrt(segsum(kk * kk) + 1e-12)
    kmod = k * (1.0 + (a - 1.0) * ka_ref[...])
    bvec = kk * a

    lw_hi = lw.astype(BF16)
    lw_lo = (lw - lw_hi.astype(F32)).astype(BF16)
    tril = tril_ref[...]
    cum = _dot(tril, lw_hi) + _dot(tril, lw_lo)
    g_inv = jnp.exp(-cum)
    kkg_all = (kk * jnp.exp(cum - lw)).astype(BF16)
    rg_all = (r * jnp.exp(cum)).astype(BF16)
    kd_all = (kmod * g_inv).astype(BF16)
    bd_all = (bvec * g_inv).astype(BF16)
    v_all = v.astype(BF16)

    keep = amask_ref[...] > 0.5
    eye = eye_ref[...]
    bdl = bdl_ref[...]
    bdsq = bdsq_ref[...]
    smask = smask_ref[...]
    lane_bd = lambda t: jnp.concatenate([t] * hg, axis=0) * bdl
    sq_bd = lambda t: jnp.concatenate([t] * hg, axis=0) * bdsq

    chains = [(b, g) for b in range(nb) for g in range(groups)]
    rs = lambda b: slice(b * L, (b + 1) * L)
    ls = lambda g: slice(g * wid, (g + 1) * wid)
    each = lambda f: [f(i, b, g) for i, (b, g) in enumerate(chains)]

    cum_last = [cum[(b + 1) * L - 1:(b + 1) * L] for b in range(nb)]
    g_tail = [jnp.exp(cum_last[b] - cum[rs(b)]) for b in range(nb)]
    kt = [(kmod[rs(b)] * g_tail[b]).astype(BF16) for b in range(nb)]
    bt = [(bvec[rs(b)] * g_tail[b]).astype(BF16) for b in range(nb)]
    g_last = [jnp.exp(cum_last[b]) for b in range(nb)]

    kkg = each(lambda i, b, g: kkg_all[rs(b), ls(g)])
    rg = each(lambda i, b, g: rg_all[rs(b), ls(g)])
    v16 = each(lambda i, b, g: v_all[rs(b), ls(g)])
    lhs = each(lambda i, b, g: jnp.concatenate([kkg[i], rg[i]], axis=0))
    a_k = each(lambda i, b, g: jnp.where(keep, _dot_nt(lhs[i], lane_bd(kd_all[rs(b), ls(g)])), 0.0))
    a_b = each(lambda i, b, g: jnp.where(keep, _dot_nt(lhs[i], lane_bd(bd_all[rs(b), ls(g)])), 0.0))
    a_kk = each(lambda i, b, g: a_k[i][:L].astype(BF16))
    a_rk = each(lambda i, b, g: a_k[i][L:].astype(BF16))
    a_rb = each(lambda i, b, g: a_b[i][L:].astype(BF16))

    npow = each(lambda i, b, g: -a_b[i][:L])
    tinv = each(lambda i, b, g: eye + npow[i])
    n16 = each(lambda i, b, g: npow[i].astype(BF16))
    npow = each(lambda i, b, g: _dot(n16[i], sq_bd(n16[i])))
    span = 2
    while span < L:
        n16 = each(lambda i, b, g: npow[i].astype(BF16))
        t_bd = each(lambda i, b, g: sq_bd(tinv[i].astype(BF16)))
        if 2 * span < L:
            both = each(lambda i, b, g: _dot(n16[i], jnp.concatenate([sq_bd(n16[i]), t_bd[i]], axis=1)))
            npow = each(lambda i, b, g: both[i][:, :hl])
            prod = each(lambda i, b, g: both[i][:, hl:])
        else:
            prod = each(lambda i, b, g: _dot(n16[i], t_bd[i]))
        tinv = each(lambda i, b, g: tinv[i] + prod[i])
        span *= 2

    s16 = each(lambda i, b, g: bds[b, g].astype(BF16))
    v_bd = each(lambda i, b, g: lane_bd(v16[i]))
    rhs = each(lambda i, b, g: (_dot_nt(kkg[i], s16[i]) + _dot(a_kk[i], v_bd[i])).astype(BF16))
    u16 = each(lambda i, b, g: _dot(tinv[i].astype(BF16), lane_bd(rhs[i])).astype(BF16))
    yc_ = each(lambda i, b, g: _dot_nt(rg[i], s16[i]) + _dot(a_rk[i], v_bd[i]) - _dot(a_rb[i], lane_bd(u16[i])))
    upd = each(lambda i, b, g: _dot_tn(jnp.concatenate([v16[i], -u16[i]], axis=0),
                                       jnp.concatenate([kt[b][:, ls(g)], bt[b][:, ls(g)]], axis=0)))
    for i, (b, g) in enumerate(chains):
        bds[b, g] = bds[b, g] * g_last[b][:, ls(g)] + upd[i] * smask
    ys = [jnp.concatenate(yc_[b * groups:(b + 1) * groups], axis=1) for b in range(nb)]
    y = jnp.concatenate(ys, axis=0)

    inv_d = 1.0 / A_HEAD_DIM
    yc = y - segsum(y) * inv_d
    var = segsum(yc * yc) * inv_d
    yn = yc * lax.rsqrt(var + GN_EPS) * lnw_ref[...] + lnb_ref[...]
    bonus = segsum(r * kmod * rk_ref[...]) * v
    out = (yn + bonus) * gate
    for b in range(nb):
        y_ref[b] = out[b * L:(b + 1) * L]

    @pl.when(c == pl.num_programs(1) - 1)
    def _():
        for b in range(nb):
            for h in range(A_HEADS):
                g, sl = diag(h)
                s_out_ref[b, h] = bds[b, g, sl, sl]


def _rwkv(pa, shift0, s0, wts, chunk):
    bn, t, _ = pa.shape
    nb = MIX_STREAMS
    hg = _rwkv_heads_per_group(chunk)
    consts = _rwkv_consts(chunk, nb)
    full = lambda a: pl.BlockSpec(a.shape, lambda i, c: (0,) * a.ndim)
    state = pl.BlockSpec((nb, A_HEADS, A_HEAD_DIM, A_HEAD_DIM), lambda i, c: (i, 0, 0, 0))
    shift = pl.BlockSpec((nb, 1, A_COLS), lambda i, c: (i, 0, 0))
    return pl.pallas_call(
        functools.partial(_rwkv_kernel, chunk=chunk, hg=hg),
        grid=(bn // nb, t // chunk),
        in_specs=[pl.BlockSpec((nb, chunk, A_COLS), lambda i, c: (i, c, 0)), shift, state]
        + [full(a) for a in wts] + [full(a) for a in consts],
        out_specs=[pl.BlockSpec((nb, chunk, A_WIDTH), lambda i, c: (i, c, 0)), shift, state],
        out_shape=[
            jax.ShapeDtypeStruct((bn, t, A_WIDTH), F32),
            jax.ShapeDtypeStruct((bn, 1, A_COLS), F32),
            jax.ShapeDtypeStruct((bn, A_HEADS, A_HEAD_DIM, A_HEAD_DIM), F32),
        ],
        scratch_shapes=[
            pltpu.VMEM((nb, CARRY + chunk, A_COLS), F32),
            pltpu.VMEM((nb, A_HEADS // hg, hg * A_HEAD_DIM, hg * A_HEAD_DIM), F32),
        ],
        compiler_params=pltpu.CompilerParams(
            dimension_semantics=("arbitrary", "arbitrary"), vmem_limit_bytes=VMEM_LIMIT),
        name="rwkv_mix",
    )(pa, shift0, s0, *wts, *consts)


def _mlstm_consts(chunk, streams):
    r = np.arange(streams * chunk)[:, None]
    c = np.arange(streams * chunk)[None, :]
    return (jnp.asarray((r // chunk == c // chunk) & (c <= r), BF16),)


def _split3(x):
    hi = x.astype(BF16)
    rest = x - hi.astype(F32)
    mid = rest.astype(BF16)
    lo = (rest - mid.astype(F32)).astype(BF16)
    return hi, mid, lo


def _mlstm_kernel(pb_ref, pg_ref, conv0_ref, c0_ref, n0_ref, m0_ref,
                  cw_ref, cb_ref, gb_ref, hnw_ref, tril_ref,
                  y_ref, conv_out_ref, c_ref, n_ref, m_ref, xbuf, *, chunk):
    L = chunk
    nb = pb_ref.shape[0]
    rows = nb * L
    hist = B_CONV - 1

    @pl.when(pl.program_id(1) == 0)
    def _():
        for b in range(nb):
            xbuf[b, CARRY - hist:CARRY, :] = conv0_ref[b]
        c_ref[...] = c0_ref[...]
        n_ref[...] = n0_ref[...]
        m_ref[...] = m0_ref[...]

    qk = []
    for b in range(nb):
        x = pb_ref[b, :, 0:2 * B_WIDTH]
        xbuf[b, CARRY:CARRY + L, :] = x
        acc = cb_ref[...] + cw_ref[hist:hist + 1, :] * x
        for j in range(hist):
            acc = acc + cw_ref[j:j + 1, :] * xbuf[b, CARRY - hist + j:CARRY - hist + j + L, :]
        tail = xbuf[b, CARRY + L - hist:CARRY + L, :]
        xbuf[b, CARRY - hist:CARRY, :] = tail
        conv_out_ref[b] = tail
        qk.append(acc * _sigmoid(acc))

    gates = jnp.concatenate([pg_ref[b] for b in range(nb)], axis=0) + gb_ref[...]
    lane = lax.broadcasted_iota(jnp.int32, (rows, GATE_PAD), 1)
    tril = tril_ref[...]
    cum = sum(_dot(tril, part) for part in _split3(-_softplus(-gates)))
    gcols = jnp.where(lane < B_HEADS, gates, cum)
    grows = gcols.T
    causal = _tril(L, False)

    chains = [(b, h) for b in range(nb) for h in range(B_HEADS)]
    each = lambda f: [f(i, b, h) for i, (b, h) in enumerate(chains)]
    rs = lambda b: slice(b * L, (b + 1) * L)
    hs = lambda h, base=0: slice(base + h * B_HEAD_DIM, base + (h + 1) * B_HEAD_DIM)

    q = each(lambda i, b, h: qk[b][:, hs(h)])
    k = each(lambda i, b, h: qk[b][:, hs(h, B_WIDTH)] * (B_HEAD_DIM ** -0.5))
    q16 = each(lambda i, b, h: q[i].astype(BF16))
    v16 = each(lambda i, b, h: pb_ref[b, :, hs(h, 2 * B_WIDTH)].astype(BF16))
    li_col = each(lambda i, b, h: gcols[rs(b), h:h + 1])
    b_col = each(lambda i, b, h: gcols[rs(b), B_HEADS + h:B_HEADS + h + 1])
    li_row = each(lambda i, b, h: grows[h:h + 1, rs(b)])
    b_row = each(lambda i, b, h: grows[B_HEADS + h:B_HEADS + h + 1, rs(b)])
    c_prev = each(lambda i, b, h: c_ref[b, h])
    n_prev = each(lambda i, b, h: n_ref[b, h:h + 1, :])
    m_prev = each(lambda i, b, h: m_ref[b, h:h + 1, 0:1])

    dm = each(lambda i, b, h: jnp.where(causal, b_col[i] - b_row[i] + li_row[i], -jnp.inf))
    inter = each(lambda i, b, h: b_col[i] + m_prev[i])
    mt = each(lambda i, b, h: jnp.maximum(inter[i], jnp.max(dm[i], axis=-1, keepdims=True)))
    wo = each(lambda i, b, h: jnp.exp(inter[i] - mt[i]))
    s = each(lambda i, b, h: _dot_nt(q16[i], k[i].astype(BF16)) * jnp.exp(dm[i] - mt[i]))
    num = each(lambda i, b, h: wo[i] * _dot(q16[i], c_prev[i].astype(BF16)) + _dot(s[i].astype(BF16), v16[i]))
    den = each(lambda i, b, h: wo[i] * jnp.sum(q[i] * n_prev[i], axis=-1, keepdims=True)
               + jnp.sum(s[i], axis=-1, keepdims=True))
    hh = each(lambda i, b, h: num[i] / jnp.maximum(jnp.abs(den[i]), jnp.exp(-mt[i])))

    m_new = each(lambda i, b, h: mt[i][L - 1:L, :])
    b_last = each(lambda i, b, h: b_col[i][L - 1:L, :])
    kw = each(lambda i, b, h: k[i] * jnp.exp(b_last[i] - b_col[i] + li_col[i] - m_new[i]))
    dec = each(lambda i, b, h: jnp.exp(b_last[i] + m_prev[i] - m_new[i]))
    c_new = each(lambda i, b, h: dec[i] * c_prev[i] + _dot_tn(kw[i].astype(BF16), v16[i]))
    n_new = each(lambda i, b, h: dec[i] * n_prev[i] + jnp.sum(kw[i], axis=0, keepdims=True))
    out = each(lambda i, b, h: hh[i] * lax.rsqrt(jnp.mean(hh[i] * hh[i], axis=-1, keepdims=True) + RMS_EPS)
               * hnw_ref[:, hs(h)] * _sigmoid(pb_ref[b, :, hs(h, 3 * B_WIDTH)]))
    for i, (b, h) in enumerate(chains):
        c_ref[b, h] = c_new[i]
        n_ref[b, h:h + 1, :] = n_new[i]
        m_ref[b, h:h + 1, :] = jnp.broadcast_to(m_new[i], (1, B_HEAD_DIM))
        y_ref[b, :, hs(h)] = out[i]


def _mlstm(pb, pg, conv0, c0, n0, m0, wts, chunk):
    bn, t, _ = pb.shape
    nb = MIX_STREAMS
    hist = B_CONV - 1
    consts = _mlstm_consts(chunk, nb)
    full = lambda a: pl.BlockSpec(a.shape, lambda i, c: (0,) * a.ndim)
    conv_spec = pl.BlockSpec((nb, hist, 2 * B_WIDTH), lambda i, c: (i, 0, 0))
    c_spec = pl.BlockSpec((nb, B_HEADS, B_HEAD_DIM, B_HEAD_DIM), lambda i, c: (i, 0, 0, 0))
    nm_spec = pl.BlockSpec((nb, B_HEADS, B_HEAD_DIM), lambda i, c: (i, 0, 0))
    return pl.pallas_call(
        functools.partial(_mlstm_kernel, chunk=chunk),
        grid=(bn // nb, t // chunk),
        in_specs=[
            pl.BlockSpec((nb, chunk, B_MAIN), lambda i, c: (i, c, 0)),
            pl.BlockSpec((nb, chunk, GATE_PAD), lambda i, c: (i, c, 0)),
            conv_spec, c_spec, nm_spec, nm_spec,
        ] + [full(a) for a in wts] + [full(a) for a in consts],
        out_specs=[
            pl.BlockSpec((nb, chunk, B_WIDTH), lambda i, c: (i, c, 0)),
            conv_spec, c_spec, nm_spec, nm_spec,
        ],
        out_shape=[
            jax.ShapeDtypeStruct((bn, t, B_WIDTH), F32),
            jax.ShapeDtypeStruct((bn, hist, 2 * B_WIDTH), F32),
            jax.ShapeDtypeStruct((bn, B_HEADS, B_HEAD_DIM, B_HEAD_DIM), F32),
            jax.ShapeDtypeStruct((bn, B_HEADS, B_HEAD_DIM), F32),
            jax.ShapeDtypeStruct((bn, B_HEADS, B_HEAD_DIM), F32),
        ],
        scratch_shapes=[pltpu.VMEM((nb, CARRY + chunk, 2 * B_WIDTH), F32)],
        compiler_params=pltpu.CompilerParams(
            dimension_semantics=("arbitrary", "arbitrary"), vmem_limit_bytes=VMEM_LIMIT),
        name="mlstm_mix",
    )(pb, pg, conv0, c0, n0, m0, *wts, *consts)


def _ffn_kernel(x_ref, ya_ref, yb_ref, f0_ref, wout_ref, g_ref, wup_ref, cw_ref, cb_ref,
                wdown_ref, gfin_ref, o_ref, fout_ref, ubuf, *, tile, final):
    t = pl.program_id(1)
    hist = FFN_CONV - 1

    @pl.when(t == 0)
    def _():
        ubuf[CARRY - hist:CARRY, :] = f0_ref[0]

    x1 = (x_ref[0]
          + _dot(ya_ref[0].astype(BF16), wout_ref[0:A_WIDTH, :])
          + _dot(yb_ref[0].astype(BF16), wout_ref[A_WIDTH:, :]))
    h2 = _rms(x1, g_ref[...]).astype(BF16)
    u = _dot(h2, wup_ref[:, 0:D_FF])
    gate = _dot(h2, wup_ref[:, D_FF:])
    ubuf[CARRY:CARRY + tile, :] = u
    conv = cb_ref[...] + cw_ref[hist:hist + 1, :] * u
    for j in range(hist):
        conv = conv + cw_ref[j:j + 1, :] * ubuf[CARRY - hist + j:CARRY - hist + j + tile, :]
    tail = ubuf[CARRY + tile - hist:CARRY + tile, :]
    ubuf[CARRY - hist:CARRY, :] = tail
    fout_ref[0] = tail
    act = conv * _sigmoid(conv) * gate
    x2 = x1 + _dot(act.astype(BF16), wdown_ref[...])
    o_ref[0] = _rms(x2, gfin_ref[...]) if final else x2


def _ffn(x, ya, yb, f0, wts, tile, final):
    bn, t, _ = x.shape
    hist = FFN_CONV - 1
    wout, g, wup, cw, cb, wdown, gfin = wts
    const = lambda b, i: (0, 0)
    tok = lambda n: pl.BlockSpec((1, tile, n), lambda b, i: (b, i, 0))
    resident = lambda a: pl.BlockSpec(a.shape, const, pipeline_mode=pl.Buffered(1))
    fspec = pl.BlockSpec((1, hist, D_FF), lambda b, i: (b, 0, 0))
    return pl.pallas_call(
        functools.partial(_ffn_kernel, tile=tile, final=final),
        grid=(bn, t // tile),
        in_specs=[
            tok(D_MODEL), tok(A_WIDTH), tok(B_WIDTH), fspec,
            resident(wout), resident(g), resident(wup), resident(cw), resident(cb),
            resident(wdown), resident(gfin),
        ],
        out_specs=[tok(D_MODEL), fspec],
        out_shape=[
            jax.ShapeDtypeStruct((bn, t, D_MODEL), F32),
            jax.ShapeDtypeStruct((bn, hist, D_FF), F32),
        ],
        scratch_shapes=[pltpu.VMEM((CARRY + tile, D_FF), F32)],
        compiler_params=pltpu.CompilerParams(
            dimension_semantics=("arbitrary", "arbitrary"), vmem_limit_bytes=VMEM_LIMIT),
        name="out_ffn",
    )(x, ya, yb, f0, wout, g, wup, cw, cb, wdown, gfin)


def _layer(x, st, lw, *, in_tile, mix_chunk, ffn_tile, final):
    shift0, s0, bconv0, c0, n0, m0, fconv0 = st
    bn, t, _ = x.shape
    pa, pb, pg = _in_proj(x.reshape(bn * t, D_MODEL), lw["norm_mix"], lw["w_in"], in_tile)
    pa = pa.reshape(bn, t, A_COLS)
    pb = pb.reshape(bn, t, B_MAIN)
    pg = pg.reshape(bn, t, GATE_PAD)
    ya, shift1, s1 = _rwkv(pa, shift0, s0, lw["rwkv"], mix_chunk)
    yb, bconv1, c1, n1, m1 = _mlstm(pb, pg, bconv0, c0, n0, m0, lw["mlstm"], mix_chunk)
    x, fconv1 = _ffn(x, ya, yb, fconv0, lw["ffn"], ffn_tile, final)
    return x, (shift1, s1, bconv1, c1, n1, m1, fconv1)


def kernel(x_prompt, x_sample, state_rwkv_shift, state_rwkv_wkv, state_mlstm_conv, state_mlstm_C,
           state_mlstm_n, state_mlstm_m, state_ffn_conv, meta_tokens, norm_mix, w_in, a_mu, a_w0, a_w2,
           a_a0, a_a2, a_g2, a_k_k, a_k_a, a_r_k, a_ln_w, a_ln_b, b_conv_w, b_conv_b, b_i_bias, b_f_bias,
           b_hn_w, w_out, norm_ffn, w_up, ffn_conv_w, ffn_conv_b, w_down, norm_final):
    n_prompt = x_prompt.shape[0]
    n_sample = x_sample.shape[0]
    assert x_sample.shape[1] == N_META
    assert n_prompt % MIX_STREAMS == 0 and (n_prompt + n_sample) % MIX_STREAMS == 0

    row = lambda a: a.reshape(1, -1)

    def layer_weights(l):
        gate_w = jnp.zeros((D_MODEL, GATE_PAD), F32).at[:, :2 * B_HEADS].set(w_in[l][:, A_COLS + B_MAIN:])
        gate_b = jnp.zeros((1, GATE_PAD), F32).at[0, :B_HEADS].set(b_i_bias[l])
        gate_b = gate_b.at[0, B_HEADS:2 * B_HEADS].set(b_f_bias[l])
        return {
            "norm_mix": row(norm_mix[l]),
            "w_in": jnp.concatenate([w_in[l][:, :A_COLS + B_MAIN], gate_w], axis=1).astype(BF16),
            "rwkv": (row(a_mu[l]), row(a_w0[l]), a_w2[l].astype(BF16), row(a_a0[l]), a_a2[l].astype(BF16),
                     a_g2[l].astype(BF16), row(a_k_k[l]), row(a_k_a[l]), row(a_r_k[l]), row(a_ln_w[l]),
                     row(a_ln_b[l])),
            "mlstm": (b_conv_w[l], row(b_conv_b[l]), gate_b, row(b_hn_w[l])),
            "ffn": (w_out[l].astype(BF16), row(norm_ffn[l]), w_up[l].astype(BF16), ffn_conv_w[l],
                    row(ffn_conv_b[l]), w_down[l].astype(BF16), row(norm_final)),
        }

    def lead_state(s, l):
        return jnp.concatenate([jnp.zeros((n_prompt,) + s.shape[2:], s.dtype), s[l]], axis=0)

    meta = jnp.broadcast_to(meta_tokens[None].astype(x_prompt.dtype), (n_prompt, N_META, D_MODEL))
    x_lead = jnp.concatenate([meta, x_sample], axis=0)
    x_main = x_prompt
    n_lead = n_prompt + n_sample

    prompt_states, sample_states = [], []
    for l in range(DEPTH):
        lw = layer_weights(l)
        final = l == DEPTH - 1
        st_lead = (
            lead_state(state_rwkv_shift, l)[:, None, :],
            lead_state(state_rwkv_wkv, l),
            lead_state(state_mlstm_conv, l),
            lead_state(state_mlstm_C, l),
            lead_state(state_mlstm_n, l),
            jnp.broadcast_to(lead_state(state_mlstm_m, l)[..., None], (n_lead, B_HEADS, B_HEAD_DIM)),
            lead_state(state_ffn_conv, l),
        )
        x_lead, st_lead = _layer(x_lead, st_lead, lw, in_tile=n_lead * N_META, mix_chunk=N_META,
                                 ffn_tile=N_META, final=final)
        st_main = tuple(s[:n_prompt] for s in st_lead)
        x_main, st_main = _layer(x_main, st_main, lw, in_tile=256, mix_chunk=64, ffn_tile=256, final=final)
        prompt_states.append(st_main)
        sample_states.append(tuple(s[n_prompt:] for s in st_lead))

    def collect(per_layer):
        shift, wkv, bconv, c, n, m, fconv = (jnp.stack([st[i] for st in per_layer]) for i in range(7))
        return (shift[:, :, 0, :], wkv, bconv, c, n, m[..., 0], fconv)

    return (x_main, x_lead[n_prompt:]) + collect(prompt_states) + collect(sample_states)
```

```python
import functools

import numpy as np

import jax
import jax.numpy as jnp
from jax import lax
from jax.experimental import pallas as pl
from jax.experimental.pallas import tpu as pltpu

D_MODEL = 1024
DEPTH = 2
N_META = 16
A_HEADS = 8
A_HEAD_DIM = 64
A_WIDTH = 512
A_DECAY_LORA = 64
A_AAA_LORA = 64
A_GATE_LORA = 128
A_COLS = 1792
B_HEADS = 4
B_HEAD_DIM = 128
B_WIDTH = 512
B_CONV = 4
B_MAIN = 4 * B_WIDTH
GATE_PAD = 128
D_FF = 2816
FFN_CONV = 3
RMS_EPS = 1e-6
GN_EPS = 64e-5
CARRY = 8
MXU_DIM = 256
MIX_STREAMS = 4
FF_BOUNDS = (0, 6 * MXU_DIM, D_FF)
MAIN_TILE = 512
MAIN_CHUNK = 64
VMEM_LIMIT = 56 * 1024 * 1024

F32 = jnp.float32
BF16 = jnp.bfloat16
HI = lax.Precision.HIGHEST
NT_DIMS = (((1,), (1,)), ((), ()))
TN_DIMS = (((0,), (0,)), ((), ()))


def _dot_hi(a, b):
    return jnp.dot(a, b, precision=HI, preferred_element_type=F32)


def _dot(a, b):
    return jnp.dot(a, b, preferred_element_type=F32)


def _dot_nt(a, b):
    return lax.dot_general(a, b, NT_DIMS, preferred_element_type=F32)


def _dot_tn(a, b):
    return lax.dot_general(a, b, TN_DIMS, preferred_element_type=F32)


def _sigmoid(x):
    return 1.0 / (1.0 + jnp.exp(-x))


def _softplus(x):
    return jnp.maximum(x, 0.0) + jnp.log(1.0 + jnp.exp(-jnp.abs(x)))


def _rms(x, g):
    return x * lax.rsqrt(jnp.mean(x * x, axis=-1, keepdims=True) + RMS_EPS) * g


def _tril(n, strict):
    row = lax.broadcasted_iota(jnp.int32, (n, n), 0)
    col = lax.broadcasted_iota(jnp.int32, (n, n), 1)
    return (col < row) if strict else (col <= row)


def _layer_spec(a, l, resident=False):
    index = lambda *grid: (l,) + (0,) * (a.ndim - 1)
    if resident:
        return pl.BlockSpec((None,) + a.shape[1:], index, pipeline_mode=pl.Buffered(1))
    return pl.BlockSpec((None,) + a.shape[1:], index)


def _state_spec(a, l, nb):
    return pl.BlockSpec((None, nb) + a.shape[2:], lambda i, c: (l, i) + (0,) * (a.ndim - 2))


def _in_proj_kernel(x_ref, g_ref, w_ref, wg_ref, oa_ref, ob_ref, og_ref):
    h = _rms(x_ref[...], g_ref[...]).astype(BF16)
    oa_ref[...] = _dot(h, w_ref[:, 0:A_COLS])
    ob_ref[...] = _dot(h, w_ref[:, A_COLS:])
    og_ref[...] = _dot(h, wg_ref[...])


def _in_proj(x2d, l, g, w, wg, tm):
    m = x2d.shape[0]
    return pl.pallas_call(
        _in_proj_kernel,
        grid=(m // tm,),
        in_specs=[
            pl.BlockSpec((tm, D_MODEL), lambda i: (i, 0)),
            _layer_spec(g, l), _layer_spec(w, l, resident=True), _layer_spec(wg, l, resident=True),
        ],
        out_specs=[
            pl.BlockSpec((tm, A_COLS), lambda i: (i, 0)),
            pl.BlockSpec((tm, B_MAIN), lambda i: (i, 0)),
            pl.BlockSpec((tm, GATE_PAD), lambda i: (i, 0)),
        ],
        out_shape=[
            jax.ShapeDtypeStruct((m, A_COLS), F32),
            jax.ShapeDtypeStruct((m, B_MAIN), F32),
            jax.ShapeDtypeStruct((m, GATE_PAD), F32),
        ],
        compiler_params=pltpu.CompilerParams(
            dimension_semantics=("arbitrary",), vmem_limit_bytes=VMEM_LIMIT),
        name="in_proj",
    )(x2d, g, w, wg)


def _rwkv_heads_per_group(chunk):
    return min(A_HEADS, MXU_DIM // chunk)


def _rwkv_consts(chunk, streams):
    hg = _rwkv_heads_per_group(chunk)
    hl, w, rows = hg * chunk, hg * A_HEAD_DIM, streams * chunk
    ix = lambda n: (np.arange(n)[:, None], np.arange(n)[None, :])
    r, c = ix(MXU_DIM)
    seg = r // A_HEAD_DIM == c // A_HEAD_DIM
    r, c = ix(rows)
    tril = (r // chunk == c // chunk) & (c <= r)
    r, c = np.arange(2 * chunk)[:, None], np.arange(hl)[None, :]
    amask = np.where(r < chunk, c % chunk < r, c % chunk <= r - chunk)
    r, c = np.arange(chunk)[:, None], np.arange(hl)[None, :]
    eye = c % chunk == r
    r, c = np.arange(hl)[:, None], np.arange(w)[None, :]
    bdl = r // chunk == c // A_HEAD_DIM
    r, c = ix(hl)
    bdsq = r // chunk == c // chunk
    r, c = ix(w)
    smask = r // A_HEAD_DIM == c // A_HEAD_DIM
    return (jnp.asarray(seg, BF16), jnp.asarray(tril, BF16), jnp.asarray(amask, F32), jnp.asarray(eye, F32),
            jnp.asarray(bdl, BF16), jnp.asarray(bdsq, BF16), jnp.asarray(smask, F32))


def _rwkv_kernel(p_ref, shift_ref, s0_ref, mu_ref, w0_ref, w2_ref, a0_ref, a2_ref, g2_ref,
                 kk_ref, ka_ref, rk_ref, lnw_ref, lnb_ref,
                 seg_ref, tril_ref, amask_ref, eye_ref, bdl_ref, bdsq_ref, smask_ref,
                 y_ref, shift_out_ref, s_out_ref, pbuf, bds, *, chunk, hg):
    L = chunk
    nb = p_ref.shape[0]
    rows = nb * L
    hl = hg * L
    wid = hg * A_HEAD_DIM
    groups = A_HEADS // hg
    c = pl.program_id(1)
    diag = lambda h: (h // hg, slice((h % hg) * A_HEAD_DIM, (h % hg + 1) * A_HEAD_DIM))

    @pl.when(c == 0)
    def _():
        bds[...] = jnp.zeros(bds.shape, F32)
        for b in range(nb):
            pbuf[b, CARRY - 1:CARRY, :] = shift_ref[b]
            for h in range(A_HEADS):
                g, sl = diag(h)
                bds[b, g, sl, sl] = s0_ref[b, h]

    ps, prevs = [], []
    for b in range(nb):
        pb = p_ref[b]
        pbuf[b, CARRY:CARRY + L, :] = pb
        prevs.append(pbuf[b, CARRY - 1:CARRY - 1 + L, :])
        pbuf[b, CARRY - 1:CARRY, :] = pb[L - 1:L, :]
        shift_out_ref[b] = pb[L - 1:L, :]
        ps.append(pb)
    p = jnp.concatenate(ps, axis=0)
    pm = p + (jnp.concatenate(prevs, axis=0) - p) * mu_ref[...]

    i1, i2, i3 = A_WIDTH, 2 * A_WIDTH, 3 * A_WIDTH
    i4, i5 = i3 + A_DECAY_LORA, i3 + A_DECAY_LORA + A_AAA_LORA
    r, k, v = pm[:, :i1], pm[:, i1:i2], pm[:, i2:i3]
    wl, al, gl = pm[:, i3:i4], pm[:, i4:i5], pm[:, i5:]
    seg = seg_ref[...]

    def segsum(t):
        half = A_WIDTH // 2
        stacked = jnp.concatenate([t[:, :half], t[:, half:]], axis=0).astype(BF16)
        out = _dot(stacked, seg)
        return jnp.concatenate([out[:rows], out[rows:]], axis=1)

    w = -_softplus(-(w0_ref[...] + _dot(jnp.tanh(wl).astype(BF16), w2_ref[...]))) - 0.5
    lw = -jnp.exp(w)
    a = _sigmoid(a0_ref[...] + _dot(al.astype(BF16), a2_ref[...]))
    gate = _dot(_sigmoid(gl).astype(BF16), g2_ref[...])
    kk = k * kk_ref[...]
    kk = kk * lax.rsqrt(segsum(kk * kk) + 1e-12)
    kmod = k * (1.0 + (a - 1.0) * ka_ref[...])
    bvec = kk * a

    lw_hi = lw.astype(BF16)
    lw_lo = (lw - lw_hi.astype(F32)).astype(BF16)
    tril = tril_ref[...]
    cum = _dot(tril, lw_hi) + _dot(tril, lw_lo)
    g_inv = jnp.exp(-cum)
    kkg_all = (kk * jnp.exp(cum - lw)).astype(BF16)
    rg_all = (r * jnp.exp(cum)).astype(BF16)
    kd_all = (kmod * g_inv).astype(BF16)
    bd_all = (bvec * g_inv).astype(BF16)
    v_all = v.astype(BF16)

    keep = amask_ref[...] > 0.5
    eye = eye_ref[...]
    bdl = bdl_ref[...]
    bdsq = bdsq_ref[...]
    smask = smask_ref[...]
    lane_bd = lambda t: jnp.concatenate([t] * hg, axis=0) * bdl
    sq_bd = lambda t: jnp.concatenate([t] * hg, axis=0) * bdsq

    chains = [(b, g) for b in range(nb) for g in range(groups)]
    rs = lambda b: slice(b * L, (b + 1) * L)
    ls = lambda g: slice(g * wid, (g + 1) * wid)
    each = lambda f: [f(i, b, g) for i, (b, g) in enumerate(chains)]

    cum_last = [cum[(b + 1) * L - 1:(b + 1) * L] for b in range(nb)]
    g_tail = [jnp.exp(cum_last[b] - cum[rs(b)]) for b in range(nb)]
    kt = [(kmod[rs(b)] * g_tail[b]).astype(BF16) for b in range(nb)]
    bt = [(bvec[rs(b)] * g_tail[b]).astype(BF16) for b in range(nb)]
    g_last = [jnp.exp(cum_last[b]) for b in range(nb)]

    kkg = each(lambda i, b, g: kkg_all[rs(b), ls(g)])
    rg = each(lambda i, b, g: rg_all[rs(b), ls(g)])
    v16 = each(lambda i, b, g: v_all[rs(b), ls(g)])
    lhs = each(lambda i, b, g: jnp.concatenate([kkg[i], rg[i]], axis=0))
    a_k = each(lambda i, b, g: jnp.where(keep, _dot_nt(lhs[i], lane_bd(kd_all[rs(b), ls(g)])), 0.0))
    a_b = each(lambda i, b, g: jnp.where(keep, _dot_nt(lhs[i], lane_bd(bd_all[rs(b), ls(g)])), 0.0))
    a_kk = each(lambda i, b, g: a_k[i][:L].astype(BF16))
    a_rk = each(lambda i, b, g: a_k[i][L:].astype(BF16))
    a_rb = each(lambda i, b, g: a_b[i][L:].astype(BF16))

    npow = each(lambda i, b, g: -a_b[i][:L])
    tinv = each(lambda i, b, g: eye + npow[i])
    n16 = each(lambda i, b, g: npow[i].astype(BF16))
    npow = each(lambda i, b, g: _dot(n16[i], sq_bd(n16[i])))
    span = 2
    while span < L:
        n16 = each(lambda i, b, g: npow[i].astype(BF16))
        t_bd = each(lambda i, b, g: sq_bd(tinv[i].astype(BF16)))
        if 2 * span < L:
            both = each(lambda i, b, g: _dot(n16[i], jnp.concatenate([sq_bd(n16[i]), t_bd[i]], axis=1)))
            npow = each(lambda i, b, g: both[i][:, :hl])
            prod = each(lambda i, b, g: both[i][:, hl:])
        else:
            prod = each(lambda i, b, g: _dot(n16[i], t_bd[i]))
        tinv = each(lambda i, b, g: tinv[i] + prod[i])
        span *= 2

    s16 = each(lambda i, b, g: bds[b, g].astype(BF16))
    v_bd = each(lambda i, b, g: lane_bd(v16[i]))
    rhs = each(lambda i, b, g: (_dot_nt(kkg[i], s16[i]) + _dot(a_kk[i], v_bd[i])).astype(BF16))
    u16 = each(lambda i, b, g: _dot(tinv[i].astype(BF16), lane_bd(rhs[i])).astype(BF16))
    yc_ = each(lambda i, b, g: _dot_nt(rg[i], s16[i]) + _dot(a_rk[i], v_bd[i]) - _dot(a_rb[i], lane_bd(u16[i])))
    upd = each(lambda i, b, g: _dot_tn(jnp.concatenate([v16[i], -u16[i]], axis=0),
                                       jnp.concatenate([kt[b][:, ls(g)], bt[b][:, ls(g)]], axis=0)))
    for i, (b, g) in enumerate(chains):
        bds[b, g] = bds[b, g] * g_last[b][:, ls(g)] + upd[i] * smask
    ys = [jnp.concatenate(yc_[b * groups:(b + 1) * groups], axis=1) for b in range(nb)]
    y = jnp.concatenate(ys, axis=0)

    inv_d = 1.0 / A_HEAD_DIM
    yc = y - segsum(y) * inv_d
    var = segsum(yc * yc) * inv_d
    yn = yc * lax.rsqrt(var + GN_EPS) * lnw_ref[...] + lnb_ref[...]
    bonus = segsum(r * kmod * rk_ref[...]) * v
    out = (yn + bonus) * gate
    for b in range(nb):
        y_ref[b] = out[b * L:(b + 1) * L]

    @pl.when(c == pl.num_programs(1) - 1)
    def _():
        for b in range(nb):
            for h in range(A_HEADS):
                g, sl = diag(h)
                s_out_ref[b, h] = bds[b, g, sl, sl]


def _rwkv(pa, l, sl, shift0, s0, wts, chunk):
    bn, t, _ = pa.shape
    nb = MIX_STREAMS
    hg = _rwkv_heads_per_group(chunk)
    consts = _rwkv_consts(chunk, nb)
    full = lambda a: pl.BlockSpec(a.shape, lambda i, c: (0,) * a.ndim)
    state = pl.BlockSpec((nb, A_HEADS, A_HEAD_DIM, A_HEAD_DIM), lambda i, c: (i, 0, 0, 0))
    shift = pl.BlockSpec((nb, 1, A_COLS), lambda i, c: (i, 0, 0))
    return pl.pallas_call(
        functools.partial(_rwkv_kernel, chunk=chunk, hg=hg),
        grid=(bn // nb, t // chunk),
        in_specs=[pl.BlockSpec((nb, chunk, A_COLS), lambda i, c: (i, c, 0)),
                  _state_spec(shift0, sl, nb), _state_spec(s0, sl, nb)]
        + [_layer_spec(a, l) for a in wts] + [full(a) for a in consts],
        out_specs=[pl.BlockSpec((nb, chunk, A_WIDTH), lambda i, c: (i, c, 0)), shift, state],
        out_shape=[
            jax.ShapeDtypeStruct((bn, t, A_WIDTH), F32),
            jax.ShapeDtypeStruct((bn, 1, A_COLS), F32),
            jax.ShapeDtypeStruct((bn, A_HEADS, A_HEAD_DIM, A_HEAD_DIM), F32),
        ],
        scratch_shapes=[
            pltpu.VMEM((nb, CARRY + chunk, A_COLS), F32),
            pltpu.VMEM((nb, A_HEADS // hg, hg * A_HEAD_DIM, hg * A_HEAD_DIM), F32),
        ],
        compiler_params=pltpu.CompilerParams(
            dimension_semantics=("arbitrary", "arbitrary"), vmem_limit_bytes=VMEM_LIMIT),
        name="rwkv_mix",
    )(pa, shift0, s0, *wts, *consts)


def _mlstm_consts(chunk, streams):
    r = np.arange(streams * chunk)[:, None]
    c = np.arange(streams * chunk)[None, :]
    return (jnp.asarray((r // chunk == c // chunk) & (c <= r), BF16),)


def _split3(x):
    hi = x.astype(BF16)
    rest = x - hi.astype(F32)
    mid = rest.astype(BF16)
    lo = (rest - mid.astype(F32)).astype(BF16)
    return hi, mid, lo


def _mlstm_kernel(pb_ref, pg_ref, conv0_ref, c0_ref, n0_ref, m0_ref,
                  cw_ref, cb_ref, gb_ref, hnw_ref, tril_ref,
                  y_ref, conv_out_ref, c_ref, n_ref, m_ref, xbuf, *, chunk):
    L = chunk
    nb = pb_ref.shape[0]
    rows = nb * L
    hist = B_CONV - 1

    @pl.when(pl.program_id(1) == 0)
    def _():
        for b in range(nb):
            xbuf[b, CARRY - hist:CARRY, :] = conv0_ref[b]
        c_ref[...] = c0_ref[...]
        n_ref[...] = n0_ref[...]
        m_ref[...] = m0_ref[...]

    qk = []
    for b in range(nb):
        x = pb_ref[b, :, 0:2 * B_WIDTH]
        xbuf[b, CARRY:CARRY + L, :] = x
        acc = cb_ref[...] + cw_ref[hist:hist + 1, :] * x
        for j in range(hist):
            acc = acc + cw_ref[j:j + 1, :] * xbuf[b, CARRY - hist + j:CARRY - hist + j + L, :]
        tail = xbuf[b, CARRY + L - hist:CARRY + L, :]
        xbuf[b, CARRY - hist:CARRY, :] = tail
        conv_out_ref[b] = tail
        qk.append(acc * _sigmoid(acc))

    gates = jnp.concatenate([pg_ref[b] for b in range(nb)], axis=0) + gb_ref[...]
    lane = lax.broadcasted_iota(jnp.int32, (rows, GATE_PAD), 1)
    tril = tril_ref[...]
    cum = sum(_dot(tril, part) for part in _split3(-_softplus(-gates)))
    gcols = jnp.where(lane < B_HEADS, gates, cum)
    grows = gcols.T
    causal = _tril(L, False)

    chains = [(b, h) for b in range(nb) for h in range(B_HEADS)]
    each = lambda f: [f(i, b, h) for i, (b, h) in enumerate(chains)]
    rs = lambda b: slice(b * L, (b + 1) * L)
    hs = lambda h, base=0: slice(base + h * B_HEAD_DIM, base + (h + 1) * B_HEAD_DIM)

    q = each(lambda i, b, h: qk[b][:, hs(h)])
    k = each(lambda i, b, h: qk[b][:, hs(h, B_WIDTH)] * (B_HEAD_DIM ** -0.5))
    q16 = each(lambda i, b, h: q[i].astype(BF16))
    v16 = each(lambda i, b, h: pb_ref[b, :, hs(h, 2 * B_WIDTH)].astype(BF16))
    li_col = each(lambda i, b, h: gcols[rs(b), h:h + 1])
    b_col = each(lambda i, b, h: gcols[rs(b), B_HEADS + h:B_HEADS + h + 1])
    li_row = each(lambda i, b, h: grows[h:h + 1, rs(b)])
    b_row = each(lambda i, b, h: grows[B_HEADS + h:B_HEADS + h + 1, rs(b)])
    c_prev = each(lambda i, b, h: c_ref[b, h])
    n_prev = each(lambda i, b, h: n_ref[b, h:h + 1, :])
    m_prev = each(lambda i, b, h: m_ref[b, h:h + 1, 0:1])

    dm = each(lambda i, b, h: jnp.where(causal, b_col[i] - b_row[i] + li_row[i], -jnp.inf))
    inter = each(lambda i, b, h: b_col[i] + m_prev[i])
    mt = each(lambda i, b, h: jnp.maximum(inter[i], jnp.max(dm[i], axis=-1, keepdims=True)))
    wo = each(lambda i, b, h: jnp.exp(inter[i] - mt[i]))
    s = each(lambda i, b, h: _dot_nt(q16[i], k[i].astype(BF16)) * jnp.exp(dm[i] - mt[i]))
    num = each(lambda i, b, h: wo[i] * _dot(q16[i], c_prev[i].astype(BF16)) + _dot(s[i].astype(BF16), v16[i]))
    den = each(lambda i, b, h: wo[i] * jnp.sum(q[i] * n_prev[i], axis=-1, keepdims=True)
               + jnp.sum(s[i], axis=-1, keepdims=True))
    hh = each(lambda i, b, h: num[i] / jnp.maximum(jnp.abs(den[i]), jnp.exp(-mt[i])))

    m_new = each(lambda i, b, h: mt[i][L - 1:L, :])
    b_last = each(lambda i, b, h: b_col[i][L - 1:L, :])
    kw = each(lambda i, b, h: k[i] * jnp.exp(b_last[i] - b_col[i] + li_col[i] - m_new[i]))
    dec = each(lambda i, b, h: jnp.exp(b_last[i] + m_prev[i] - m_new[i]))
    c_new = each(lambda i, b, h: dec[i] * c_prev[i] + _dot_tn(kw[i].astype(BF16), v16[i]))
    n_new = each(lambda i, b, h: dec[i] * n_prev[i] + jnp.sum(kw[i], axis=0, keepdims=True))
    out = each(lambda i, b, h: hh[i] * lax.rsqrt(jnp.mean(hh[i] * hh[i], axis=-1, keepdims=True) + RMS_EPS)
               * hnw_ref[:, hs(h)] * _sigmoid(pb_ref[b, :, hs(h, 3 * B_WIDTH)]))
    for i, (b, h) in enumerate(chains):
        c_ref[b, h] = c_new[i]
        n_ref[b, h:h + 1, :] = n_new[i]
        m_ref[b, h:h + 1, :] = jnp.broadcast_to(m_new[i], (1, B_HEAD_DIM))
        y_ref[b, :, hs(h)] = out[i]


def _mlstm(pb, pg, l, sl, conv0, c0, n0, m0, wts, chunk):
    bn, t, _ = pb.shape
    nb = MIX_STREAMS
    hist = B_CONV - 1
    consts = _mlstm_consts(chunk, nb)
    full = lambda a: pl.BlockSpec(a.shape, lambda i, c: (0,) * a.ndim)
    conv_spec = pl.BlockSpec((nb, hist, 2 * B_WIDTH), lambda i, c: (i, 0, 0))
    c_spec = pl.BlockSpec((nb, B_HEADS, B_HEAD_DIM, B_HEAD_DIM), lambda i, c: (i, 0, 0, 0))
    nm_spec = pl.BlockSpec((nb, B_HEADS, B_HEAD_DIM), lambda i, c: (i, 0, 0))
    return pl.pallas_call(
        functools.partial(_mlstm_kernel, chunk=chunk),
        grid=(bn // nb, t // chunk),
        in_specs=[
            pl.BlockSpec((nb, chunk, B_MAIN), lambda i, c: (i, c, 0)),
            pl.BlockSpec((nb, chunk, GATE_PAD), lambda i, c: (i, c, 0)),
            _state_spec(conv0, sl, nb), _state_spec(c0, sl, nb), _state_spec(n0, sl, nb), _state_spec(m0, sl, nb),
        ] + [_layer_spec(a, l) for a in wts] + [full(a) for a in consts],
        out_specs=[
            pl.BlockSpec((nb, chunk, B_WIDTH), lambda i, c: (i, c, 0)),
            conv_spec, c_spec, nm_spec, nm_spec,
        ],
        out_shape=[
            jax.ShapeDtypeStruct((bn, t, B_WIDTH), F32),
            jax.ShapeDtypeStruct((bn, hist, 2 * B_WIDTH), F32),
            jax.ShapeDtypeStruct((bn, B_HEADS, B_HEAD_DIM, B_HEAD_DIM), F32),
            jax.ShapeDtypeStruct((bn, B_HEADS, B_HEAD_DIM), F32),
            jax.ShapeDtypeStruct((bn, B_HEADS, B_HEAD_DIM), F32),
        ],
        scratch_shapes=[pltpu.VMEM((nb, CARRY + chunk, 2 * B_WIDTH), F32)],
        compiler_params=pltpu.CompilerParams(
            dimension_semantics=("arbitrary", "arbitrary"), vmem_limit_bytes=VMEM_LIMIT),
        name="mlstm_mix",
    )(pb, pg, conv0, c0, n0, m0, *wts, *consts)


def _ffn_kernel(x_ref, ya_ref, yb_ref, f0_ref, wout_ref, g_ref, wup_ref, cw_ref, cb_ref,
                wdown_ref, gfin_ref, o_ref, fout_ref, ubuf, *, tile, final):
    nb = x_ref.shape[0]
    hist = FFN_CONV - 1
    rows_of = lambda ref: ref[0] if nb == 1 else jnp.concatenate([ref[b] for b in range(nb)], axis=0)

    @pl.when(pl.program_id(1) == 0)
    def _():
        for b in range(nb):
            ubuf[b, CARRY - hist:CARRY, :] = f0_ref[b]

    x1 = (rows_of(x_ref)
          + _dot(rows_of(ya_ref).astype(BF16), wout_ref[0:A_WIDTH, :])
          + _dot(rows_of(yb_ref).astype(BF16), wout_ref[A_WIDTH:, :]))
    h2 = _rms(x1, g_ref[...]).astype(BF16)
    acts = []
    for lo, hi in zip(FF_BOUNDS[:-1], FF_BOUNDS[1:]):
        cols = slice(lo, hi)
        u = _dot(h2, wup_ref[:, cols])
        gate = _dot(h2, wup_ref[:, D_FF + lo:D_FF + hi])
        convs = []
        for b in range(nb):
            ub = u[b * tile:(b + 1) * tile]
            ubuf[b, CARRY:CARRY + tile, cols] = ub
            acc = cb_ref[:, cols] + cw_ref[hist:hist + 1, cols] * ub
            for j in range(hist):
                acc = acc + cw_ref[j:j + 1, cols] * ubuf[b, CARRY - hist + j:CARRY - hist + j + tile, cols]
            tail = ubuf[b, CARRY + tile - hist:CARRY + tile, cols]
            ubuf[b, CARRY - hist:CARRY, cols] = tail
            fout_ref[b, :, cols] = tail
            convs.append(acc)
        conv = convs[0] if nb == 1 else jnp.concatenate(convs, axis=0)
        acts.append((conv * _sigmoid(conv) * gate).astype(BF16))
    x2 = x1 + _dot(jnp.concatenate(acts, axis=1), wdown_ref[...])
    out = _rms(x2, gfin_ref[...]) if final else x2
    for b in range(nb):
        o_ref[b] = out[b * tile:(b + 1) * tile]


def _ffn(x, ya, yb, l, sl, f0, wts, gfin, nb, tile, final):
    bn, t, _ = x.shape
    hist = FFN_CONV - 1
    tok = lambda n: pl.BlockSpec((nb, tile, n), lambda i, c: (i, c, 0))
    fspec = pl.BlockSpec((nb, hist, D_FF), lambda i, c: (i, 0, 0))
    return pl.pallas_call(
        functools.partial(_ffn_kernel, tile=tile, final=final),
        grid=(bn // nb, t // tile),
        in_specs=[tok(D_MODEL), tok(A_WIDTH), tok(B_WIDTH), _state_spec(f0, sl, nb)]
        + [_layer_spec(a, l, resident=True) for a in wts] + [_layer_spec(gfin, 0, resident=True)],
        out_specs=[tok(D_MODEL), fspec],
        out_shape=[
            jax.ShapeDtypeStruct((bn, t, D_MODEL), F32),
            jax.ShapeDtypeStruct((bn, hist, D_FF), F32),
        ],
        scratch_shapes=[pltpu.VMEM((nb, CARRY + tile, D_FF), F32)],
        compiler_params=pltpu.CompilerParams(
            dimension_semantics=("arbitrary", "arbitrary"), vmem_limit_bytes=VMEM_LIMIT),
        name="out_ffn",
    )(x, ya, yb, f0, *wts, gfin)


def _layer(x, l, sl, st, w, *, in_tile, mix_chunk, ffn_streams, ffn_tile, final):
    shift0, s0, bconv0, c0, n0, m0, fconv0 = st
    bn, t, _ = x.shape
    pa, pb, pg = _in_proj(x.reshape(bn * t, D_MODEL), l, w["norm_mix"], w["w_main"], w["w_gate"], in_tile)
    pa = pa.reshape(bn, t, A_COLS)
    pb = pb.reshape(bn, t, B_MAIN)
    pg = pg.reshape(bn, t, GATE_PAD)
    ya, shift1, s1 = _rwkv(pa, l, sl, shift0, s0, w["rwkv"], mix_chunk)
    yb, bconv1, c1, n1, m1 = _mlstm(pb, pg, l, sl, bconv0, c0, n0, m0, w["mlstm"], mix_chunk)
    x, fconv1 = _ffn(x, ya, yb, l, sl, fconv0, w["ffn"], w["norm_final"], ffn_streams, ffn_tile, final)
    return x, (shift1, s1, bconv1, c1, n1, m1, fconv1)


def kernel(x_prompt, x_sample, state_rwkv_shift, state_rwkv_wkv, state_mlstm_conv, state_mlstm_C,
           state_mlstm_n, state_mlstm_m, state_ffn_conv, meta_tokens, norm_mix, w_in, a_mu, a_w0, a_w2,
           a_a0, a_a2, a_g2, a_k_k, a_k_a, a_r_k, a_ln_w, a_ln_b, b_conv_w, b_conv_b, b_i_bias, b_f_bias,
           b_hn_w, w_out, norm_ffn, w_up, ffn_conv_w, ffn_conv_b, w_down, norm_final):
    n_prompt = x_prompt.shape[0]
    n_sample = x_sample.shape[0]
    n_lead = n_prompt + n_sample
    assert x_sample.shape[1] == N_META
    assert n_prompt == MIX_STREAMS and n_lead % MIX_STREAMS == 0

    bf = lambda a: a.astype(BF16)
    vec = lambda a: a[:, None, :]
    n_gate = 2 * B_HEADS
    gate_pad = ((0, 0), (0, GATE_PAD - n_gate))
    w = {
        "norm_mix": vec(norm_mix),
        "w_main": bf(w_in[:, :, :A_COLS + B_MAIN]),
        "w_gate": bf(jnp.pad(w_in[:, :, A_COLS + B_MAIN:], ((0, 0),) + gate_pad)),
        "rwkv": (vec(a_mu), vec(a_w0), bf(a_w2), vec(a_a0), bf(a_a2), bf(a_g2), vec(a_k_k), vec(a_k_a),
                 vec(a_r_k), vec(a_ln_w), vec(a_ln_b)),
        "mlstm": (b_conv_w, vec(b_conv_b), vec(jnp.pad(jnp.concatenate([b_i_bias, b_f_bias], axis=1), gate_pad)),
                  vec(b_hn_w)),
        "ffn": (bf(w_out), vec(norm_ffn), bf(w_up), ffn_conv_w, vec(ffn_conv_b), bf(w_down)),
        "norm_final": norm_final.reshape(1, 1, D_MODEL),
    }

    lead_pad = lambda s: jnp.pad(s, ((0, 0), (n_prompt, 0)) + ((0, 0),) * (s.ndim - 2))
    st_lead_in = (
        lead_pad(state_rwkv_shift)[:, :, None, :],
        lead_pad(state_rwkv_wkv),
        lead_pad(state_mlstm_conv),
        lead_pad(state_mlstm_C),
        lead_pad(state_mlstm_n),
        jnp.broadcast_to(lead_pad(state_mlstm_m)[..., None], (DEPTH, n_lead, B_HEADS, B_HEAD_DIM)),
        lead_pad(state_ffn_conv),
    )
    meta = jnp.broadcast_to(meta_tokens[None].astype(x_prompt.dtype), (n_prompt, N_META, D_MODEL))
    x_lead = jnp.concatenate([meta, x_sample], axis=0)
    x_main = x_prompt

    main_states, lead_states = [], []
    for l in range(DEPTH):
        final = l == DEPTH - 1
        x_lead, st_lead = _layer(x_lead, l, l, st_lead_in, w, in_tile=n_lead * N_META, mix_chunk=N_META,
                                 ffn_streams=n_lead, ffn_tile=N_META, final=final)
        x_main, st_main = _layer(x_main, l, 0, tuple(s[None] for s in st_lead), w, in_tile=MAIN_TILE,
                                 mix_chunk=MAIN_CHUNK, ffn_streams=1, ffn_tile=MAIN_TILE, final=final)
        main_states.append(st_main)
        lead_states.append(st_lead)

    def collect(per_layer, first):
        shift, wkv, bconv, c, n, m, fconv = (jnp.stack([st[i] for st in per_layer])[:, first:] for i in range(7))
        return (shift[:, :, 0, :], wkv, bconv, c, n, m[..., 0], fconv)

    return (x_main, x_lead[n_prompt:]) + collect(main_states, 0) + collect(lead_states, n_prompt)
```

```python
import functools

import numpy as np

import jax
import jax.numpy as jnp
from jax import lax
from jax.experimental import pallas as pl
from jax.experimental.pallas import tpu as pltpu

D_MODEL = 1024
DEPTH = 2
N_META = 16
A_HEADS = 8
A_HEAD_DIM = 64
A_WIDTH = 512
A_DECAY_LORA = 64
A_AAA_LORA = 64
A_GATE_LORA = 128
A_COLS = 1792
B_HEADS = 4
B_HEAD_DIM = 128
B_WIDTH = 512
B_CONV = 4
B_MAIN = 4 * B_WIDTH
GATE_PAD = 128
D_MIX = A_WIDTH + B_WIDTH
D_FF = 2816
FFN_CONV = 3
RMS_EPS = 1e-6
GN_EPS = 64e-5
CARRY = 8
MXU_DIM = 256
MIX_STREAMS = 4
FF_BOUNDS = (0, 6 * MXU_DIM, D_FF)
MAIN_TILE = 512
MAIN_CHUNK = 64
VMEM_LIMIT = 56 * 1024 * 1024

F32 = jnp.float32
BF16 = jnp.bfloat16
NT_DIMS = (((1,), (1,)), ((), ()))
TN_DIMS = (((0,), (0,)), ((), ()))


def _dot(a, b):
    return jnp.dot(a, b, preferred_element_type=F32)


def _dot_nt(a, b):
    return lax.dot_general(a, b, NT_DIMS, preferred_element_type=F32)


def _dot_tn(a, b):
    return lax.dot_general(a, b, TN_DIMS, preferred_element_type=F32)


def _sigmoid(x):
    return 0.5 * jnp.tanh(0.5 * x) + 0.5


def _softplus(x):
    return jnp.maximum(x, 0.0) + jnp.log(1.0 + jnp.exp(-jnp.abs(x)))


def _rms(x, g):
    return x * lax.rsqrt(jnp.mean(x * x, axis=-1, keepdims=True) + RMS_EPS) * g


def _tril(n, strict):
    row = lax.broadcasted_iota(jnp.int32, (n, n), 0)
    col = lax.broadcasted_iota(jnp.int32, (n, n), 1)
    return (col < row) if strict else (col <= row)


def _split3(x):
    hi = x.astype(BF16)
    rest = x - hi.astype(F32)
    mid = rest.astype(BF16)
    lo = (rest - mid.astype(F32)).astype(BF16)
    return hi, mid, lo


def _layer_spec(a, l, resident=False):
    index = lambda *grid: (l,) + (0,) * (a.ndim - 1)
    if resident:
        return pl.BlockSpec((None,) + a.shape[1:], index, pipeline_mode=pl.Buffered(1))
    return pl.BlockSpec((None,) + a.shape[1:], index)


def _state_spec(a, l, nb):
    return pl.BlockSpec((None, nb) + a.shape[2:], lambda i, c: (l, i) + (0,) * (a.ndim - 2))


def _in_proj_kernel(x_ref, g_ref, w_ref, wg_ref, oa_ref, ob_ref, og_ref):
    h = _rms(x_ref[...], g_ref[...]).astype(BF16)
    oa_ref[...] = _dot(h, w_ref[:, 0:A_COLS])
    ob_ref[...] = _dot(h, w_ref[:, A_COLS:A_COLS + B_MAIN])
    og_ref[...] = _dot(h, wg_ref[...])


def _in_proj(x2d, l, g, w, wg, tm):
    m = x2d.shape[0]
    return pl.pallas_call(
        _in_proj_kernel,
        grid=(m // tm,),
        in_specs=[
            pl.BlockSpec((tm, D_MODEL), lambda i: (i, 0)),
            _layer_spec(g, l), _layer_spec(w, l, resident=True), _layer_spec(wg, l, resident=True),
        ],
        out_specs=[
            pl.BlockSpec((tm, A_COLS), lambda i: (i, 0)),
            pl.BlockSpec((tm, B_MAIN), lambda i: (i, 0)),
            pl.BlockSpec((tm, GATE_PAD), lambda i: (i, 0)),
        ],
        out_shape=[
            jax.ShapeDtypeStruct((m, A_COLS), F32),
            jax.ShapeDtypeStruct((m, B_MAIN), F32),
            jax.ShapeDtypeStruct((m, GATE_PAD), F32),
        ],
        compiler_params=pltpu.CompilerParams(
            dimension_semantics=("arbitrary",), vmem_limit_bytes=VMEM_LIMIT),
        name="in_proj",
    )(x2d, g, w, wg)


def _rwkv_heads_per_group(chunk):
    return min(A_HEADS, MXU_DIM // chunk)


def _mix_consts(chunk, streams):
    hg = _rwkv_heads_per_group(chunk)
    hl, w, rows = hg * chunk, hg * A_HEAD_DIM, streams * chunk
    ix = lambda n: (np.arange(n)[:, None], np.arange(n)[None, :])
    r, c = ix(MXU_DIM)
    seg = r // A_HEAD_DIM == c // A_HEAD_DIM
    r, c = ix(rows)
    tril = (r // chunk == c // chunk) & (c <= r)
    r, c = np.arange(2 * chunk)[:, None], np.arange(hl)[None, :]
    amask = np.where(r < chunk, c % chunk < r, c % chunk <= r - chunk)
    r, c = np.arange(chunk)[:, None], np.arange(hl)[None, :]
    eye = c % chunk == r
    r, c = np.arange(hl)[:, None], np.arange(w)[None, :]
    bdl = r // chunk == c // A_HEAD_DIM
    r, c = ix(hl)
    bdsq = r // chunk == c // chunk
    r, c = ix(w)
    smask = r // A_HEAD_DIM == c // A_HEAD_DIM
    return (jnp.asarray(seg, BF16), jnp.asarray(tril, BF16), jnp.asarray(amask, F32), jnp.asarray(eye, F32),
            jnp.asarray(bdl, BF16), jnp.asarray(bdsq, BF16), jnp.asarray(smask, F32))


def _rwkv_stages(p_ref, shift_ref, s0_ref, wts, consts, y_ref, shift_out_ref, s_out_ref, pbuf, bds,
                 *, chunk, hg, first, last):
    mu_ref, w0_ref, w2_ref, a0_ref, a2_ref, g2_ref, kk_ref, ka_ref, rk_ref, lnw_ref, lnb_ref = wts
    seg_ref, tril_ref, amask_ref, eye_ref, bdl_ref, bdsq_ref, smask_ref = consts
    L = chunk
    nb = p_ref.shape[0]
    rows = nb * L
    hl = hg * L
    wid = hg * A_HEAD_DIM
    groups = A_HEADS // hg
    diag = lambda h: (h // hg, slice((h % hg) * A_HEAD_DIM, (h % hg + 1) * A_HEAD_DIM))

    @pl.when(first)
    def _():
        bds[...] = jnp.zeros(bds.shape, F32)
        for b in range(nb):
            pbuf[b, CARRY - 1:CARRY, :] = shift_ref[b]
            for h in range(A_HEADS):
                g, sl = diag(h)
                bds[b, g, sl, sl] = s0_ref[b, h]

    ps, prevs = [], []
    for b in range(nb):
        pb = p_ref[b]
        pbuf[b, CARRY:CARRY + L, :] = pb
        prevs.append(pbuf[b, CARRY - 1:CARRY - 1 + L, :])
        pbuf[b, CARRY - 1:CARRY, :] = pb[L - 1:L, :]
        shift_out_ref[b] = pb[L - 1:L, :]
        ps.append(pb)
    p = jnp.concatenate(ps, axis=0)
    pm = p + (jnp.concatenate(prevs, axis=0) - p) * mu_ref[...]
    yield

    i1, i2, i3 = A_WIDTH, 2 * A_WIDTH, 3 * A_WIDTH
    i4, i5 = i3 + A_DECAY_LORA, i3 + A_DECAY_LORA + A_AAA_LORA
    r, k, v = pm[:, :i1], pm[:, i1:i2], pm[:, i2:i3]
    wl, al, gl = pm[:, i3:i4], pm[:, i4:i5], pm[:, i5:]
    seg = seg_ref[...]

    def segsum(t):
        half = A_WIDTH // 2
        stacked = jnp.concatenate([t[:, :half], t[:, half:]], axis=0).astype(BF16)
        out = _dot(stacked, seg)
        return jnp.concatenate([out[:rows], out[rows:]], axis=1)

    w = -_softplus(-(w0_ref[...] + _dot(jnp.tanh(wl).astype(BF16), w2_ref[...]))) - 0.5
    lw = -jnp.exp(w)
    yield
    a = _sigmoid(a0_ref[...] + _dot(al.astype(BF16), a2_ref[...]))
    gate = _dot(_sigmoid(gl).astype(BF16), g2_ref[...])
    kk = k * kk_ref[...]
    kk = kk * lax.rsqrt(segsum(kk * kk) + 1e-12)
    kmod = k * (1.0 + (a - 1.0) * ka_ref[...])
    bvec = kk * a
    yield

    lw_hi = lw.astype(BF16)
    lw_lo = (lw - lw_hi.astype(F32)).astype(BF16)
    tril = tril_ref[...]
    cum = _dot(tril, lw_hi) + _dot(tril, lw_lo)
    g_inv = jnp.exp(-cum)
    kkg_all = (kk * jnp.exp(cum - lw)).astype(BF16)
    rg_all = (r * jnp.exp(cum)).astype(BF16)
    yield
    kd_all = (kmod * g_inv).astype(BF16)
    bd_all = (bvec * g_inv).astype(BF16)
    v_all = v.astype(BF16)

    keep = amask_ref[...] > 0.5
    eye = eye_ref[...]
    bdl = bdl_ref[...]
    bdsq = bdsq_ref[...]
    smask = smask_ref[...]
    lane_bd = lambda t: jnp.concatenate([t] * hg, axis=0) * bdl
    sq_bd = lambda t: jnp.concatenate([t] * hg, axis=0) * bdsq

    chains = [(b, g) for b in range(nb) for g in range(groups)]
    rs = lambda b: slice(b * L, (b + 1) * L)
    ls = lambda g: slice(g * wid, (g + 1) * wid)
    each = lambda f: [f(i, b, g) for i, (b, g) in enumerate(chains)]

    cum_last = [cum[(b + 1) * L - 1:(b + 1) * L] for b in range(nb)]
    g_tail = [jnp.exp(cum_last[b] - cum[rs(b)]) for b in range(nb)]
    kt = [(kmod[rs(b)] * g_tail[b]).astype(BF16) for b in range(nb)]
    bt = [(bvec[rs(b)] * g_tail[b]).astype(BF16) for b in range(nb)]
    g_last = [jnp.exp(cum_last[b]) for b in range(nb)]
    yield

    kkg = each(lambda i, b, g: kkg_all[rs(b), ls(g)])
    rg = each(lambda i, b, g: rg_all[rs(b), ls(g)])
    v16 = each(lambda i, b, g: v_all[rs(b), ls(g)])
    lhs = each(lambda i, b, g: jnp.concatenate([kkg[i], rg[i]], axis=0))
    a_k = each(lambda i, b, g: jnp.where(keep, _dot_nt(lhs[i], lane_bd(kd_all[rs(b), ls(g)])), 0.0))
    yield
    a_b = each(lambda i, b, g: jnp.where(keep, _dot_nt(lhs[i], lane_bd(bd_all[rs(b), ls(g)])), 0.0))
    a_kk = each(lambda i, b, g: a_k[i][:L].astype(BF16))
    a_rk = each(lambda i, b, g: a_k[i][L:].astype(BF16))
    a_rb = each(lambda i, b, g: a_b[i][L:].astype(BF16))
    yield

    npow = each(lambda i, b, g: -a_b[i][:L])
    tinv = each(lambda i, b, g: eye + npow[i])
    n16 = each(lambda i, b, g: npow[i].astype(BF16))
    npow = each(lambda i, b, g: _dot(n16[i], sq_bd(n16[i])))
    yield
    span = 2
    while span < L:
        n16 = each(lambda i, b, g: npow[i].astype(BF16))
        t_bd = each(lambda i, b, g: sq_bd(tinv[i].astype(BF16)))
        if 2 * span < L:
            both = each(lambda i, b, g: _dot(n16[i], jnp.concatenate([sq_bd(n16[i]), t_bd[i]], axis=1)))
            npow = each(lambda i, b, g: both[i][:, :hl])
            prod = each(lambda i, b, g: both[i][:, hl:])
        else:
            prod = each(lambda i, b, g: _dot(n16[i], t_bd[i]))
        tinv = each(lambda i, b, g: tinv[i] + prod[i])
        span *= 2
        yield

    s16 = each(lambda i, b, g: bds[b, g].astype(BF16))
    v_bd = each(lambda i, b, g: lane_bd(v16[i]))
    rhs = each(lambda i, b, g: (_dot_nt(kkg[i], s16[i]) + _dot(a_kk[i], v_bd[i])).astype(BF16))
    yield
    u16 = each(lambda i, b, g: _dot(tinv[i].astype(BF16), lane_bd(rhs[i])).astype(BF16))
    yield
    yc_ = each(lambda i, b, g: _dot_nt(rg[i], s16[i]) + _dot(a_rk[i], v_bd[i]) - _dot(a_rb[i], lane_bd(u16[i])))
    yield
    upd = each(lambda i, b, g: _dot_tn(jnp.concatenate([v16[i], -u16[i]], axis=0),
                                       jnp.concatenate([kt[b][:, ls(g)], bt[b][:, ls(g)]], axis=0)))
    for i, (b, g) in enumerate(chains):
        bds[b, g] = bds[b, g] * g_last[b][:, ls(g)] + upd[i] * smask
    yield
    ys = [jnp.concatenate(yc_[b * groups:(b + 1) * groups], axis=1) for b in range(nb)]
    y = jnp.concatenate(ys, axis=0)

    inv_d = 1.0 / A_HEAD_DIM
    yc = y - segsum(y) * inv_d
    var = segsum(yc * yc) * inv_d
    yield
    yn = yc * lax.rsqrt(var + GN_EPS) * lnw_ref[...] + lnb_ref[...]
    bonus = segsum(r * kmod * rk_ref[...]) * v
    out = (yn + bonus) * gate
    for b in range(nb):
        y_ref[b, :, 0:A_WIDTH] = out[b * L:(b + 1) * L]

    @pl.when(last)
    def _():
        for b in range(nb):
            for h in range(A_HEADS):
                g, sl = diag(h)
                s_out_ref[b, h] = bds[b, g, sl, sl]


def _mlstm_stages(pb_ref, pg_ref, conv0_ref, c0_ref, n0_ref, m0_ref, wts, tril_ref,
                  y_ref, conv_out_ref, c_ref, n_ref, m_ref, xbuf, *, chunk, first):
    cw_ref, cb_ref, gb_ref, hnw_ref = wts
    L = chunk
    nb = pb_ref.shape[0]
    rows = nb * L
    hist = B_CONV - 1

    @pl.when(first)
    def _():
        for b in range(nb):
            xbuf[b, CARRY - hist:CARRY, :] = conv0_ref[b]
        c_ref[...] = c0_ref[...]
        n_ref[...] = n0_ref[...]
        m_ref[...] = m0_ref[...]

    qk = []
    for b in range(nb):
        x = pb_ref[b, :, 0:2 * B_WIDTH]
        xbuf[b, CARRY:CARRY + L, :] = x
        acc = cb_ref[...] + cw_ref[hist:hist + 1, :] * x
        for j in range(hist):
            acc = acc + cw_ref[j:j + 1, :] * xbuf[b, CARRY - hist + j:CARRY - hist + j + L, :]
        tail = xbuf[b, CARRY + L - hist:CARRY + L, :]
        xbuf[b, CARRY - hist:CARRY, :] = tail
        conv_out_ref[b] = tail
        qk.append(acc * _sigmoid(acc))
        yield

    gates = jnp.concatenate([pg_ref[b] for b in range(nb)], axis=0) + gb_ref[...]
    lane = lax.broadcasted_iota(jnp.int32, (rows, GATE_PAD), 1)
    tril = tril_ref[...]
    cum = sum(_dot(tril, part) for part in _split3(-_softplus(-gates)))
    gcols = jnp.where(lane < B_HEADS, gates, cum)
    grows = gcols.T
    causal = _tril(L, False)
    yield

    chains = [(b, h) for b in range(nb) for h in range(B_HEADS)]
    each = lambda f: [f(i, b, h) for i, (b, h) in enumerate(chains)]
    rs = lambda b: slice(b * L, (b + 1) * L)
    hs = lambda h, base=0: slice(base + h * B_HEAD_DIM, base + (h + 1) * B_HEAD_DIM)

    q = each(lambda i, b, h: qk[b][:, hs(h)])
    k = each(lambda i, b, h: qk[b][:, hs(h, B_WIDTH)] * (B_HEAD_DIM ** -0.5))
    q16 = each(lambda i, b, h: q[i].astype(BF16))
    v16 = each(lambda i, b, h: pb_ref[b, :, hs(h, 2 * B_WIDTH)].astype(BF16))
    li_col = each(lambda i, b, h: gcols[rs(b), h:h + 1])
    b_col = each(lambda i, b, h: gcols[rs(b), B_HEADS + h:B_HEADS + h + 1])
    li_row = each(lambda i, b, h: grows[h:h + 1, rs(b)])
    b_row = each(lambda i, b, h: grows[B_HEADS + h:B_HEADS + h + 1, rs(b)])
    c_prev = each(lambda i, b, h: c_ref[b, h])
    n_prev = each(lambda i, b, h: n_ref[b, h:h + 1, :])
    m_prev = each(lambda i, b, h: m_ref[b, h:h + 1, 0:1])
    yield

    dm = each(lambda i, b, h: jnp.where(causal, b_col[i] - b_row[i] + li_row[i], -jnp.inf))
    inter = each(lambda i, b, h: b_col[i] + m_prev[i])
    mt = each(lambda i, b, h: jnp.maximum(inter[i], jnp.max(dm[i], axis=-1, keepdims=True)))
    wo = each(lambda i, b, h: jnp.exp(inter[i] - mt[i]))
    yield
    s = each(lambda i, b, h: _dot_nt(q16[i], k[i].astype(BF16)) * jnp.exp(dm[i] - mt[i]))
    yield
    num = each(lambda i, b, h: wo[i] * _dot(q16[i], c_prev[i].astype(BF16)) + _dot(s[i].astype(BF16), v16[i]))
    yield
    den = each(lambda i, b, h: wo[i] * jnp.sum(q[i] * n_prev[i], axis=-1, keepdims=True)
               + jnp.sum(s[i], axis=-1, keepdims=True))
    hh = each(lambda i, b, h: num[i] * (1.0 / jnp.maximum(jnp.abs(den[i]), jnp.exp(-mt[i]))))
    yield

    m_new = each(lambda i, b, h: mt[i][L - 1:L, :])
    b_last = each(lambda i, b, h: b_col[i][L - 1:L, :])
    kw = each(lambda i, b, h: k[i] * jnp.exp(b_last[i] - b_col[i] + li_col[i] - m_new[i]))
    dec = each(lambda i, b, h: jnp.exp(b_last[i] + m_prev[i] - m_new[i]))
    yield
    c_new = each(lambda i, b, h: dec[i] * c_prev[i] + _dot_tn(kw[i].astype(BF16), v16[i]))
    n_new = each(lambda i, b, h: dec[i] * n_prev[i] + jnp.sum(kw[i], axis=0, keepdims=True))
    yield
    out = each(lambda i, b, h: hh[i] * lax.rsqrt(jnp.mean(hh[i] * hh[i], axis=-1, keepdims=True) + RMS_EPS)
               * hnw_ref[:, hs(h)] * _sigmoid(pb_ref[b, :, hs(h, 3 * B_WIDTH)]))
    for i, (b, h) in enumerate(chains):
        c_ref[b, h] = c_new[i]
        n_ref[b, h:h + 1, :] = n_new[i]
        m_ref[b, h:h + 1, :] = jnp.broadcast_to(m_new[i], (1, B_HEAD_DIM))
        y_ref[b, :, hs(h, A_WIDTH)] = out[i]


N_RWKV_W = 11
N_MLSTM_W = 4
N_CONSTS = 7


def _mix_kernel(*refs, chunk, hg):
    it = iter(refs)
    take = lambda n: [next(it) for _ in range(n)]
    pa_ref, pb_ref, pg_ref, shift_ref, s0_ref, conv0_ref, c0_ref, n0_ref, m0_ref = take(9)
    rwkv_w, mlstm_w, consts = take(N_RWKV_W), take(N_MLSTM_W), take(N_CONSTS)
    y_ref, shift_out_ref, s_out_ref, conv_out_ref, c_ref, n_ref, m_ref, pbuf, bds, xbuf = take(10)
    first = pl.program_id(1) == 0
    last = pl.program_id(1) == pl.num_programs(1) - 1
    bodies = [
        _rwkv_stages(pa_ref, shift_ref, s0_ref, rwkv_w, consts, y_ref, shift_out_ref, s_out_ref, pbuf, bds,
                     chunk=chunk, hg=hg, first=first, last=last),
        _mlstm_stages(pb_ref, pg_ref, conv0_ref, c0_ref, n0_ref, m0_ref, mlstm_w, consts[1],
                      y_ref, conv_out_ref, c_ref, n_ref, m_ref, xbuf, chunk=chunk, first=first),
    ]
    while bodies:
        for body in list(bodies):
            if next(body, "done") == "done":
                bodies.remove(body)


def _mix(pa, pb, pg, l, sl, states, rwkv_w, mlstm_w, chunk):
    shift0, s0, conv0, c0, n0, m0 = states
    bn, t, _ = pa.shape
    nb = MIX_STREAMS
    hg = _rwkv_heads_per_group(chunk)
    hist = B_CONV - 1
    consts = _mix_consts(chunk, nb)
    full = lambda a: pl.BlockSpec(a.shape, lambda i, c: (0,) * a.ndim)
    tok = lambda n: pl.BlockSpec((nb, chunk, n), lambda i, c: (i, c, 0))
    per_stream = lambda *dims: pl.BlockSpec((nb,) + dims, lambda i, c: (i,) + (0,) * len(dims))
    state_shapes = [(1, A_COLS), (A_HEADS, A_HEAD_DIM, A_HEAD_DIM), (hist, 2 * B_WIDTH),
                    (B_HEADS, B_HEAD_DIM, B_HEAD_DIM), (B_HEADS, B_HEAD_DIM), (B_HEADS, B_HEAD_DIM)]
    return pl.pallas_call(
        functools.partial(_mix_kernel, chunk=chunk, hg=hg),
        grid=(bn // nb, t // chunk),
        in_specs=[tok(A_COLS), tok(B_MAIN), tok(GATE_PAD)]
        + [_state_spec(a, sl, nb) for a in states]
        + [_layer_spec(a, l) for a in rwkv_w + mlstm_w] + [full(a) for a in consts],
        out_specs=[tok(D_MIX)] + [per_stream(*dims) for dims in state_shapes],
        out_shape=[jax.ShapeDtypeStruct((bn, t, D_MIX), F32)]
        + [jax.ShapeDtypeStruct((bn,) + dims, F32) for dims in state_shapes],
        scratch_shapes=[
            pltpu.VMEM((nb, CARRY + chunk, A_COLS), F32),
            pltpu.VMEM((nb, A_HEADS // hg, hg * A_HEAD_DIM, hg * A_HEAD_DIM), F32),
            pltpu.VMEM((nb, CARRY + chunk, 2 * B_WIDTH), F32),
        ],
        compiler_params=pltpu.CompilerParams(
            dimension_semantics=("arbitrary", "arbitrary"), vmem_limit_bytes=VMEM_LIMIT),
        name="mixers",
    )(pa, pb, pg, *states, *rwkv_w, *mlstm_w, *consts)


def _ffn_kernel(x_ref, y_ref, f0_ref, wout_ref, g_ref, wup_ref, cw_ref, cb_ref,
                wdown_ref, gfin_ref, o_ref, fout_ref, ubuf, *, tile, final):
    nb = x_ref.shape[0]
    hist = FFN_CONV - 1
    rows_of = lambda ref: ref[0] if nb == 1 else jnp.concatenate([ref[b] for b in range(nb)], axis=0)

    @pl.when(pl.program_id(1) == 0)
    def _():
        for b in range(nb):
            ubuf[b, CARRY - hist:CARRY, :] = f0_ref[b]

    x1 = rows_of(x_ref) + _dot(rows_of(y_ref).astype(BF16), wout_ref[...])
    h2 = _rms(x1, g_ref[...]).astype(BF16)
    acts = []
    for lo, hi in zip(FF_BOUNDS[:-1], FF_BOUNDS[1:]):
        cols = slice(lo, hi)
        u = _dot(h2, wup_ref[:, cols])
        gate = _dot(h2, wup_ref[:, D_FF + lo:D_FF + hi])
        convs = []
        for b in range(nb):
            ub = u[b * tile:(b + 1) * tile]
            ubuf[b, CARRY:CARRY + tile, cols] = ub
            acc = cb_ref[:, cols] + cw_ref[hist:hist + 1, cols] * ub
            for j in range(hist):
                acc = acc + cw_ref[j:j + 1, cols] * ubuf[b, CARRY - hist + j:CARRY - hist + j + tile, cols]
            tail = ubuf[b, CARRY + tile - hist:CARRY + tile, cols]
            ubuf[b, CARRY - hist:CARRY, cols] = tail
            fout_ref[b, :, cols] = tail
            convs.append(acc)
        conv = convs[0] if nb == 1 else jnp.concatenate(convs, axis=0)
        acts.append((conv * _sigmoid(conv) * gate).astype(BF16))
    x2 = x1 + _dot(jnp.concatenate(acts, axis=1), wdown_ref[...])
    out = _rms(x2, gfin_ref[...]) if final else x2
    for b in range(nb):
        o_ref[b] = out[b * tile:(b + 1) * tile]


def _ffn(x, y, l, sl, f0, wts, gfin, nb, tile, final):
    bn, t, _ = x.shape
    hist = FFN_CONV - 1
    tok = pl.BlockSpec((nb, tile, D_MODEL), lambda i, c: (i, c, 0))
    fspec = pl.BlockSpec((nb, hist, D_FF), lambda i, c: (i, 0, 0))
    return pl.pallas_call(
        functools.partial(_ffn_kernel, tile=tile, final=final),
        grid=(bn // nb, t // tile),
        in_specs=[tok, tok, _state_spec(f0, sl, nb)]
        + [_layer_spec(a, l, resident=True) for a in wts] + [_layer_spec(gfin, 0, resident=True)],
        out_specs=[tok, fspec],
        out_shape=[
            jax.ShapeDtypeStruct((bn, t, D_MODEL), F32),
            jax.ShapeDtypeStruct((bn, hist, D_FF), F32),
        ],
        scratch_shapes=[pltpu.VMEM((nb, CARRY + tile, D_FF), F32)],
        compiler_params=pltpu.CompilerParams(
            dimension_semantics=("arbitrary", "arbitrary"), vmem_limit_bytes=VMEM_LIMIT),
        name="out_ffn",
    )(x, y, f0, *wts, gfin)


def _layer(x, l, sl, st, w, *, in_tile, mix_chunk, ffn_streams, ffn_tile, final):
    bn, t, _ = x.shape
    pa, pb, pg = _in_proj(x.reshape(bn * t, D_MODEL), l, w["norm_mix"], w["w_in"], w["w_gate"], in_tile)
    pa = pa.reshape(bn, t, A_COLS)
    pb = pb.reshape(bn, t, B_MAIN)
    pg = pg.reshape(bn, t, GATE_PAD)
    y, *mix_states = _mix(pa, pb, pg, l, sl, st[:6], w["rwkv"], w["mlstm"], mix_chunk)
    x, fconv1 = _ffn(x, y, l, sl, st[6], w["ffn"], w["norm_final"], ffn_streams, ffn_tile, final)
    return x, tuple(mix_states) + (fconv1,)


def kernel(x_prompt, x_sample, state_rwkv_shift, state_rwkv_wkv, state_mlstm_conv, state_mlstm_C,
           state_mlstm_n, state_mlstm_m, state_ffn_conv, meta_tokens, norm_mix, w_in, a_mu, a_w0, a_w2,
           a_a0, a_a2, a_g2, a_k_k, a_k_a, a_r_k, a_ln_w, a_ln_b, b_conv_w, b_conv_b, b_i_bias, b_f_bias,
           b_hn_w, w_out, norm_ffn, w_up, ffn_conv_w, ffn_conv_b, w_down, norm_final):
    n_prompt = x_prompt.shape[0]
    n_sample = x_sample.shape[0]
    n_lead = n_prompt + n_sample
    assert x_sample.shape[1] == N_META
    assert n_prompt == MIX_STREAMS and n_lead % MIX_STREAMS == 0

    bf = lambda a: a.astype(BF16)
    vec = lambda a: a[:, None, :]
    n_gate = 2 * B_HEADS
    gate_pad = ((0, 0), (0, GATE_PAD - n_gate))
    w = {
        "norm_mix": vec(norm_mix),
        "w_in": bf(w_in),
        "w_gate": bf(jnp.pad(w_in[:, :, A_COLS + B_MAIN:], ((0, 0),) + gate_pad)),
        "rwkv": [vec(a_mu), vec(a_w0), bf(a_w2), vec(a_a0), bf(a_a2), bf(a_g2), vec(a_k_k), vec(a_k_a),
                 vec(a_r_k), vec(a_ln_w), vec(a_ln_b)],
        "mlstm": [b_conv_w, vec(b_conv_b), vec(jnp.pad(jnp.concatenate([b_i_bias, b_f_bias], axis=1), gate_pad)),
                  vec(b_hn_w)],
        "ffn": (bf(w_out), vec(norm_ffn), bf(w_up), ffn_conv_w, vec(ffn_conv_b), bf(w_down)),
        "norm_final": norm_final.reshape(1, 1, D_MODEL),
    }
    assert len(w["rwkv"]) == N_RWKV_W and len(w["mlstm"]) == N_MLSTM_W

    lead_pad = lambda s: jnp.pad(s, ((0, 0), (n_prompt, 0)) + ((0, 0),) * (s.ndim - 2))
    st_lead_in = (
        lead_pad(state_rwkv_shift)[:, :, None, :],
        lead_pad(state_rwkv_wkv),
        lead_pad(state_mlstm_conv),
        lead_pad(state_mlstm_C),
        lead_pad(state_mlstm_n),
        jnp.broadcast_to(lead_pad(state_mlstm_m)[..., None], (DEPTH, n_lead, B_HEADS, B_HEAD_DIM)),
        lead_pad(state_ffn_conv),
    )
    meta = jnp.broadcast_to(meta_tokens[None].astype(x_prompt.dtype), (n_prompt, N_META, D_MODEL))
    x_lead = jnp.concatenate([meta, x_sample], axis=0)
    x_main = x_prompt

    main_states, lead_states = [], []
    for l in range(DEPTH):
        final = l == DEPTH - 1
        x_lead, st_lead = _layer(x_lead, l, l, st_lead_in, w, in_tile=n_lead * N_META, mix_chunk=N_META,
                                 ffn_streams=n_lead, ffn_tile=N_META, final=final)
        x_main, st_main = _layer(x_main, l, 0, tuple(s[None] for s in st_lead), w, in_tile=MAIN_TILE,
                                 mix_chunk=MAIN_CHUNK, ffn_streams=1, ffn_tile=MAIN_TILE, final=final)
        main_states.append(st_main)
        lead_states.append(st_lead)

    def collect(per_layer, first):
        shift, wkv, bconv, c, n, m, fconv = (jnp.stack([st[i] for st in per_layer])[:, first:] for i in range(7))
        return (shift[:, :, 0, :], wkv, bconv, c, n, m[..., 0], fconv)

    return (x_main, x_lead[n_prompt:]) + collect(main_states, 0) + collect(lead_states, n_prompt)
```

```python
import functools
import math

import numpy as np

import jax
import jax.numpy as jnp
from jax import lax
from jax.experimental import pallas as pl
from jax.experimental.pallas import tpu as pltpu

D_MODEL = 1024
DEPTH = 2
N_META = 16
A_HEADS = 8
A_HEAD_DIM = 64
A_WIDTH = 512
A_DECAY_LORA = 64
A_AAA_LORA = 64
A_GATE_LORA = 128
A_COLS = 1792
B_HEADS = 4
B_HEAD_DIM = 128
B_WIDTH = 512
B_CONV = 4
B_MAIN = 4 * B_WIDTH
GATE_PAD = 128
D_MIX = A_WIDTH + B_WIDTH
D_FF = 2816
FFN_CONV = 3
RMS_EPS = 1e-6
GN_EPS = 64e-5
CARRY = 8
ROW_BLOCK = 16
MXU_DIM = 256
MIX_STREAMS = 4
FF_BOUNDS = (0, 6 * MXU_DIM, D_FF)
MAIN_TILE = 512
MAIN_CHUNK = 64
MAIN_SUBCHUNKS = 2
VMEM_LIMIT = 56 * 1024 * 1024

F32 = jnp.float32
BF16 = jnp.bfloat16
NT_DIMS = (((1,), (1,)), ((), ()))
TN_DIMS = (((0,), (0,)), ((), ()))


def _dot(a, b):
    return jnp.dot(a, b, preferred_element_type=F32)


def _dot_nt(a, b):
    return lax.dot_general(a, b, NT_DIMS, preferred_element_type=F32)


def _dot_tn(a, b):
    return lax.dot_general(a, b, TN_DIMS, preferred_element_type=F32)


def _sigmoid(x):
    return 0.5 * jnp.tanh(0.5 * x) + 0.5


def _softplus(x):
    return jnp.maximum(x, 0.0) + jnp.log(1.0 + jnp.exp(-jnp.abs(x)))


def _rms(x, g):
    return x * lax.rsqrt(jnp.mean(x * x, axis=-1, keepdims=True) + RMS_EPS) * g


def _tril(n, strict):
    row = lax.broadcasted_iota(jnp.int32, (n, n), 0)
    col = lax.broadcasted_iota(jnp.int32, (n, n), 1)
    return (col < row) if strict else (col <= row)


def _split3(x):
    hi = x.astype(BF16)
    rest = x - hi.astype(F32)
    mid = rest.astype(BF16)
    lo = (rest - mid.astype(F32)).astype(BF16)
    return hi, mid, lo


def _shift_rows(x, prev, s):
    rolled = pltpu.roll(x, s, 0)
    from_prev = lax.broadcasted_iota(jnp.int32, prev.shape, 0) < s
    head = jnp.where(from_prev, pltpu.roll(prev, s, 0), rolled[0:CARRY])
    return jnp.concatenate([head, rolled[CARRY:]], axis=0)


def _rowwise(fn, n_rows, *xs):
    pieces = [fn(*[x[i:i + ROW_BLOCK] for x in xs]) for i in range(0, n_rows, ROW_BLOCK)]
    return [jnp.concatenate(col, axis=0) for col in zip(*pieces)]


def _layer_spec(a, l, resident=False):
    index = lambda *grid: (l,) + (0,) * (a.ndim - 1)
    if resident:
        return pl.BlockSpec((None,) + a.shape[1:], index, pipeline_mode=pl.Buffered(1))
    return pl.BlockSpec((None,) + a.shape[1:], index)


def _state_spec(a, l, nb):
    return pl.BlockSpec((None, nb) + a.shape[2:], lambda i, c: (l, i) + (0,) * (a.ndim - 2))


def _in_proj_kernel(x_ref, g_ref, w_ref, wg_ref, oa_ref, ob_ref, og_ref):
    h = _rms(x_ref[...], g_ref[...]).astype(BF16)
    oa_ref[...] = _dot(h, w_ref[:, 0:A_COLS])
    ob_ref[...] = _dot(h, w_ref[:, A_COLS:A_COLS + B_MAIN])
    og_ref[...] = _dot(h, wg_ref[...])


def _in_proj(x2d, l, g, w, wg, tm):
    m = x2d.shape[0]
    return pl.pallas_call(
        _in_proj_kernel,
        grid=(m // tm,),
        in_specs=[
            pl.BlockSpec((tm, D_MODEL), lambda i: (i, 0)),
            _layer_spec(g, l), _layer_spec(w, l, resident=True), _layer_spec(wg, l, resident=True),
        ],
        out_specs=[
            pl.BlockSpec((tm, A_COLS), lambda i: (i, 0)),
            pl.BlockSpec((tm, B_MAIN), lambda i: (i, 0)),
            pl.BlockSpec((tm, GATE_PAD), lambda i: (i, 0)),
        ],
        out_shape=[
            jax.ShapeDtypeStruct((m, A_COLS), F32),
            jax.ShapeDtypeStruct((m, B_MAIN), F32),
            jax.ShapeDtypeStruct((m, GATE_PAD), F32),
        ],
        compiler_params=pltpu.CompilerParams(
            dimension_semantics=("arbitrary",), vmem_limit_bytes=VMEM_LIMIT),
        name="in_proj",
    )(x2d, g, w, wg)


def _rwkv_heads_per_group(chunk):
    return min(A_HEADS, MXU_DIM // chunk)


def _mix_consts(chunk, streams):
    hg = _rwkv_heads_per_group(chunk)
    hl, w, rows = hg * chunk, hg * A_HEAD_DIM, streams * chunk
    ix = lambda n: (np.arange(n)[:, None], np.arange(n)[None, :])
    r, c = ix(MXU_DIM)
    seg = r // A_HEAD_DIM == c // A_HEAD_DIM
    r, c = ix(rows)
    tril = (r // chunk == c // chunk) & (c <= r)
    r, c = np.arange(2 * chunk)[:, None], np.arange(hl)[None, :]
    amask = np.where(r < chunk, c % chunk < r, c % chunk <= r - chunk)
    r, c = np.arange(chunk)[:, None], np.arange(hl)[None, :]
    eye = c % chunk == r
    r, c = np.arange(hl)[:, None], np.arange(w)[None, :]
    bdl = r // chunk == c // A_HEAD_DIM
    r, c = ix(hl)
    bdsq = r // chunk == c // chunk
    r, c = ix(w)
    smask = r // A_HEAD_DIM == c // A_HEAD_DIM
    return (jnp.asarray(seg, BF16), jnp.asarray(tril, BF16), jnp.asarray(amask, F32), jnp.asarray(eye, F32),
            jnp.asarray(bdl, BF16), jnp.asarray(bdsq, BF16), jnp.asarray(smask, F32))


def _rwkv_stages(p_ref, shift_ref, s0_ref, wts, consts, y_ref, shift_out_ref, s_out_ref, pbuf, bds,
                 *, chunk, hg, row0, first, last):
    mu_ref, w0_ref, w2_ref, a0_ref, a2_ref, g2_ref, kk_ref, ka_ref, rk_ref, lnw_ref, lnb_ref = wts
    seg_ref, tril_ref, amask_ref, eye_ref, bdl_ref, bdsq_ref, smask_ref = consts
    L = chunk
    nb = p_ref.shape[0]
    rows = nb * L
    hl = hg * L
    wid = hg * A_HEAD_DIM
    groups = A_HEADS // hg
    diag = lambda h: (h // hg, slice((h % hg) * A_HEAD_DIM, (h % hg + 1) * A_HEAD_DIM))
    win = slice(row0, row0 + L)

    if first is not None:
        @pl.when(first)
        def _():
            bds[...] = jnp.zeros(bds.shape, F32)
            for b in range(nb):
                pbuf[b] = jnp.zeros(pbuf.shape[1:], F32)
                pbuf[b, CARRY - 1:CARRY, :] = shift_ref[b]
                for h in range(A_HEADS):
                    g, sl = diag(h)
                    bds[b, g, sl, sl] = s0_ref[b, h]

    ps, prevs = [], []
    for b in range(nb):
        pb = p_ref[b, win, :]
        prevs.append(_shift_rows(pb, pbuf[b], 1))
        pbuf[b] = pb[L - CARRY:L, :]
        shift_out_ref[b] = pb[L - 1:L, :]
        ps.append(pb)
    i1, i2, i3 = A_WIDTH, 2 * A_WIDTH, 3 * A_WIDTH
    i4, i5 = i3 + A_DECAY_LORA, i3 + A_DECAY_LORA + A_AAA_LORA
    mu = mu_ref[...]

    def token_shift(p, prev):
        pm = p + (prev - p) * mu
        return (pm[:, :i1], pm[:, i1:i2], pm[:, i2:i3], jnp.tanh(pm[:, i3:i4]).astype(BF16),
                pm[:, i4:i5].astype(BF16), _sigmoid(pm[:, i5:]).astype(BF16))

    r, k, v, wl16, al16, gl16 = _rowwise(token_shift, rows, jnp.concatenate(ps, axis=0), jnp.concatenate(prevs, axis=0))
    yield
    seg = seg_ref[...]
    half = A_WIDTH // 2
    segsum = lambda t16: jnp.concatenate([_dot(t16[:, :half], seg), _dot(t16[:, half:], seg)], axis=1)
    z = w0_ref[...] + _dot(wl16, w2_ref[...])
    a_pre = a0_ref[...] + _dot(al16, a2_ref[...])
    gate = _dot(gl16, g2_ref[...])
    yield
    k_k, k_a, r_k = kk_ref[...], ka_ref[...], rk_ref[...]

    def decay_and_keys(k, z, a_pre):
        lw = -math.exp(-0.5) * _sigmoid(z)
        lw_hi = lw.astype(BF16)
        a = _sigmoid(a_pre)
        kk0 = k * k_k
        return (lw, lw_hi, (lw - lw_hi.astype(F32)).astype(BF16), a, kk0, (kk0 * kk0).astype(BF16),
                k * (1.0 + (a - 1.0) * k_a))

    lw, lw_hi, lw_lo, a, kk0, kk_sq16, kmod = _rowwise(decay_and_keys, rows, k, z, a_pre)
    yield
    tril = tril_ref[...]
    cum = _dot(tril, lw_hi) + _dot(tril, lw_lo)
    kk_ss = segsum(kk_sq16)
    yield

    rs = lambda b: slice(b * L, (b + 1) * L)
    cum_last = [cum[(b + 1) * L - 1:(b + 1) * L] for b in range(nb)]
    g_last = [jnp.exp(cum_last[b]) for b in range(nb)]

    def scaled_operands(b):
        def fn(r, v, kk0, kk_ss, a, kmod, lw, cum):
            kk = kk0 * lax.rsqrt(kk_ss + 1e-12)
            bvec = kk * a
            g_inv = jnp.exp(-cum)
            g_tail = jnp.exp(cum_last[b] - cum)
            return ((kk * jnp.exp(cum - lw)).astype(BF16), (r * jnp.exp(cum)).astype(BF16),
                    (kmod * g_inv).astype(BF16), (bvec * g_inv).astype(BF16), v.astype(BF16),
                    (kmod * g_tail).astype(BF16), (bvec * g_tail).astype(BF16), (r * kmod * r_k).astype(BF16))
        return fn

    per_stream = [_rowwise(scaled_operands(b), L, *[t[rs(b)] for t in (r, v, kk0, kk_ss, a, kmod, lw, cum)])
                  for b in range(nb)]
    kkg_s, rg_s, kd_s, bd_s, v_s, kt, bt, bonus_s = zip(*per_stream)
    bonus_ss = segsum(jnp.concatenate(bonus_s, axis=0))
    yield

    keep = amask_ref[...] > 0.5
    eye = eye_ref[...]
    bdl = bdl_ref[...]
    bdsq = bdsq_ref[...]
    smask = smask_ref[...]
    lane_bd = lambda t: jnp.concatenate([t] * hg, axis=0) * bdl
    sq_bd = lambda t: jnp.concatenate([t] * hg, axis=0) * bdsq

    chains = [(b, g) for b in range(nb) for g in range(groups)]
    ls = lambda g: slice(g * wid, (g + 1) * wid)
    each = lambda f: [f(i, b, g) for i, (b, g) in enumerate(chains)]

    kkg = each(lambda i, b, g: kkg_s[b][:, ls(g)])
    rg = each(lambda i, b, g: rg_s[b][:, ls(g)])
    v16 = each(lambda i, b, g: v_s[b][:, ls(g)])
    lhs = each(lambda i, b, g: jnp.concatenate([kkg[i], rg[i]], axis=0))
    a_k = each(lambda i, b, g: jnp.where(keep, _dot_nt(lhs[i], lane_bd(kd_s[b][:, ls(g)])), 0.0))
    yield
    a_b = each(lambda i, b, g: jnp.where(keep, _dot_nt(lhs[i], lane_bd(bd_s[b][:, ls(g)])), 0.0))
    a_kk = each(lambda i, b, g: a_k[i][:L].astype(BF16))
    a_rk = each(lambda i, b, g: a_k[i][L:].astype(BF16))
    a_rb = each(lambda i, b, g: a_b[i][L:].astype(BF16))
    yield

    npow = each(lambda i, b, g: -a_b[i][:L])
    tinv = each(lambda i, b, g: eye + npow[i])
    n16 = each(lambda i, b, g: npow[i].astype(BF16))
    npow = each(lambda i, b, g: _dot(n16[i], sq_bd(n16[i])))
    yield
    span = 2
    while span < L:
        n16 = each(lambda i, b, g: npow[i].astype(BF16))
        t_bd = each(lambda i, b, g: sq_bd(tinv[i].astype(BF16)))
        if 2 * span < L:
            both = each(lambda i, b, g: _dot(n16[i], jnp.concatenate([sq_bd(n16[i]), t_bd[i]], axis=1)))
            npow = each(lambda i, b, g: both[i][:, :hl])
            prod = each(lambda i, b, g: both[i][:, hl:])
        else:
            prod = each(lambda i, b, g: _dot(n16[i], t_bd[i]))
        tinv = each(lambda i, b, g: tinv[i] + prod[i])
        span *= 2
        yield

    yield READS_STATE
    s16 = each(lambda i, b, g: bds[b, g].astype(BF16))
    v_bd = each(lambda i, b, g: lane_bd(v16[i]))
    rhs = each(lambda i, b, g: (_dot_nt(kkg[i], s16[i]) + _dot(a_kk[i], v_bd[i])).astype(BF16))
    yield
    u16 = each(lambda i, b, g: _dot(tinv[i].astype(BF16), lane_bd(rhs[i])).astype(BF16))
    yield
    yc_ = each(lambda i, b, g: _dot_nt(rg[i], s16[i]) + _dot(a_rk[i], v_bd[i]) - _dot(a_rb[i], lane_bd(u16[i])))
    yield
    upd = each(lambda i, b, g: _dot_tn(jnp.concatenate([v16[i], -u16[i]], axis=0),
                                       jnp.concatenate([kt[b][:, ls(g)], bt[b][:, ls(g)]], axis=0)))
    for i, (b, g) in enumerate(chains):
        bds[b, g] = bds[b, g] * g_last[b][:, ls(g)] + upd[i] * smask
    yield WROTE_STATE
    ys = [jnp.concatenate(yc_[b * groups:(b + 1) * groups], axis=1) for b in range(nb)]
    y = jnp.concatenate(ys, axis=0)

    inv_d = 1.0 / A_HEAD_DIM
    y_sum = segsum(y.astype(BF16))

    def centre(y, y_sum):
        yc = y - y_sum * inv_d
        return yc, (yc * yc).astype(BF16)

    yc, yc_sq16 = _rowwise(centre, rows, y, y_sum)
    var_sum = segsum(yc_sq16)
    yield
    ln_w, ln_b = lnw_ref[...], lnb_ref[...]

    def finish(yc, var_sum, bonus_ss, v, gate):
        return ((yc * lax.rsqrt(var_sum * inv_d + GN_EPS) * ln_w + ln_b + bonus_ss * v) * gate,)

    out, = _rowwise(finish, rows, yc, var_sum, bonus_ss, v, gate)
    for b in range(nb):
        y_ref[b, win, 0:A_WIDTH] = out[b * L:(b + 1) * L]

    if last is not None:
        @pl.when(last)
        def _():
            for b in range(nb):
                for h in range(A_HEADS):
                    g, sl = diag(h)
                    s_out_ref[b, h] = bds[b, g, sl, sl]


def _mlstm_stages(pb_ref, pg_ref, conv0_ref, c0_ref, n0_ref, m0_ref, wts, tril_ref,
                  y_ref, conv_out_ref, c_ref, n_ref, m_ref, xbuf, *, chunk, row0, first):
    cw_ref, cb_ref, gb_ref, hnw_ref = wts
    win = slice(row0, row0 + chunk)
    L = chunk
    nb = pb_ref.shape[0]
    rows = nb * L
    hist = B_CONV - 1

    if first is not None:
        @pl.when(first)
        def _():
            for b in range(nb):
                xbuf[b] = jnp.zeros(xbuf.shape[1:], F32)
                xbuf[b, CARRY - hist:CARRY, :] = conv0_ref[b]
            c_ref[...] = c0_ref[...]
            n_ref[...] = n0_ref[...]
            m_ref[...] = m0_ref[...]

    qk = []
    for b in range(nb):
        x = pb_ref[b, win, 0:2 * B_WIDTH]
        prev = xbuf[b]
        acc = cb_ref[...] + cw_ref[hist:hist + 1, :] * x
        for j in range(hist):
            acc = acc + cw_ref[j:j + 1, :] * _shift_rows(x, prev, hist - j)
        xbuf[b] = x[L - CARRY:L, :]
        conv_out_ref[b] = x[L - hist:L, :]
        qk.append(acc * _sigmoid(acc))
        yield

    gates = jnp.concatenate([pg_ref[b, win, :] for b in range(nb)], axis=0) + gb_ref[...]
    lane = lax.broadcasted_iota(jnp.int32, (rows, GATE_PAD), 1)
    tril = tril_ref[...]
    cum = sum(_dot(tril, part) for part in _split3(-_softplus(-gates)))
    gcols = jnp.where(lane < B_HEADS, gates, cum)
    grows = gcols.T
    causal = _tril(L, False)
    yield

    chains = [(b, h) for b in range(nb) for h in range(B_HEADS)]
    each = lambda f: [f(i, b, h) for i, (b, h) in enumerate(chains)]
    rs = lambda b: slice(b * L, (b + 1) * L)
    hs = lambda h, base=0: slice(base + h * B_HEAD_DIM, base + (h + 1) * B_HEAD_DIM)

    q = each(lambda i, b, h: qk[b][:, hs(h)])
    k = each(lambda i, b, h: qk[b][:, hs(h, B_WIDTH)] * (B_HEAD_DIM ** -0.5))
    q16 = each(lambda i, b, h: q[i].astype(BF16))
    v16 = each(lambda i, b, h: pb_ref[b, win, hs(h, 2 * B_WIDTH)].astype(BF16))
    li_col = each(lambda i, b, h: gcols[rs(b), h:h + 1])
    b_col = each(lambda i, b, h: gcols[rs(b), B_HEADS + h:B_HEADS + h + 1])
    li_row = each(lambda i, b, h: grows[h:h + 1, rs(b)])
    b_row = each(lambda i, b, h: grows[B_HEADS + h:B_HEADS + h + 1, rs(b)])
    yield READS_STATE
    c_prev = each(lambda i, b, h: c_ref[b, h])
    n_prev = each(lambda i, b, h: n_ref[b, h:h + 1, :])
    m_prev = each(lambda i, b, h: m_ref[b, h:h + 1, 0:1])

    dm = each(lambda i, b, h: jnp.where(causal, b_col[i] - b_row[i] + li_row[i], -jnp.inf))
    inter = each(lambda i, b, h: b_col[i] + m_prev[i])
    mt = each(lambda i, b, h: jnp.maximum(inter[i], jnp.max(dm[i], axis=-1, keepdims=True)))
    wo = each(lambda i, b, h: jnp.exp(inter[i] - mt[i]))
    yield
    s = each(lambda i, b, h: _dot_nt(q16[i], k[i].astype(BF16)) * jnp.exp(dm[i] - mt[i]))
    yield
    num = each(lambda i, b, h: wo[i] * _dot(q16[i], c_prev[i].astype(BF16)) + _dot(s[i].astype(BF16), v16[i]))
    yield
    den = each(lambda i, b, h: wo[i] * jnp.sum(q[i] * n_prev[i], axis=-1, keepdims=True)
               + jnp.sum(s[i], axis=-1, keepdims=True))
    hh = each(lambda i, b, h: num[i] * (1.0 / jnp.maximum(jnp.abs(den[i]), jnp.exp(-mt[i]))))
    yield

    m_new = each(lambda i, b, h: mt[i][L - 1:L, :])
    b_last = each(lambda i, b, h: b_col[i][L - 1:L, :])
    kw = each(lambda i, b, h: k[i] * jnp.exp(b_last[i] - b_col[i] + li_col[i] - m_new[i]))
    dec = each(lambda i, b, h: jnp.exp(b_last[i] + m_prev[i] - m_new[i]))
    yield
    c_new = each(lambda i, b, h: dec[i] * c_prev[i] + _dot_tn(kw[i].astype(BF16), v16[i]))
    n_new = each(lambda i, b, h: dec[i] * n_prev[i] + jnp.sum(kw[i], axis=0, keepdims=True))
    yield
    out = each(lambda i, b, h: hh[i] * lax.rsqrt(jnp.mean(hh[i] * hh[i], axis=-1, keepdims=True) + RMS_EPS)
               * hnw_ref[:, hs(h)] * _sigmoid(pb_ref[b, win, hs(h, 3 * B_WIDTH)]))
    for i, (b, h) in enumerate(chains):
        c_ref[b, h] = c_new[i]
        n_ref[b, h:h + 1, :] = n_new[i]
        m_ref[b, h:h + 1, :] = jnp.broadcast_to(m_new[i], (1, B_HEAD_DIM))
        y_ref[b, win, hs(h, A_WIDTH)] = out[i]


READS_STATE, WROTE_STATE, DONE = "reads_state", "wrote_state", "done"
N_RWKV_W = 11
N_MLSTM_W = 4
N_CONSTS = 7


def _mix_kernel(*refs, chunk, hg):
    it = iter(refs)
    take = lambda n: [next(it) for _ in range(n)]
    pa_ref, pb_ref, pg_ref, shift_ref, s0_ref, conv0_ref, c0_ref, n0_ref, m0_ref = take(9)
    rwkv_w, mlstm_w, consts = take(N_RWKV_W), take(N_MLSTM_W), take(N_CONSTS)
    y_ref, shift_out_ref, s_out_ref, conv_out_ref, c_ref, n_ref, m_ref, pbuf, bds, xbuf = take(10)
    first = pl.program_id(1) == 0
    last = pl.program_id(1) == pl.num_programs(1) - 1
    subs = pa_ref.shape[1] // chunk
    bodies = []
    for sub in range(subs):
        is_first = first if sub == 0 else None
        bodies.append([
            _rwkv_stages(pa_ref, shift_ref, s0_ref, rwkv_w, consts, y_ref, shift_out_ref, s_out_ref, pbuf, bds,
                         chunk=chunk, hg=hg, row0=sub * chunk, first=is_first,
                         last=last if sub == subs - 1 else None),
            _mlstm_stages(pb_ref, pg_ref, conv0_ref, c0_ref, n0_ref, m0_ref, mlstm_w, consts[1],
                          y_ref, conv_out_ref, c_ref, n_ref, m_ref, xbuf, chunk=chunk, row0=sub * chunk,
                          first=is_first),
        ])
    live = [(sub, kind) for sub in range(subs) for kind in range(2)]
    wrote, waiting = set(), set()
    while live:
        for key in list(live):
            sub, kind = key
            if key in waiting and (sub - 1, kind) not in wrote:
                continue
            waiting.discard(key)
            mark = next(bodies[sub][kind], DONE)
            if mark == READS_STATE and sub > 0:
                waiting.add(key)
            elif mark in (WROTE_STATE, DONE):
                wrote.add(key)
                if mark == DONE:
                    live.remove(key)


def _mix(pa, pb, pg, l, sl, states, rwkv_w, mlstm_w, chunk, subs):
    shift0, s0, conv0, c0, n0, m0 = states
    bn, t, _ = pa.shape
    nb = MIX_STREAMS
    hg = _rwkv_heads_per_group(chunk)
    hist = B_CONV - 1
    consts = _mix_consts(chunk, nb)
    full = lambda a: pl.BlockSpec(a.shape, lambda i, c: (0,) * a.ndim)
    tok = lambda n: pl.BlockSpec((nb, subs * chunk, n), lambda i, c: (i, c, 0))
    per_stream = lambda *dims: pl.BlockSpec((nb,) + dims, lambda i, c: (i,) + (0,) * len(dims))
    state_shapes = [(1, A_COLS), (A_HEADS, A_HEAD_DIM, A_HEAD_DIM), (hist, 2 * B_WIDTH),
                    (B_HEADS, B_HEAD_DIM, B_HEAD_DIM), (B_HEADS, B_HEAD_DIM), (B_HEADS, B_HEAD_DIM)]
    return pl.pallas_call(
        functools.partial(_mix_kernel, chunk=chunk, hg=hg),
        grid=(bn // nb, t // (subs * chunk)),
        in_specs=[tok(A_COLS), tok(B_MAIN), tok(GATE_PAD)]
        + [_state_spec(a, sl, nb) for a in states]
        + [_layer_spec(a, l) for a in rwkv_w + mlstm_w] + [full(a) for a in consts],
        out_specs=[tok(D_MIX)] + [per_stream(*dims) for dims in state_shapes],
        out_shape=[jax.ShapeDtypeStruct((bn, t, D_MIX), F32)]
        + [jax.ShapeDtypeStruct((bn,) + dims, F32) for dims in state_shapes],
        scratch_shapes=[
            pltpu.VMEM((nb, CARRY, A_COLS), F32),
            pltpu.VMEM((nb, A_HEADS // hg, hg * A_HEAD_DIM, hg * A_HEAD_DIM), F32),
            pltpu.VMEM((nb, CARRY, 2 * B_WIDTH), F32),
        ],
        compiler_params=pltpu.CompilerParams(
            dimension_semantics=("arbitrary", "arbitrary"), vmem_limit_bytes=VMEM_LIMIT),
        name="mixers",
    )(pa, pb, pg, *states, *rwkv_w, *mlstm_w, *consts)


def _ffn_kernel(x_ref, y_ref, f0_ref, wout_ref, g_ref, wup_ref, cw_ref, cb_ref,
                wdown_ref, gfin_ref, o_ref, fout_ref, ubuf, *, tile, final):
    nb = x_ref.shape[0]
    hist = FFN_CONV - 1
    rows_of = lambda ref: ref[0] if nb == 1 else jnp.concatenate([ref[b] for b in range(nb)], axis=0)

    @pl.when(pl.program_id(1) == 0)
    def _():
        for b in range(nb):
            ubuf[b] = jnp.zeros(ubuf.shape[1:], F32)
            ubuf[b, CARRY - hist:CARRY, :] = f0_ref[b]

    x1 = rows_of(x_ref) + _dot(rows_of(y_ref).astype(BF16), wout_ref[...])
    h2 = _rms(x1, g_ref[...]).astype(BF16)
    acts = []
    for lo, hi in zip(FF_BOUNDS[:-1], FF_BOUNDS[1:]):
        cols = slice(lo, hi)
        u = _dot(h2, wup_ref[:, cols])
        gate = _dot(h2, wup_ref[:, D_FF + lo:D_FF + hi])
        convs = []
        for b in range(nb):
            ub = u[b * tile:(b + 1) * tile]
            prev = ubuf[b, :, cols]
            acc = cb_ref[:, cols] + cw_ref[hist:hist + 1, cols] * ub
            for j in range(hist):
                acc = acc + cw_ref[j:j + 1, cols] * _shift_rows(ub, prev, hist - j)
            ubuf[b, :, cols] = ub[tile - CARRY:tile]
            fout_ref[b, :, cols] = ub[tile - hist:tile]
            convs.append(acc)
        conv = convs[0] if nb == 1 else jnp.concatenate(convs, axis=0)
        acts.append((conv * _sigmoid(conv) * gate).astype(BF16))
    x2 = x1 + _dot(jnp.concatenate(acts, axis=1), wdown_ref[...])
    out = _rms(x2, gfin_ref[...]) if final else x2
    for b in range(nb):
        o_ref[b] = out[b * tile:(b + 1) * tile]


def _ffn(x, y, l, sl, f0, wts, gfin, nb, tile, final):
    bn, t, _ = x.shape
    hist = FFN_CONV - 1
    tok = pl.BlockSpec((nb, tile, D_MODEL), lambda i, c: (i, c, 0))
    fspec = pl.BlockSpec((nb, hist, D_FF), lambda i, c: (i, 0, 0))
    return pl.pallas_call(
        functools.partial(_ffn_kernel, tile=tile, final=final),
        grid=(bn // nb, t // tile),
        in_specs=[tok, tok, _state_spec(f0, sl, nb)]
        + [_layer_spec(a, l, resident=True) for a in wts] + [_layer_spec(gfin, 0, resident=True)],
        out_specs=[tok, fspec],
        out_shape=[
            jax.ShapeDtypeStruct((bn, t, D_MODEL), F32),
            jax.ShapeDtypeStruct((bn, hist, D_FF), F32),
        ],
        scratch_shapes=[pltpu.VMEM((nb, CARRY, D_FF), F32)],
        compiler_params=pltpu.CompilerParams(
            dimension_semantics=("arbitrary", "arbitrary"), vmem_limit_bytes=VMEM_LIMIT),
        name="out_ffn",
    )(x, y, f0, *wts, gfin)


def _layer(x, l, sl, st, w, *, in_tile, mix_chunk, mix_subs, ffn_streams, ffn_tile, final):
    bn, t, _ = x.shape
    pa, pb, pg = _in_proj(x.reshape(bn * t, D_MODEL), l, w["norm_mix"], w["w_in"], w["w_gate"], in_tile)
    pa = pa.reshape(bn, t, A_COLS)
    pb = pb.reshape(bn, t, B_MAIN)
    pg = pg.reshape(bn, t, GATE_PAD)
    y, *mix_states = _mix(pa, pb, pg, l, sl, st[:6], w["rwkv"], w["mlstm"], mix_chunk, mix_subs)
    x, fconv1 = _ffn(x, y, l, sl, st[6], w["ffn"], w["norm_final"], ffn_streams, ffn_tile, final)
    return x, tuple(mix_states) + (fconv1,)


def kernel(x_prompt, x_sample, state_rwkv_shift, state_rwkv_wkv, state_mlstm_conv, state_mlstm_C,
           state_mlstm_n, state_mlstm_m, state_ffn_conv, meta_tokens, norm_mix, w_in, a_mu, a_w0, a_w2,
           a_a0, a_a2, a_g2, a_k_k, a_k_a, a_r_k, a_ln_w, a_ln_b, b_conv_w, b_conv_b, b_i_bias, b_f_bias,
           b_hn_w, w_out, norm_ffn, w_up, ffn_conv_w, ffn_conv_b, w_down, norm_final):
    n_prompt = x_prompt.shape[0]
    n_sample = x_sample.shape[0]
    n_lead = n_prompt + n_sample
    assert x_sample.shape[1] == N_META
    assert n_prompt == MIX_STREAMS and n_lead % MIX_STREAMS == 0

    bf = lambda a: a.astype(BF16)
    vec = lambda a: a[:, None, :]
    n_gate = 2 * B_HEADS
    gate_pad = ((0, 0), (0, GATE_PAD - n_gate))
    w = {
        "norm_mix": vec(norm_mix),
        "w_in": bf(w_in),
        "w_gate": bf(jnp.pad(w_in[:, :, A_COLS + B_MAIN:], ((0, 0),) + gate_pad)),
        "rwkv": [vec(a_mu), vec(a_w0), bf(a_w2), vec(a_a0), bf(a_a2), bf(a_g2), vec(a_k_k), vec(a_k_a),
                 vec(a_r_k), vec(a_ln_w), vec(a_ln_b)],
        "mlstm": [b_conv_w, vec(b_conv_b), vec(jnp.pad(jnp.concatenate([b_i_bias, b_f_bias], axis=1), gate_pad)),
                  vec(b_hn_w)],
        "ffn": (bf(w_out), vec(norm_ffn), bf(w_up), ffn_conv_w, vec(ffn_conv_b), bf(w_down)),
        "norm_final": norm_final.reshape(1, 1, D_MODEL),
    }
    assert len(w["rwkv"]) == N_RWKV_W and len(w["mlstm"]) == N_MLSTM_W

    lead_pad = lambda s: jnp.pad(s, ((0, 0), (n_prompt, 0)) + ((0, 0),) * (s.ndim - 2))
    st_lead_in = (
        lead_pad(state_rwkv_shift)[:, :, None, :],
        lead_pad(state_rwkv_wkv),
        lead_pad(state_mlstm_conv),
        lead_pad(state_mlstm_C),
        lead_pad(state_mlstm_n),
        jnp.broadcast_to(lead_pad(state_mlstm_m)[..., None], (DEPTH, n_lead, B_HEADS, B_HEAD_DIM)),
        lead_pad(state_ffn_conv),
    )
    meta = jnp.broadcast_to(meta_tokens[None].astype(x_prompt.dtype), (n_prompt, N_META, D_MODEL))
    x_lead = jnp.concatenate([meta, x_sample], axis=0)
    x_main = x_prompt

    main_states, lead_states = [], []
    for l in range(DEPTH):
        final = l == DEPTH - 1
        x_lead, st_lead = _layer(x_lead, l, l, st_lead_in, w, in_tile=n_lead * N_META, mix_chunk=N_META,
                                 mix_subs=1, ffn_streams=n_lead, ffn_tile=N_META, final=final)
        x_main, st_main = _layer(x_main, l, 0, tuple(s[None] for s in st_lead), w, in_tile=MAIN_TILE,
                                 mix_chunk=MAIN_CHUNK, mix_subs=MAIN_SUBCHUNKS, ffn_streams=1, ffn_tile=MAIN_TILE, final=final)
        main_states.append(st_main)
        lead_states.append(st_lead)

    def collect(per_layer, first):
        shift, wkv, bconv, c, n, m, fconv = (jnp.stack([st[i] for st in per_layer])[:, first:] for i in range(7))
        return (shift[:, :, 0, :], wkv, bconv, c, n, m[..., 0], fconv)

    return (x_main, x_lead[n_prompt:]) + collect(main_states, 0) + collect(lead_states, n_prompt)
```

```python
import functools
import math

import numpy as np

import jax
import jax.numpy as jnp
from jax import lax
from jax.experimental import pallas as pl
from jax.experimental.pallas import tpu as pltpu

D_MODEL = 1024
DEPTH = 2
N_META = 16
A_HEADS = 8
A_HEAD_DIM = 64
A_WIDTH = 512
A_DECAY_LORA = 64
A_AAA_LORA = 64
A_GATE_LORA = 128
A_COLS = 1792
B_HEADS = 4
B_HEAD_DIM = 128
B_WIDTH = 512
B_CONV = 4
B_MAIN = 4 * B_WIDTH
GATE_PAD = 128
D_MIX = A_WIDTH + B_WIDTH
D_FF = 2816
FFN_CONV = 3
RMS_EPS = 1e-6
GN_EPS = 64e-5
CARRY = 8
ROW_BLOCK = 16
MXU_DIM = 256
MIX_STREAMS = 4
FF_BOUNDS = (0, 6 * MXU_DIM, D_FF)
MAIN_TILE = 512
PROJ_COLS = 512
MAIN_CHUNK = 64
MAIN_SUBCHUNKS = 2
VMEM_LIMIT = 56 * 1024 * 1024

F32 = jnp.float32
BF16 = jnp.bfloat16
NT_DIMS = (((1,), (1,)), ((), ()))
TN_DIMS = (((0,), (0,)), ((), ()))


def _dot(a, b):
    return jnp.dot(a, b, preferred_element_type=F32)


def _dot_nt(a, b):
    return lax.dot_general(a, b, NT_DIMS, preferred_element_type=F32)


def _dot_tn(a, b):
    return lax.dot_general(a, b, TN_DIMS, preferred_element_type=F32)


def _sigmoid(x):
    return 0.5 * jnp.tanh(0.5 * x) + 0.5


def _softplus(x):
    return jnp.maximum(x, 0.0) + jnp.log(1.0 + jnp.exp(-jnp.abs(x)))


def _rms(x, g):
    return x * lax.rsqrt(jnp.mean(x * x, axis=-1, keepdims=True) + RMS_EPS) * g


def _tril(n, strict):
    row = lax.broadcasted_iota(jnp.int32, (n, n), 0)
    col = lax.broadcasted_iota(jnp.int32, (n, n), 1)
    return (col < row) if strict else (col <= row)


def _split3(x):
    hi = x.astype(BF16)
    rest = x - hi.astype(F32)
    mid = rest.astype(BF16)
    lo = (rest - mid.astype(F32)).astype(BF16)
    return hi, mid, lo


def _shift_rows(x, prev, s):
    rolled = pltpu.roll(x, s, 0)
    from_prev = lax.broadcasted_iota(jnp.int32, prev.shape, 0) < s
    head = jnp.where(from_prev, pltpu.roll(prev, s, 0), rolled[0:CARRY])
    return jnp.concatenate([head, rolled[CARRY:]], axis=0)


def _rowwise(fn, n_rows, *xs):
    pieces = [fn(*[x[i:i + ROW_BLOCK] for x in xs]) for i in range(0, n_rows, ROW_BLOCK)]
    return [jnp.concatenate(col, axis=0) for col in zip(*pieces)]


def _layer_spec(a, l, resident=False):
    index = lambda *grid: (l,) + (0,) * (a.ndim - 1)
    if resident:
        return pl.BlockSpec((None,) + a.shape[1:], index, pipeline_mode=pl.Buffered(1))
    return pl.BlockSpec((None,) + a.shape[1:], index)


def _state_spec(a, l, nb):
    return pl.BlockSpec((None, nb) + a.shape[2:], lambda i, c: (l, i) + (0,) * (a.ndim - 2))


def _proj_stages(x_ref, g_ref, w_ref, wg_ref, pa_s, pb_s, pg_s, *, sub, chunk):
    nb = x_ref.shape[0]
    L = chunk
    x = jnp.concatenate([x_ref[b, sub * L:(sub + 1) * L, :] for b in range(nb)], axis=0)
    h = _rms(x, g_ref[...]).astype(BF16)
    yield
    for dst, base, width in ((pa_s, 0, A_COLS), (pb_s, A_COLS, B_MAIN)):
        for lo in range(0, width, PROJ_COLS):
            hi = min(lo + PROJ_COLS, width)
            part = _dot(h, w_ref[:, base + lo:base + hi])
            for b in range(nb):
                dst[sub, b, :, lo:hi] = part[b * L:(b + 1) * L]
            yield
    part = _dot(h, wg_ref[...])
    for b in range(nb):
        pg_s[sub, b] = part[b * L:(b + 1) * L]


def _rwkv_heads_per_group(chunk):
    return min(A_HEADS, MXU_DIM // chunk)


def _mix_consts(chunk, streams):
    hg = _rwkv_heads_per_group(chunk)
    hl, w, rows = hg * chunk, hg * A_HEAD_DIM, streams * chunk
    ix = lambda n: (np.arange(n)[:, None], np.arange(n)[None, :])
    r, c = ix(MXU_DIM)
    seg = r // A_HEAD_DIM == c // A_HEAD_DIM
    r, c = ix(rows)
    tril = (r // chunk == c // chunk) & (c <= r)
    r, c = np.arange(2 * chunk)[:, None], np.arange(hl)[None, :]
    amask = np.where(r < chunk, c % chunk < r, c % chunk <= r - chunk)
    r, c = np.arange(chunk)[:, None], np.arange(hl)[None, :]
    eye = c % chunk == r
    r, c = np.arange(hl)[:, None], np.arange(w)[None, :]
    bdl = r // chunk == c // A_HEAD_DIM
    r, c = ix(hl)
    bdsq = r // chunk == c // chunk
    r, c = ix(w)
    smask = r // A_HEAD_DIM == c // A_HEAD_DIM
    return (jnp.asarray(seg, BF16), jnp.asarray(tril, BF16), jnp.asarray(amask, F32), jnp.asarray(eye, F32),
            jnp.asarray(bdl, BF16), jnp.asarray(bdsq, BF16), jnp.asarray(smask, F32))


def _rwkv_stages(p_ref, shift_ref, s0_ref, wts, consts, y_ref, shift_out_ref, s_out_ref, pbuf, bds,
                 *, chunk, hg, row0, first, last):
    mu_ref, w0_ref, w2_ref, a0_ref, a2_ref, g2_ref, kk_ref, ka_ref, rk_ref, lnw_ref, lnb_ref = wts
    seg_ref, tril_ref, amask_ref, eye_ref, bdl_ref, bdsq_ref, smask_ref = consts
    L = chunk
    nb = p_ref.shape[0]
    rows = nb * L
    hl = hg * L
    wid = hg * A_HEAD_DIM
    groups = A_HEADS // hg
    diag = lambda h: (h // hg, slice((h % hg) * A_HEAD_DIM, (h % hg + 1) * A_HEAD_DIM))
    win = slice(row0, row0 + L)

    if first is not None:
        @pl.when(first)
        def _():
            bds[...] = jnp.zeros(bds.shape, F32)
            for b in range(nb):
                pbuf[b] = jnp.zeros(pbuf.shape[1:], F32)
                pbuf[b, CARRY - 1:CARRY, :] = shift_ref[b]
                for h in range(A_HEADS):
                    g, sl = diag(h)
                    bds[b, g, sl, sl] = s0_ref[b, h]

    ps, prevs = [], []
    for b in range(nb):
        pb = p_ref[b]
        prevs.append(_shift_rows(pb, pbuf[b], 1))
        pbuf[b] = pb[L - CARRY:L, :]
        shift_out_ref[b] = pb[L - 1:L, :]
        ps.append(pb)
    i1, i2, i3 = A_WIDTH, 2 * A_WIDTH, 3 * A_WIDTH
    i4, i5 = i3 + A_DECAY_LORA, i3 + A_DECAY_LORA + A_AAA_LORA
    mu = mu_ref[...]

    def token_shift(p, prev):
        pm = p + (prev - p) * mu
        return (pm[:, :i1], pm[:, i1:i2], pm[:, i2:i3], jnp.tanh(pm[:, i3:i4]).astype(BF16),
                pm[:, i4:i5].astype(BF16), _sigmoid(pm[:, i5:]).astype(BF16))

    r, k, v, wl16, al16, gl16 = _rowwise(token_shift, rows, jnp.concatenate(ps, axis=0), jnp.concatenate(prevs, axis=0))
    yield INPUTS_READ
    seg = seg_ref[...]
    half = A_WIDTH // 2
    segsum = lambda t16: jnp.concatenate([_dot(t16[:, :half], seg), _dot(t16[:, half:], seg)], axis=1)
    z = w0_ref[...] + _dot(wl16, w2_ref[...])
    a_pre = a0_ref[...] + _dot(al16, a2_ref[...])
    gate = _dot(gl16, g2_ref[...])
    yield
    k_k, k_a, r_k = kk_ref[...], ka_ref[...], rk_ref[...]

    def decay_and_keys(k, z, a_pre):
        lw = -math.exp(-0.5) * _sigmoid(z)
        lw_hi = lw.astype(BF16)
        a = _sigmoid(a_pre)
        kk0 = k * k_k
        return (lw, lw_hi, (lw - lw_hi.astype(F32)).astype(BF16), a, kk0, (kk0 * kk0).astype(BF16),
                k * (1.0 + (a - 1.0) * k_a))

    lw, lw_hi, lw_lo, a, kk0, kk_sq16, kmod = _rowwise(decay_and_keys, rows, k, z, a_pre)
    yield
    tril = tril_ref[...]
    cum = _dot(tril, lw_hi) + _dot(tril, lw_lo)
    kk_ss = segsum(kk_sq16)
    yield

    rs = lambda b: slice(b * L, (b + 1) * L)
    cum_last = [cum[(b + 1) * L - 1:(b + 1) * L] for b in range(nb)]
    g_last = [jnp.exp(cum_last[b]) for b in range(nb)]

    def scaled_operands(b):
        def fn(r, v, kk0, kk_ss, a, kmod, lw, cum):
            kk = kk0 * lax.rsqrt(kk_ss + 1e-12)
            bvec = kk * a
            g_inv = jnp.exp(-cum)
            g_tail = jnp.exp(cum_last[b] - cum)
            return ((kk * jnp.exp(cum - lw)).astype(BF16), (r * jnp.exp(cum)).astype(BF16),
                    (kmod * g_inv).astype(BF16), (bvec * g_inv).astype(BF16), v.astype(BF16),
                    (kmod * g_tail).astype(BF16), (bvec * g_tail).astype(BF16), (r * kmod * r_k).astype(BF16))
        return fn

    per_stream = [_rowwise(scaled_operands(b), L, *[t[rs(b)] for t in (r, v, kk0, kk_ss, a, kmod, lw, cum)])
                  for b in range(nb)]
    kkg_s, rg_s, kd_s, bd_s, v_s, kt, bt, bonus_s = zip(*per_stream)
    bonus_ss = segsum(jnp.concatenate(bonus_s, axis=0))
    yield

    keep = amask_ref[...] > 0.5
    eye = eye_ref[...]
    bdl = bdl_ref[...]
    bdsq = bdsq_ref[...]
    smask = smask_ref[...]
    lane_bd = lambda t: jnp.concatenate([t] * hg, axis=0) * bdl
    sq_bd = lambda t: jnp.concatenate([t] * hg, axis=0) * bdsq

    chains = [(b, g) for b in range(nb) for g in range(groups)]
    ls = lambda g: slice(g * wid, (g + 1) * wid)
    each = lambda f: [f(i, b, g) for i, (b, g) in enumerate(chains)]

    kkg = each(lambda i, b, g: kkg_s[b][:, ls(g)])
    rg = each(lambda i, b, g: rg_s[b][:, ls(g)])
    v16 = each(lambda i, b, g: v_s[b][:, ls(g)])
    lhs = each(lambda i, b, g: jnp.concatenate([kkg[i], rg[i]], axis=0))
    a_k = each(lambda i, b, g: jnp.where(keep, _dot_nt(lhs[i], lane_bd(kd_s[b][:, ls(g)])), 0.0))
    yield
    a_b = each(lambda i, b, g: jnp.where(keep, _dot_nt(lhs[i], lane_bd(bd_s[b][:, ls(g)])), 0.0))
    a_kk = each(lambda i, b, g: a_k[i][:L].astype(BF16))
    a_rk = each(lambda i, b, g: a_k[i][L:].astype(BF16))
    a_rb = each(lambda i, b, g: a_b[i][L:].astype(BF16))
    yield

    npow = each(lambda i, b, g: -a_b[i][:L])
    tinv = each(lambda i, b, g: eye + npow[i])
    n16 = each(lambda i, b, g: npow[i].astype(BF16))
    npow = each(lambda i, b, g: _dot(n16[i], sq_bd(n16[i])))
    yield
    span = 2
    while span < L:
        n16 = each(lambda i, b, g: npow[i].astype(BF16))
        t_bd = each(lambda i, b, g: sq_bd(tinv[i].astype(BF16)))
        if 2 * span < L:
            both = each(lambda i, b, g: _dot(n16[i], jnp.concatenate([sq_bd(n16[i]), t_bd[i]], axis=1)))
            npow = each(lambda i, b, g: both[i][:, :hl])
            prod = each(lambda i, b, g: both[i][:, hl:])
        else:
            prod = each(lambda i, b, g: _dot(n16[i], t_bd[i]))
        tinv = each(lambda i, b, g: tinv[i] + prod[i])
        span *= 2
        yield

    yield READS_STATE
    s16 = each(lambda i, b, g: bds[b, g].astype(BF16))
    v_bd = each(lambda i, b, g: lane_bd(v16[i]))
    rhs = each(lambda i, b, g: (_dot_nt(kkg[i], s16[i]) + _dot(a_kk[i], v_bd[i])).astype(BF16))
    yield
    u16 = each(lambda i, b, g: _dot(tinv[i].astype(BF16), lane_bd(rhs[i])).astype(BF16))
    yield
    yc_ = each(lambda i, b, g: _dot_nt(rg[i], s16[i]) + _dot(a_rk[i], v_bd[i]) - _dot(a_rb[i], lane_bd(u16[i])))
    yield
    upd = each(lambda i, b, g: _dot_tn(jnp.concatenate([v16[i], -u16[i]], axis=0),
                                       jnp.concatenate([kt[b][:, ls(g)], bt[b][:, ls(g)]], axis=0)))
    for i, (b, g) in enumerate(chains):
        bds[b, g] = bds[b, g] * g_last[b][:, ls(g)] + upd[i] * smask
    yield WROTE_STATE
    ys = [jnp.concatenate(yc_[b * groups:(b + 1) * groups], axis=1) for b in range(nb)]
    y = jnp.concatenate(ys, axis=0)

    inv_d = 1.0 / A_HEAD_DIM
    y_sum = segsum(y.astype(BF16))

    def centre(y, y_sum):
        yc = y - y_sum * inv_d
        return yc, (yc * yc).astype(BF16)

    yc, yc_sq16 = _rowwise(centre, rows, y, y_sum)
    var_sum = segsum(yc_sq16)
    yield
    ln_w, ln_b = lnw_ref[...], lnb_ref[...]

    def finish(yc, var_sum, bonus_ss, v, gate):
        return ((yc * lax.rsqrt(var_sum * inv_d + GN_EPS) * ln_w + ln_b + bonus_ss * v) * gate,)

    out, = _rowwise(finish, rows, yc, var_sum, bonus_ss, v, gate)
    for b in range(nb):
        y_ref[b, win, 0:A_WIDTH] = out[b * L:(b + 1) * L]

    if last is not None:
        @pl.when(last)
        def _():
            for b in range(nb):
                for h in range(A_HEADS):
                    g, sl = diag(h)
                    s_out_ref[b, h] = bds[b, g, sl, sl]


def _mlstm_stages(pb_ref, pg_ref, conv0_ref, c0_ref, n0_ref, m0_ref, wts, tril_ref,
                  y_ref, conv_out_ref, c_ref, n_ref, m_ref, xbuf, *, chunk, row0, first):
    cw_ref, cb_ref, gb_ref, hnw_ref = wts
    win = slice(row0, row0 + chunk)
    L = chunk
    nb = pb_ref.shape[0]
    rows = nb * L
    hist = B_CONV - 1

    if first is not None:
        @pl.when(first)
        def _():
            for b in range(nb):
                xbuf[b] = jnp.zeros(xbuf.shape[1:], F32)
                xbuf[b, CARRY - hist:CARRY, :] = conv0_ref[b]
            c_ref[...] = c0_ref[...]
            n_ref[...] = n0_ref[...]
            m_ref[...] = m0_ref[...]

    qk = []
    for b in range(nb):
        x = pb_ref[b, :, 0:2 * B_WIDTH]
        prev = xbuf[b]
        acc = cb_ref[...] + cw_ref[hist:hist + 1, :] * x
        for j in range(hist):
            acc = acc + cw_ref[j:j + 1, :] * _shift_rows(x, prev, hist - j)
        xbuf[b] = x[L - CARRY:L, :]
        conv_out_ref[b] = x[L - hist:L, :]
        qk.append(acc * _sigmoid(acc))
        yield

    gates = jnp.concatenate([pg_ref[b] for b in range(nb)], axis=0) + gb_ref[...]
    lane = lax.broadcasted_iota(jnp.int32, (rows, GATE_PAD), 1)
    tril = tril_ref[...]
    cum = sum(_dot(tril, part) for part in _split3(-_softplus(-gates)))
    gcols = jnp.where(lane < B_HEADS, gates, cum)
    grows = gcols.T
    causal = _tril(L, False)
    yield

    chains = [(b, h) for b in range(nb) for h in range(B_HEADS)]
    each = lambda f: [f(i, b, h) for i, (b, h) in enumerate(chains)]
    rs = lambda b: slice(b * L, (b + 1) * L)
    hs = lambda h, base=0: slice(base + h * B_HEAD_DIM, base + (h + 1) * B_HEAD_DIM)
    v16 = each(lambda i, b, h: pb_ref[b, :, hs(h, 2 * B_WIDTH)].astype(BF16))
    o_gate = each(lambda i, b, h: _sigmoid(pb_ref[b, :, hs(h, 3 * B_WIDTH)]))
    yield INPUTS_READ

    q = each(lambda i, b, h: qk[b][:, hs(h)])
    k = each(lambda i, b, h: qk[b][:, hs(h, B_WIDTH)] * (B_HEAD_DIM ** -0.5))
    q16 = each(lambda i, b, h: q[i].astype(BF16))
    li_col = each(lambda i, b, h: gcols[rs(b), h:h + 1])
    b_col = each(lambda i, b, h: gcols[rs(b), B_HEADS + h:B_HEADS + h + 1])
    li_row = each(lambda i, b, h: grows[h:h + 1, rs(b)])
    b_row = each(lambda i, b, h: grows[B_HEADS + h:B_HEADS + h + 1, rs(b)])
    yield READS_STATE
    c_prev = each(lambda i, b, h: c_ref[b, h])
    n_prev = each(lambda i, b, h: n_ref[b, h:h + 1, :])
    m_prev = each(lambda i, b, h: m_ref[b, h:h + 1, 0:1])

    dm = each(lambda i, b, h: jnp.where(causal, b_col[i] - b_row[i] + li_row[i], -jnp.inf))
    inter = each(lambda i, b, h: b_col[i] + m_prev[i])
    mt = each(lambda i, b, h: jnp.maximum(inter[i], jnp.max(dm[i], axis=-1, keepdims=True)))
    wo = each(lambda i, b, h: jnp.exp(inter[i] - mt[i]))
    yield
    s = each(lambda i, b, h: _dot_nt(q16[i], k[i].astype(BF16)) * jnp.exp(dm[i] - mt[i]))
    yield
    num = each(lambda i, b, h: wo[i] * _dot(q16[i], c_prev[i].astype(BF16)) + _dot(s[i].astype(BF16), v16[i]))
    yield
    den = each(lambda i, b, h: wo[i] * jnp.sum(q[i] * n_prev[i], axis=-1, keepdims=True)
               + jnp.sum(s[i], axis=-1, keepdims=True))
    hh = each(lambda i, b, h: num[i] * (1.0 / jnp.maximum(jnp.abs(den[i]), jnp.exp(-mt[i]))))
    yield

    m_new = each(lambda i, b, h: mt[i][L - 1:L, :])
    b_last = each(lambda i, b, h: b_col[i][L - 1:L, :])
    kw = each(lambda i, b, h: k[i] * jnp.exp(b_last[i] - b_col[i] + li_col[i] - m_new[i]))
    dec = each(lambda i, b, h: jnp.exp(b_last[i] + m_prev[i] - m_new[i]))
    yield
    c_new = each(lambda i, b, h: dec[i] * c_prev[i] + _dot_tn(kw[i].astype(BF16), v16[i]))
    n_new = each(lambda i, b, h: dec[i] * n_prev[i] + jnp.sum(kw[i], axis=0, keepdims=True))
    yield
    out = each(lambda i, b, h: hh[i] * lax.rsqrt(jnp.mean(hh[i] * hh[i], axis=-1, keepdims=True) + RMS_EPS)
               * hnw_ref[:, hs(h)] * o_gate[i])
    for i, (b, h) in enumerate(chains):
        c_ref[b, h] = c_new[i]
        n_ref[b, h:h + 1, :] = n_new[i]
        m_ref[b, h:h + 1, :] = jnp.broadcast_to(m_new[i], (1, B_HEAD_DIM))
        y_ref[b, win, hs(h, A_WIDTH)] = out[i]


INPUTS_READ, READS_STATE, WROTE_STATE, DONE = "inputs_read", "reads_state", "wrote_state", "done"
N_RWKV_W = 11
N_MLSTM_W = 4
N_CONSTS = 7
RWKV, MLSTM, PROJ = 0, 1, 2


def _mix_kernel(*refs, chunk, hg, n_steps):
    it = iter(refs)
    take = lambda n: [next(it) for _ in range(n)]
    x_ref, shift_ref, s0_ref, conv0_ref, c0_ref, n0_ref, m0_ref, norm_ref, w_ref, wg_ref = take(10)
    rwkv_w, mlstm_w, consts = take(N_RWKV_W), take(N_MLSTM_W), take(N_CONSTS)
    y_ref, shift_out_ref, s_out_ref, conv_out_ref, c_ref, n_ref, m_ref = take(7)
    pa_s, pb_s, pg_s, pbuf, bds, xbuf = take(6)
    step = pl.program_id(1)
    subs = x_ref.shape[1] // chunk
    project = lambda sub: _proj_stages(x_ref, norm_ref, w_ref, wg_ref, pa_s, pb_s, pg_s, sub=sub, chunk=chunk)

    @pl.when(step == 0)
    def _():
        for sub in range(subs):
            for _ in project(sub):
                pass

    @pl.when(step > 0)
    def _():
        first = step == 1
        last = step == n_steps
        bodies = {}
        for sub in range(subs):
            is_first = first if sub == 0 else None
            bodies[sub, RWKV] = _rwkv_stages(
                pa_s.at[sub], shift_ref, s0_ref, rwkv_w, consts, y_ref, shift_out_ref, s_out_ref, pbuf, bds,
                chunk=chunk, hg=hg, row0=sub * chunk, first=is_first, last=last if sub == subs - 1 else None)
            bodies[sub, MLSTM] = _mlstm_stages(
                pb_s.at[sub], pg_s.at[sub], conv0_ref, c0_ref, n0_ref, m0_ref, mlstm_w, consts[1],
                y_ref, conv_out_ref, c_ref, n_ref, m_ref, xbuf, chunk=chunk, row0=sub * chunk, first=is_first)
            if n_steps > 1:
                bodies[sub, PROJ] = project(sub)
        live = sorted(bodies)
        wrote, waiting, inputs_read = set(), set(), set()
        while live:
            for key in list(live):
                sub, kind = key
                if kind == PROJ and not {(sub, RWKV), (sub, MLSTM)} <= inputs_read:
                    continue
                if key in waiting and (sub - 1, kind) not in wrote:
                    continue
                waiting.discard(key)
                mark = next(bodies[key], DONE)
                if mark == INPUTS_READ:
                    inputs_read.add(key)
                elif mark == READS_STATE and sub > 0:
                    waiting.add(key)
                elif mark in (WROTE_STATE, DONE):
                    wrote.add(key)
                    if mark == DONE:
                        live.remove(key)


def _mix(x, l, sl, states, proj_w, rwkv_w, mlstm_w, chunk, subs):
    bn, t, _ = x.shape
    nb = MIX_STREAMS
    hg = _rwkv_heads_per_group(chunk)
    hist = B_CONV - 1
    rows = subs * chunk
    n_steps = t // rows
    consts = _mix_consts(chunk, nb)
    full = lambda a: pl.BlockSpec(a.shape, lambda i, c: (0,) * a.ndim)
    per_stream = lambda *dims: pl.BlockSpec((nb,) + dims, lambda i, c: (i,) + (0,) * len(dims))
    state_shapes = [(1, A_COLS), (A_HEADS, A_HEAD_DIM, A_HEAD_DIM), (hist, 2 * B_WIDTH),
                    (B_HEADS, B_HEAD_DIM, B_HEAD_DIM), (B_HEADS, B_HEAD_DIM), (B_HEADS, B_HEAD_DIM)]
    norm, w_in, w_gate = proj_w
    return pl.pallas_call(
        functools.partial(_mix_kernel, chunk=chunk, hg=hg, n_steps=n_steps),
        grid=(bn // nb, n_steps + 1),
        in_specs=[pl.BlockSpec((nb, rows, D_MODEL), lambda i, c: (i, jnp.minimum(c, n_steps - 1), 0))]
        + [_state_spec(a, sl, nb) for a in states]
        + [_layer_spec(norm, l), _layer_spec(w_in, l, resident=True), _layer_spec(w_gate, l, resident=True)]
        + [_layer_spec(a, l) for a in rwkv_w + mlstm_w] + [full(a) for a in consts],
        out_specs=[pl.BlockSpec((nb, rows, D_MIX), lambda i, c: (i, jnp.maximum(c - 1, 0), 0))]
        + [per_stream(*dims) for dims in state_shapes],
        out_shape=[jax.ShapeDtypeStruct((bn, t, D_MIX), F32)]
        + [jax.ShapeDtypeStruct((bn,) + dims, F32) for dims in state_shapes],
        scratch_shapes=[
            pltpu.VMEM((subs, nb, chunk, A_COLS), F32),
            pltpu.VMEM((subs, nb, chunk, B_MAIN), F32),
            pltpu.VMEM((subs, nb, chunk, GATE_PAD), F32),
            pltpu.VMEM((nb, CARRY, A_COLS), F32),
            pltpu.VMEM((nb, A_HEADS // hg, hg * A_HEAD_DIM, hg * A_HEAD_DIM), F32),
            pltpu.VMEM((nb, CARRY, 2 * B_WIDTH), F32),
        ],
        compiler_params=pltpu.CompilerParams(
            dimension_semantics=("arbitrary", "arbitrary"), vmem_limit_bytes=VMEM_LIMIT),
        name="mixers",
    )(x, *states, norm, w_in, w_gate, *rwkv_w, *mlstm_w, *consts)


def _ffn_kernel(x_ref, y_ref, f0_ref, wout_ref, g_ref, wup_ref, cw_ref, cb_ref,
                wdown_ref, gfin_ref, o_ref, fout_ref, ubuf, *, tile, final):
    nb = x_ref.shape[0]
    hist = FFN_CONV - 1
    rows_of = lambda ref: ref[0] if nb == 1 else jnp.concatenate([ref[b] for b in range(nb)], axis=0)

    @pl.when(pl.program_id(1) == 0)
    def _():
        for b in range(nb):
            ubuf[b] = jnp.zeros(ubuf.shape[1:], F32)
            ubuf[b, CARRY - hist:CARRY, :] = f0_ref[b]

    x1 = rows_of(x_ref) + _dot(rows_of(y_ref).astype(BF16), wout_ref[...])
    h2 = _rms(x1, g_ref[...]).astype(BF16)
    acts = []
    for lo, hi in zip(FF_BOUNDS[:-1], FF_BOUNDS[1:]):
        cols = slice(lo, hi)
        u = _dot(h2, wup_ref[:, cols])
        gate = _dot(h2, wup_ref[:, D_FF + lo:D_FF + hi])
        convs = []
        for b in range(nb):
            ub = u[b * tile:(b + 1) * tile]
            prev = ubuf[b, :, cols]
            acc = cb_ref[:, cols] + cw_ref[hist:hist + 1, cols] * ub
            for j in range(hist):
                acc = acc + cw_ref[j:j + 1, cols] * _shift_rows(ub, prev, hist - j)
            ubuf[b, :, cols] = ub[tile - CARRY:tile]
            fout_ref[b, :, cols] = ub[tile - hist:tile]
            convs.append(acc)
        conv = convs[0] if nb == 1 else jnp.concatenate(convs, axis=0)
        acts.append((conv * _sigmoid(conv) * gate).astype(BF16))
    x2 = x1 + _dot(jnp.concatenate(acts, axis=1), wdown_ref[...])
    out = _rms(x2, gfin_ref[...]) if final else x2
    for b in range(nb):
        o_ref[b] = out[b * tile:(b + 1) * tile]


def _ffn(x, y, l, sl, f0, wts, gfin, nb, tile, final):
    bn, t, _ = x.shape
    hist = FFN_CONV - 1
    tok = pl.BlockSpec((nb, tile, D_MODEL), lambda i, c: (i, c, 0))
    fspec = pl.BlockSpec((nb, hist, D_FF), lambda i, c: (i, 0, 0))
    return pl.pallas_call(
        functools.partial(_ffn_kernel, tile=tile, final=final),
        grid=(bn // nb, t // tile),
        in_specs=[tok, tok, _state_spec(f0, sl, nb)]
        + [_layer_spec(a, l, resident=True) for a in wts] + [_layer_spec(gfin, 0, resident=True)],
        out_specs=[tok, fspec],
        out_shape=[
            jax.ShapeDtypeStruct((bn, t, D_MODEL), F32),
            jax.ShapeDtypeStruct((bn, hist, D_FF), F32),
        ],
        scratch_shapes=[pltpu.VMEM((nb, CARRY, D_FF), F32)],
        compiler_params=pltpu.CompilerParams(
            dimension_semantics=("arbitrary", "arbitrary"), vmem_limit_bytes=VMEM_LIMIT),
        name="out_ffn",
    )(x, y, f0, *wts, gfin)


def _layer(x, l, sl, st, w, *, mix_chunk, mix_subs, ffn_streams, ffn_tile, final):
    y, *mix_states = _mix(x, l, sl, st[:6], (w["norm_mix"], w["w_in"], w["w_gate"]), w["rwkv"], w["mlstm"],
                          mix_chunk, mix_subs)
    x, fconv1 = _ffn(x, y, l, sl, st[6], w["ffn"], w["norm_final"], ffn_streams, ffn_tile, final)
    return x, tuple(mix_states) + (fconv1,)


def kernel(x_prompt, x_sample, state_rwkv_shift, state_rwkv_wkv, state_mlstm_conv, state_mlstm_C,
           state_mlstm_n, state_mlstm_m, state_ffn_conv, meta_tokens, norm_mix, w_in, a_mu, a_w0, a_w2,
           a_a0, a_a2, a_g2, a_k_k, a_k_a, a_r_k, a_ln_w, a_ln_b, b_conv_w, b_conv_b, b_i_bias, b_f_bias,
           b_hn_w, w_out, norm_ffn, w_up, ffn_conv_w, ffn_conv_b, w_down, norm_final):
    n_prompt = x_prompt.shape[0]
    n_sample = x_sample.shape[0]
    n_lead = n_prompt + n_sample
    assert x_sample.shape[1] == N_META
    assert n_prompt == MIX_STREAMS and n_lead % MIX_STREAMS == 0

    bf = lambda a: a.astype(BF16)
    vec = lambda a: a[:, None, :]
    n_gate = 2 * B_HEADS
    gate_pad = ((0, 0), (0, GATE_PAD - n_gate))
    w = {
        "norm_mix": vec(norm_mix),
        "w_in": bf(w_in),
        "w_gate": bf(jnp.pad(w_in[:, :, A_COLS + B_MAIN:], ((0, 0),) + gate_pad)),
        "rwkv": [vec(a_mu), vec(a_w0), bf(a_w2), vec(a_a0), bf(a_a2), bf(a_g2), vec(a_k_k), vec(a_k_a),
                 vec(a_r_k), vec(a_ln_w), vec(a_ln_b)],
        "mlstm": [b_conv_w, vec(b_conv_b), vec(jnp.pad(jnp.concatenate([b_i_bias, b_f_bias], axis=1), gate_pad)),
                  vec(b_hn_w)],
        "ffn": (bf(w_out), vec(norm_ffn), bf(w_up), ffn_conv_w, vec(ffn_conv_b), bf(w_down)),
        "norm_final": norm_final.reshape(1, 1, D_MODEL),
    }
    assert len(w["rwkv"]) == N_RWKV_W and len(w["mlstm"]) == N_MLSTM_W

    lead_pad = lambda s: jnp.pad(s, ((0, 0), (n_prompt, 0)) + ((0, 0),) * (s.ndim - 2))
    st_lead_in = (
        lead_pad(state_rwkv_shift)[:, :, None, :],
        lead_pad(state_rwkv_wkv),
        lead_pad(state_mlstm_conv),
        lead_pad(state_mlstm_C),
        lead_pad(state_mlstm_n),
        jnp.broadcast_to(lead_pad(state_mlstm_m)[..., None], (DEPTH, n_lead, B_HEADS, B_HEAD_DIM)),
        lead_pad(state_ffn_conv),
    )
    meta = jnp.broadcast_to(meta_tokens[None].astype(x_prompt.dtype), (n_prompt, N_META, D_MODEL))
    x_lead = jnp.concatenate([meta, x_sample], axis=0)
    x_main = x_prompt

    main_states, lead_states = [], []
    for l in range(DEPTH):
        final = l == DEPTH - 1
        x_lead, st_lead = _layer(x_lead, l, l, st_lead_in, w, mix_chunk=N_META,
                                 mix_subs=1, ffn_streams=n_lead, ffn_tile=N_META, final=final)
        x_main, st_main = _layer(x_main, l, 0, tuple(s[None] for s in st_lead), w,
                                 mix_chunk=MAIN_CHUNK, mix_subs=MAIN_SUBCHUNKS, ffn_streams=1, ffn_tile=MAIN_TILE, final=final)
        main_states.append(st_main)
        lead_states.append(st_lead)

    def collect(per_layer, first):
        shift, wkv, bconv, c, n, m, fconv = (jnp.stack([st[i] for st in per_layer])[:, first:] for i in range(7))
        return (shift[:, :, 0, :], wkv, bconv, c, n, m[..., 0], fconv)

    return (x_main, x_lead[n_prompt:]) + collect(main_states, 0) + collect(lead_states, n_prompt)
```

```python
import functools
import math

import numpy as np

import jax
import jax.numpy as jnp
from jax import lax
from jax.experimental import pallas as pl
from jax.experimental.pallas import tpu as pltpu

D_MODEL = 1024
DEPTH = 2
N_META = 16
A_HEADS = 8
A_HEAD_DIM = 64
A_WIDTH = 512
A_DECAY_LORA = 64
A_AAA_LORA = 64
A_GATE_LORA = 128
A_COLS = 1792
B_HEADS = 4
B_HEAD_DIM = 128
B_WIDTH = 512
B_CONV = 4
B_MAIN = 4 * B_WIDTH
GATE_PAD = 128
D_MIX = A_WIDTH + B_WIDTH
D_FF = 2816
FFN_CONV = 3
RMS_EPS = 1e-6
GN_EPS = 64e-5
CARRY = 8
ROW_BLOCK = 16
LANES = 128
MXU_DIM = 256
MIX_STREAMS = 4
FF_BOUNDS = (0, 6 * MXU_DIM, D_FF)
MAIN_TILE = 512
MAIN_CHUNK = 64
MAIN_SUBCHUNKS = 2
VMEM_LIMIT = 56 * 1024 * 1024

F32 = jnp.float32
BF16 = jnp.bfloat16
NT_DIMS = (((1,), (1,)), ((), ()))
TN_DIMS = (((0,), (0,)), ((), ()))


def _dot(a, b):
    return jnp.dot(a, b, preferred_element_type=F32)


def _dot_nt(a, b):
    return lax.dot_general(a, b, NT_DIMS, preferred_element_type=F32)


def _dot_tn(a, b):
    return lax.dot_general(a, b, TN_DIMS, preferred_element_type=F32)


def _sigmoid(x):
    return 0.5 * jnp.tanh(0.5 * x) + 0.5


def _silu(x):
    half = 0.5 * x
    return half + half * jnp.tanh(half)


def _softplus(x):
    return jnp.maximum(x, 0.0) + jnp.log(1.0 + jnp.exp(-jnp.abs(x)))


def _rms(x, g):
    return x * lax.rsqrt(jnp.mean(x * x, axis=-1, keepdims=True) + RMS_EPS) * g


def _tril(n, strict):
    row = lax.broadcasted_iota(jnp.int32, (n, n), 0)
    col = lax.broadcasted_iota(jnp.int32, (n, n), 1)
    return (col < row) if strict else (col <= row)


def _split3(x):
    hi = x.astype(BF16)
    rest = x - hi.astype(F32)
    mid = rest.astype(BF16)
    lo = (rest - mid.astype(F32)).astype(BF16)
    return hi, mid, lo


def _shift_rows(x, prev, s):
    rolled = pltpu.roll(x, s, 0)
    from_prev = lax.broadcasted_iota(jnp.int32, prev.shape, 0) < s
    head = jnp.where(from_prev, pltpu.roll(prev, s, 0), rolled[0:CARRY])
    return jnp.concatenate([head, rolled[CARRY:]], axis=0)


def _rowwise(fn, n_rows, *xs):
    pieces = [fn(*[x[i:i + ROW_BLOCK] for x in xs]) for i in range(0, n_rows, ROW_BLOCK)]
    return [jnp.concatenate(col, axis=0) for col in zip(*pieces)]


def _layer_spec(a, l, resident=False):
    index = lambda *grid: (l,) + (0,) * (a.ndim - 1)
    if resident:
        return pl.BlockSpec((None,) + a.shape[1:], index, pipeline_mode=pl.Buffered(1))
    return pl.BlockSpec((None,) + a.shape[1:], index)


def _state_spec(a, l, nb, skip_first=False):
    block = (lambda i: jnp.maximum(i - 1, 0)) if skip_first else (lambda i: i)
    return pl.BlockSpec((None, nb) + a.shape[2:], lambda i, c: (l, block(i)) + (0,) * (a.ndim - 2))


def _unless(zero_state, load):
    if zero_state is None:
        load()
    else:
        pl.when(jnp.logical_not(zero_state))(load)


def _in_proj_kernel(x_ref, g_ref, w_ref, wg_ref, oa_ref, ob_ref, og_ref):
    h = _rms(x_ref[...], g_ref[...]).astype(BF16)
    oa_ref[...] = _dot(h, w_ref[:, 0:A_COLS])
    ob_ref[...] = _dot(h, w_ref[:, A_COLS:A_COLS + B_MAIN])
    og_ref[...] = _dot(h, wg_ref[...])


def _in_proj(x2d, l, g, w, wg, tm):
    m = x2d.shape[0]
    return pl.pallas_call(
        _in_proj_kernel,
        grid=(m // tm,),
        in_specs=[
            pl.BlockSpec((tm, D_MODEL), lambda i: (i, 0)),
            _layer_spec(g, l), _layer_spec(w, l, resident=True), _layer_spec(wg, l, resident=True),
        ],
        out_specs=[
            pl.BlockSpec((tm, A_COLS), lambda i: (i, 0)),
            pl.BlockSpec((tm, B_MAIN), lambda i: (i, 0)),
            pl.BlockSpec((tm, GATE_PAD), lambda i: (i, 0)),
        ],
        out_shape=[
            jax.ShapeDtypeStruct((m, A_COLS), F32),
            jax.ShapeDtypeStruct((m, B_MAIN), F32),
            jax.ShapeDtypeStruct((m, GATE_PAD), F32),
        ],
        compiler_params=pltpu.CompilerParams(
            dimension_semantics=("arbitrary",), vmem_limit_bytes=VMEM_LIMIT),
        name="in_proj",
    )(x2d, g, w, wg)


def _rwkv_heads_per_group(chunk):
    return min(A_HEADS, MXU_DIM // chunk)


def _mix_consts(chunk, streams):
    hg = _rwkv_heads_per_group(chunk)
    hl, w, rows = hg * chunk, hg * A_HEAD_DIM, streams * chunk
    ix = lambda n: (np.arange(n)[:, None], np.arange(n)[None, :])
    r, c = ix(MXU_DIM)
    seg = r // A_HEAD_DIM == c // A_HEAD_DIM
    r, c = ix(rows)
    tril = (r // chunk == c // chunk) & (c <= r)
    r, c = np.arange(2 * chunk)[:, None], np.arange(hl)[None, :]
    amask = np.where(r < chunk, c % chunk < r, c % chunk <= r - chunk)
    r, c = np.arange(chunk)[:, None], np.arange(hl)[None, :]
    eye = c % chunk == r
    r, c = np.arange(hl)[:, None], np.arange(w)[None, :]
    bdl = r // chunk == c // A_HEAD_DIM
    r, c = ix(hl)
    bdsq = r // chunk == c // chunk
    r, c = ix(w)
    smask = r // A_HEAD_DIM == c // A_HEAD_DIM

    def pack(dtype, **masks):
        width = max(m.shape[1] for m in masks.values())
        rows = np.cumsum([0] + [m.shape[0] for m in masks.values()])
        slab = np.concatenate([np.pad(m, ((0, 0), (0, width - m.shape[1]))) for m in masks.values()], axis=0)
        return jnp.asarray(slab, dtype), tuple((name, (int(r0),) + m.shape) for (name, m), r0 in zip(masks.items(), rows))

    slab16, where16 = pack(BF16, seg=seg, tril=tril, bdl=bdl, bdsq=bdsq)
    slab32, where32 = pack(F32, amask=amask, eye=eye, smask=smask)
    return (slab16, slab32), (where16, where32)


def _rwkv_stages(p_ref, shift_ref, s0_ref, wts, consts, y_ref, shift_out_ref, s_out_ref, pbuf, bds,
                 *, chunk, hg, row0, first, last, zero_state):
    mu_ref, w0_ref, w2_ref, a0_ref, a2_ref, g2_ref, kk_ref, ka_ref, rk_ref, lnw_ref, lnb_ref = wts
    L = chunk
    nb = p_ref.shape[0]
    rows = nb * L
    hl = hg * L
    wid = hg * A_HEAD_DIM
    groups = A_HEADS // hg
    diag = lambda h: (h // hg, slice((h % hg) * A_HEAD_DIM, (h % hg + 1) * A_HEAD_DIM))
    win = slice(row0, row0 + L)

    if first is not None:
        @pl.when(first)
        def _():
            bds[...] = jnp.zeros(bds.shape, F32)
            pbuf[...] = jnp.zeros(pbuf.shape, F32)

            def load():
                for b in range(nb):
                    pbuf[b, CARRY - 1:CARRY, :] = shift_ref[b]
                    for h in range(A_HEADS):
                        g, sl = diag(h)
                        bds[b, g, sl, sl] = s0_ref[b, h]
            _unless(zero_state, load)

    ps, prevs = [], []
    for b in range(nb):
        pb = p_ref[b, win, :]
        prevs.append(_shift_rows(pb, pbuf[b], 1))
        pbuf[b] = pb[L - CARRY:L, :]
        shift_out_ref[b] = pb[L - 1:L, :]
        ps.append(pb)
    i1, i2, i3 = A_WIDTH, 2 * A_WIDTH, 3 * A_WIDTH
    i4, i5 = i3 + A_DECAY_LORA, i3 + A_DECAY_LORA + A_AAA_LORA
    mu = mu_ref[...]

    def token_shift(p, prev):
        pm = p + (prev - p) * mu
        return (pm[:, :i1], pm[:, i1:i2], pm[:, i2:i3], jnp.tanh(pm[:, i3:i4]).astype(BF16),
                pm[:, i4:i5].astype(BF16), _sigmoid(pm[:, i5:]).astype(BF16))

    r, k, v, wl16, al16, gl16 = _rowwise(token_shift, rows, jnp.concatenate(ps, axis=0), jnp.concatenate(prevs, axis=0))
    yield
    seg = consts["seg"]()
    half = A_WIDTH // 2
    segsum = lambda t16: jnp.concatenate([_dot(t16[:, :half], seg), _dot(t16[:, half:], seg)], axis=1)
    z = w0_ref[...] + _dot(wl16, w2_ref[...])
    a_pre = a0_ref[...] + _dot(al16, a2_ref[...])
    gate = _dot(gl16, g2_ref[...])
    yield
    k_k, k_a, r_k = kk_ref[...], ka_ref[...], rk_ref[...]

    def decay_and_keys(k, z, a_pre):
        lw = -math.exp(-0.5) * _sigmoid(z)
        lw_hi = lw.astype(BF16)
        a = _sigmoid(a_pre)
        kk0 = k * k_k
        return (lw, lw_hi, (lw - lw_hi.astype(F32)).astype(BF16), a, kk0, (kk0 * kk0).astype(BF16),
                k * (1.0 + (a - 1.0) * k_a))

    lw, lw_hi, lw_lo, a, kk0, kk_sq16, kmod = _rowwise(decay_and_keys, rows, k, z, a_pre)
    yield
    tril = consts["tril"]()
    cum = _dot(tril, lw_hi) + _dot(tril, lw_lo)
    kk_ss = segsum(kk_sq16)
    yield

    rs = lambda b: slice(b * L, (b + 1) * L)
    cum_last = [cum[(b + 1) * L - 1:(b + 1) * L] for b in range(nb)]
    g_last = [jnp.exp(cum_last[b]) for b in range(nb)]

    def scaled_operands(b):
        def fn(r, v, kk0, kk_ss, a, kmod, lw, cum):
            kk = kk0 * lax.rsqrt(kk_ss + 1e-12)
            bvec = kk * a
            g_inv = jnp.exp(-cum)
            g_tail = jnp.exp(cum_last[b] - cum)
            return ((kk * jnp.exp(cum - lw)).astype(BF16), (r * jnp.exp(cum)).astype(BF16),
                    (kmod * g_inv).astype(BF16), (bvec * g_inv).astype(BF16), v.astype(BF16),
                    (kmod * g_tail).astype(BF16), (bvec * g_tail).astype(BF16), (r * kmod * r_k).astype(BF16))
        return fn

    per_stream = [_rowwise(scaled_operands(b), L, *[t[rs(b)] for t in (r, v, kk0, kk_ss, a, kmod, lw, cum)])
                  for b in range(nb)]
    kkg_s, rg_s, kd_s, bd_s, v_s, kt, bt, bonus_s = zip(*per_stream)
    bonus_ss = segsum(jnp.concatenate(bonus_s, axis=0))
    yield

    keep = consts["amask"]() > 0.5
    eye = consts["eye"]()
    bdl = consts["bdl"]()
    bdsq = consts["bdsq"]()
    smask = consts["smask"]()
    def block_diag(t, mask, seg):
        per_tile = max(1, LANES // seg)
        tile = per_tile * seg
        rows = []
        for h in range(hg):
            cols = [t[:, j:j + tile] * mask[h * L:(h + 1) * L, j:j + tile] if j // tile == h // per_tile
                    else jnp.zeros((L, tile), t.dtype) for j in range(0, t.shape[1], tile)]
            rows.append(cols[0] if len(cols) == 1 else jnp.concatenate(cols, axis=1))
        return jnp.concatenate(rows, axis=0)

    lane_bd = lambda t: block_diag(t, bdl, A_HEAD_DIM)
    sq_bd = lambda t: block_diag(t, bdsq, L)

    chains = [(b, g) for b in range(nb) for g in range(groups)]
    ls = lambda g: slice(g * wid, (g + 1) * wid)
    each = lambda f: [f(i, b, g) for i, (b, g) in enumerate(chains)]

    kkg = each(lambda i, b, g: kkg_s[b][:, ls(g)])
    rg = each(lambda i, b, g: rg_s[b][:, ls(g)])
    v16 = each(lambda i, b, g: v_s[b][:, ls(g)])
    lhs = each(lambda i, b, g: jnp.concatenate([kkg[i], rg[i]], axis=0))
    a_k = each(lambda i, b, g: jnp.where(keep, _dot_nt(lhs[i], lane_bd(kd_s[b][:, ls(g)])), 0.0))
    yield
    a_b = each(lambda i, b, g: jnp.where(keep, _dot_nt(lhs[i], lane_bd(bd_s[b][:, ls(g)])), 0.0))
    a_kk = each(lambda i, b, g: a_k[i][:L].astype(BF16))
    a_rk = each(lambda i, b, g: a_k[i][L:].astype(BF16))
    a_rb = each(lambda i, b, g: a_b[i][L:].astype(BF16))
    yield

    npow = each(lambda i, b, g: -a_b[i][:L])
    tinv = each(lambda i, b, g: eye + npow[i])
    n16 = each(lambda i, b, g: npow[i].astype(BF16))
    npow = each(lambda i, b, g: _dot(n16[i], sq_bd(n16[i])))
    yield
    span = 2
    while span < L:
        n16 = each(lambda i, b, g: npow[i].astype(BF16))
        t_bd = each(lambda i, b, g: sq_bd(tinv[i].astype(BF16)))
        if 2 * span < L:
            both = each(lambda i, b, g: _dot(n16[i], jnp.concatenate([sq_bd(n16[i]), t_bd[i]], axis=1)))
            npow = each(lambda i, b, g: both[i][:, :hl])
            prod = each(lambda i, b, g: both[i][:, hl:])
        else:
            prod = each(lambda i, b, g: _dot(n16[i], t_bd[i]))
        tinv = each(lambda i, b, g: tinv[i] + prod[i])
        span *= 2
        yield

    yield READS_STATE
    s16 = each(lambda i, b, g: bds[b, g].astype(BF16))
    v_bd = each(lambda i, b, g: lane_bd(v16[i]))
    rhs = each(lambda i, b, g: (_dot_nt(kkg[i], s16[i]) + _dot(a_kk[i], v_bd[i])).astype(BF16))
    yield
    u16 = each(lambda i, b, g: _dot(tinv[i].astype(BF16), lane_bd(rhs[i])).astype(BF16))
    yield
    yc_ = each(lambda i, b, g: _dot_nt(rg[i], s16[i]) + _dot(a_rk[i], v_bd[i]) - _dot(a_rb[i], lane_bd(u16[i])))
    yield
    upd = each(lambda i, b, g: _dot_tn(jnp.concatenate([v16[i], -u16[i]], axis=0),
                                       jnp.concatenate([kt[b][:, ls(g)], bt[b][:, ls(g)]], axis=0)))
    for i, (b, g) in enumerate(chains):
        bds[b, g] = bds[b, g] * g_last[b][:, ls(g)] + upd[i] * smask
    yield WROTE_STATE
    ys = [jnp.concatenate(yc_[b * groups:(b + 1) * groups], axis=1) for b in range(nb)]
    y = jnp.concatenate(ys, axis=0)

    inv_d = 1.0 / A_HEAD_DIM
    y_sum = segsum(y.astype(BF16))

    def centre(y, y_sum):
        yc = y - y_sum * inv_d
        return yc, (yc * yc).astype(BF16)

    yc, yc_sq16 = _rowwise(centre, rows, y, y_sum)
    var_sum = segsum(yc_sq16)
    yield
    ln_w, ln_b = lnw_ref[...], lnb_ref[...]

    def finish(yc, var_sum, bonus_ss, v, gate):
        return ((yc * lax.rsqrt(var_sum * inv_d + GN_EPS) * ln_w + ln_b + bonus_ss * v) * gate,)

    out, = _rowwise(finish, rows, yc, var_sum, bonus_ss, v, gate)
    for b in range(nb):
        y_ref[b, win, 0:A_WIDTH] = out[b * L:(b + 1) * L]

    if last is not None:
        @pl.when(last)
        def _():
            for b in range(nb):
                for h in range(A_HEADS):
                    g, sl = diag(h)
                    s_out_ref[b, h] = bds[b, g, sl, sl]


def _mlstm_stages(pb_ref, pg_ref, conv0_ref, c0_ref, n0_ref, m0_ref, wts, consts,
                  y_ref, conv_out_ref, c_ref, n_ref, m_ref, xbuf, *, chunk, row0, first, zero_state):
    cw_ref, cb_ref, gb_ref, hnw_ref = wts
    win = slice(row0, row0 + chunk)
    L = chunk
    nb = pb_ref.shape[0]
    rows = nb * L
    hist = B_CONV - 1

    if first is not None:
        @pl.when(first)
        def _():
            xbuf[...] = jnp.zeros(xbuf.shape, F32)
            c_ref[...] = jnp.zeros(c_ref.shape, F32)
            n_ref[...] = jnp.zeros(n_ref.shape, F32)
            m_ref[...] = jnp.zeros(m_ref.shape, F32)

            def load():
                for b in range(nb):
                    xbuf[b, CARRY - hist:CARRY, :] = conv0_ref[b]
                c_ref[...] = c0_ref[...]
                n_ref[...] = n0_ref[...]
                m_ref[...] = m0_ref[...]
            _unless(zero_state, load)

    qk = []
    for b in range(nb):
        x = pb_ref[b, win, 0:2 * B_WIDTH]
        prev = xbuf[b]
        acc = cb_ref[...] + cw_ref[hist:hist + 1, :] * x
        for j in range(hist):
            acc = acc + cw_ref[j:j + 1, :] * _shift_rows(x, prev, hist - j)
        xbuf[b] = x[L - CARRY:L, :]
        conv_out_ref[b] = x[L - hist:L, :]
        qk.append(_silu(acc))
        yield

    gates = jnp.concatenate([pg_ref[b, win, :] for b in range(nb)], axis=0) + gb_ref[...]
    lane = lax.broadcasted_iota(jnp.int32, (rows, GATE_PAD), 1)
    tril = consts["tril"]()
    cum = sum(_dot(tril, part) for part in _split3(-_softplus(-gates)))
    gcols = jnp.where(lane < B_HEADS, gates, cum)
    grows = gcols.T
    causal = _tril(L, False)
    yield

    chains = [(b, h) for b in range(nb) for h in range(B_HEADS)]
    each = lambda f: [f(i, b, h) for i, (b, h) in enumerate(chains)]
    rs = lambda b: slice(b * L, (b + 1) * L)
    hs = lambda h, base=0: slice(base + h * B_HEAD_DIM, base + (h + 1) * B_HEAD_DIM)

    q = each(lambda i, b, h: qk[b][:, hs(h)])
    k = each(lambda i, b, h: qk[b][:, hs(h, B_WIDTH)] * (B_HEAD_DIM ** -0.5))
    q16 = each(lambda i, b, h: q[i].astype(BF16))
    v16 = each(lambda i, b, h: pb_ref[b, win, hs(h, 2 * B_WIDTH)].astype(BF16))
    li_col = each(lambda i, b, h: gcols[rs(b), h:h + 1])
    b_col = each(lambda i, b, h: gcols[rs(b), B_HEADS + h:B_HEADS + h + 1])
    li_row = each(lambda i, b, h: grows[h:h + 1, rs(b)])
    b_row = each(lambda i, b, h: grows[B_HEADS + h:B_HEADS + h + 1, rs(b)])
    yield READS_STATE
    c_prev = each(lambda i, b, h: c_ref[b, h])
    n_prev = each(lambda i, b, h: n_ref[b, h:h + 1, :])
    m_prev = each(lambda i, b, h: m_ref[b, h:h + 1, 0:1])

    dm = each(lambda i, b, h: jnp.where(causal, b_col[i] - b_row[i] + li_row[i], -jnp.inf))
    inter = each(lambda i, b, h: b_col[i] + m_prev[i])
    mt = each(lambda i, b, h: jnp.maximum(inter[i], jnp.max(dm[i], axis=-1, keepdims=True)))
    wo = each(lambda i, b, h: jnp.exp(inter[i] - mt[i]))
    yield
    s = each(lambda i, b, h: _dot_nt(q16[i], k[i].astype(BF16)) * jnp.exp(dm[i] - mt[i]))
    yield
    num = each(lambda i, b, h: wo[i] * _dot(q16[i], c_prev[i].astype(BF16)) + _dot(s[i].astype(BF16), v16[i]))
    yield
    den = each(lambda i, b, h: wo[i] * jnp.sum(q[i] * n_prev[i], axis=-1, keepdims=True)
               + jnp.sum(s[i], axis=-1, keepdims=True))
    hh = each(lambda i, b, h: num[i] * (1.0 / jnp.maximum(jnp.abs(den[i]), jnp.exp(-mt[i]))))
    yield

    m_new = each(lambda i, b, h: mt[i][L - 1:L, :])
    b_last = each(lambda i, b, h: b_col[i][L - 1:L, :])
    kw = each(lambda i, b, h: k[i] * jnp.exp(b_last[i] - b_col[i] + li_col[i] - m_new[i]))
    dec = each(lambda i, b, h: jnp.exp(b_last[i] + m_prev[i] - m_new[i]))
    yield
    c_new = each(lambda i, b, h: dec[i] * c_prev[i] + _dot_tn(kw[i].astype(BF16), v16[i]))
    n_new = each(lambda i, b, h: dec[i] * n_prev[i] + jnp.sum(kw[i], axis=0, keepdims=True))
    yield
    out = each(lambda i, b, h: hh[i] * lax.rsqrt(jnp.mean(hh[i] * hh[i], axis=-1, keepdims=True) + RMS_EPS)
               * hnw_ref[:, hs(h)] * _sigmoid(pb_ref[b, win, hs(h, 3 * B_WIDTH)]))
    for i, (b, h) in enumerate(chains):
        c_ref[b, h] = c_new[i]
        n_ref[b, h:h + 1, :] = n_new[i]
        m_ref[b, h:h + 1, :] = jnp.broadcast_to(m_new[i], (1, B_HEAD_DIM))
        y_ref[b, win, hs(h, A_WIDTH)] = out[i]


READS_STATE, WROTE_STATE, DONE = "reads_state", "wrote_state", "done"
N_RWKV_W = 11
N_MLSTM_W = 4
N_CONSTS = 2


def _mix_kernel(*refs, chunk, hg, where, zero_first):
    it = iter(refs)
    take = lambda n: [next(it) for _ in range(n)]
    pa_ref, pb_ref, pg_ref, shift_ref, s0_ref, conv0_ref, c0_ref, n0_ref, m0_ref = take(9)
    rwkv_w, mlstm_w, slabs = take(N_RWKV_W), take(N_MLSTM_W), take(N_CONSTS)
    consts = {name: functools.partial(lambda ref, r0, rows, cols: ref[r0:r0 + rows, 0:cols], ref, *at)
              for ref, masks in zip(slabs, where) for name, at in masks}
    y_ref, shift_out_ref, s_out_ref, conv_out_ref, c_ref, n_ref, m_ref, pbuf, bds, xbuf = take(10)
    first = pl.program_id(1) == 0
    last = pl.program_id(1) == pl.num_programs(1) - 1
    zero_state = pl.program_id(0) == 0 if zero_first else None
    subs = pa_ref.shape[1] // chunk
    bodies = []
    for sub in range(subs):
        is_first = first if sub == 0 else None
        bodies.append([
            _rwkv_stages(pa_ref, shift_ref, s0_ref, rwkv_w, consts, y_ref, shift_out_ref, s_out_ref, pbuf, bds,
                         chunk=chunk, hg=hg, row0=sub * chunk, first=is_first,
                         last=last if sub == subs - 1 else None, zero_state=zero_state),
            _mlstm_stages(pb_ref, pg_ref, conv0_ref, c0_ref, n0_ref, m0_ref, mlstm_w, consts,
                          y_ref, conv_out_ref, c_ref, n_ref, m_ref, xbuf, chunk=chunk, row0=sub * chunk,
                          first=is_first, zero_state=zero_state),
        ])
    live = [(sub, kind) for sub in range(subs) for kind in range(2)]
    wrote, waiting = set(), set()
    while live:
        for key in list(live):
            sub, kind = key
            if key in waiting and (sub - 1, kind) not in wrote:
                continue
            waiting.discard(key)
            mark = next(bodies[sub][kind], DONE)
            if mark == READS_STATE and sub > 0:
                waiting.add(key)
            elif mark in (WROTE_STATE, DONE):
                wrote.add(key)
                if mark == DONE:
                    live.remove(key)


def _mix(pa, pb, pg, l, sl, states, rwkv_w, mlstm_w, chunk, subs, zero_first):
    shift0, s0, conv0, c0, n0, m0 = states
    bn, t, _ = pa.shape
    nb = MIX_STREAMS
    hg = _rwkv_heads_per_group(chunk)
    hist = B_CONV - 1
    consts, where = _mix_consts(chunk, nb)
    full = lambda a: pl.BlockSpec(a.shape, lambda i, c: (0,) * a.ndim)
    tok = lambda n: pl.BlockSpec((nb, subs * chunk, n), lambda i, c: (i, c, 0))
    per_stream = lambda *dims: pl.BlockSpec((nb,) + dims, lambda i, c: (i,) + (0,) * len(dims))
    state_shapes = [(1, A_COLS), (A_HEADS, A_HEAD_DIM, A_HEAD_DIM), (hist, 2 * B_WIDTH),
                    (B_HEADS, B_HEAD_DIM, B_HEAD_DIM), (B_HEADS, B_HEAD_DIM), (B_HEADS, B_HEAD_DIM)]
    return pl.pallas_call(
        functools.partial(_mix_kernel, chunk=chunk, hg=hg, where=where, zero_first=zero_first),
        grid=(bn // nb, t // (subs * chunk)),
        in_specs=[tok(A_COLS), tok(B_MAIN), tok(GATE_PAD)]
        + [_state_spec(a, sl, nb, zero_first) for a in states]
        + [_layer_spec(a, l) for a in rwkv_w + mlstm_w] + [full(a) for a in consts],
        out_specs=[tok(D_MIX)] + [per_stream(*dims) for dims in state_shapes],
        out_shape=[jax.ShapeDtypeStruct((bn, t, D_MIX), F32)]
        + [jax.ShapeDtypeStruct((bn,) + dims, F32) for dims in state_shapes],
        scratch_shapes=[
            pltpu.VMEM((nb, CARRY, A_COLS), F32),
            pltpu.VMEM((nb, A_HEADS // hg, hg * A_HEAD_DIM, hg * A_HEAD_DIM), F32),
            pltpu.VMEM((nb, CARRY, 2 * B_WIDTH), F32),
        ],
        compiler_params=pltpu.CompilerParams(
            dimension_semantics=("arbitrary", "arbitrary"), vmem_limit_bytes=VMEM_LIMIT),
        name="mixers",
    )(pa, pb, pg, *states, *rwkv_w, *mlstm_w, *consts)


def _ffn_kernel(x_ref, y_ref, f0_ref, wout_ref, g_ref, wup_ref, cw_ref, cb_ref,
                wdown_ref, gfin_ref, o_ref, fout_ref, ubuf, *, tile, final, zero_streams):
    nb = x_ref.shape[0]
    hist = FFN_CONV - 1
    rows_of = lambda ref: ref[0] if nb == 1 else jnp.concatenate([ref[b] for b in range(nb)], axis=0)

    @pl.when(pl.program_id(1) == 0)
    def _():
        ubuf[...] = jnp.zeros(ubuf.shape, F32)
        for b in range(zero_streams, nb):
            ubuf[b, CARRY - hist:CARRY, :] = f0_ref[b - zero_streams]

    x1 = rows_of(x_ref) + _dot(rows_of(y_ref).astype(BF16), wout_ref[...])
    h2 = _rms(x1, g_ref[...]).astype(BF16)
    acts = []
    for lo, hi in zip(FF_BOUNDS[:-1], FF_BOUNDS[1:]):
        cols = slice(lo, hi)
        u = _dot(h2, wup_ref[:, cols])
        gate = _dot(h2, wup_ref[:, D_FF + lo:D_FF + hi])
        convs = []
        for b in range(nb):
            ub = u[b * tile:(b + 1) * tile]
            prev = ubuf[b, :, cols]
            acc = cb_ref[:, cols] + cw_ref[hist:hist + 1, cols] * ub
            for j in range(hist):
                acc = acc + cw_ref[j:j + 1, cols] * _shift_rows(ub, prev, hist - j)
            ubuf[b, :, cols] = ub[tile - CARRY:tile]
            fout_ref[b, :, cols] = ub[tile - hist:tile]
            convs.append(acc)
        conv = convs[0] if nb == 1 else jnp.concatenate(convs, axis=0)
        acts.append((_silu(conv) * gate).astype(BF16))
    x2 = x1 + _dot(jnp.concatenate(acts, axis=1), wdown_ref[...])
    out = _rms(x2, gfin_ref[...]) if final else x2
    for b in range(nb):
        o_ref[b] = out[b * tile:(b + 1) * tile]


def _ffn(x, y, l, sl, f0, wts, gfin, nb, tile, final, zero_streams):
    bn, t, _ = x.shape
    hist = FFN_CONV - 1
    tok = pl.BlockSpec((nb, tile, D_MODEL), lambda i, c: (i, c, 0))
    fspec = pl.BlockSpec((nb, hist, D_FF), lambda i, c: (i, 0, 0))
    return pl.pallas_call(
        functools.partial(_ffn_kernel, tile=tile, final=final, zero_streams=zero_streams),
        grid=(bn // nb, t // tile),
        in_specs=[tok, tok, _state_spec(f0, sl, nb - zero_streams)]
        + [_layer_spec(a, l, resident=True) for a in wts] + [_layer_spec(gfin, 0, resident=True)],
        out_specs=[tok, fspec],
        out_shape=[
            jax.ShapeDtypeStruct((bn, t, D_MODEL), F32),
            jax.ShapeDtypeStruct((bn, hist, D_FF), F32),
        ],
        scratch_shapes=[pltpu.VMEM((nb, CARRY, D_FF), F32)],
        compiler_params=pltpu.CompilerParams(
            dimension_semantics=("arbitrary", "arbitrary"), vmem_limit_bytes=VMEM_LIMIT),
        name="out_ffn",
    )(x, y, f0, *wts, gfin)


def _layer(x, l, sl, st, w, *, in_tile, mix_chunk, mix_subs, ffn_streams, ffn_tile, final, zero_streams):
    bn, t, _ = x.shape
    pa, pb, pg = _in_proj(x.reshape(bn * t, D_MODEL), l, w["norm_mix"], w["w_in"], w["w_gate"], in_tile)
    pa = pa.reshape(bn, t, A_COLS)
    pb = pb.reshape(bn, t, B_MAIN)
    pg = pg.reshape(bn, t, GATE_PAD)
    y, *mix_states = _mix(pa, pb, pg, l, sl, st[:6], w["rwkv"], w["mlstm"], mix_chunk, mix_subs,
                          zero_streams > 0)
    x, fconv1 = _ffn(x, y, l, sl, st[6], w["ffn"], w["norm_final"], ffn_streams, ffn_tile, final, zero_streams)
    return x, tuple(mix_states) + (fconv1,)


def kernel(x_prompt, x_sample, state_rwkv_shift, state_rwkv_wkv, state_mlstm_conv, state_mlstm_C,
           state_mlstm_n, state_mlstm_m, state_ffn_conv, meta_tokens, norm_mix, w_in, a_mu, a_w0, a_w2,
           a_a0, a_a2, a_g2, a_k_k, a_k_a, a_r_k, a_ln_w, a_ln_b, b_conv_w, b_conv_b, b_i_bias, b_f_bias,
           b_hn_w, w_out, norm_ffn, w_up, ffn_conv_w, ffn_conv_b, w_down, norm_final):
    n_prompt = x_prompt.shape[0]
    n_sample = x_sample.shape[0]
    n_lead = n_prompt + n_sample
    assert x_sample.shape[1] == N_META
    assert n_prompt == MIX_STREAMS and n_lead % MIX_STREAMS == 0

    bf = lambda a: a.astype(BF16)
    vec = lambda a: a[:, None, :]
    n_gate = 2 * B_HEADS
    gate_pad = ((0, 0), (0, GATE_PAD - n_gate))
    w = {
        "norm_mix": vec(norm_mix),
        "w_in": bf(w_in),
        "w_gate": bf(jnp.pad(w_in[:, :, A_COLS + B_MAIN:], ((0, 0),) + gate_pad)),
        "rwkv": [vec(a_mu), vec(a_w0), bf(a_w2), vec(a_a0), bf(a_a2), bf(a_g2), vec(a_k_k), vec(a_k_a),
                 vec(a_r_k), vec(a_ln_w), vec(a_ln_b)],
        "mlstm": [b_conv_w, vec(b_conv_b), vec(jnp.pad(jnp.concatenate([b_i_bias, b_f_bias], axis=1), gate_pad)),
                  vec(b_hn_w)],
        "ffn": (bf(w_out), vec(norm_ffn), bf(w_up), ffn_conv_w, vec(ffn_conv_b), bf(w_down)),
        "norm_final": norm_final.reshape(1, 1, D_MODEL),
    }
    assert len(w["rwkv"]) == N_RWKV_W and len(w["mlstm"]) == N_MLSTM_W

    st_lead_in = (
        state_rwkv_shift[:, :, None, :], state_rwkv_wkv, state_mlstm_conv, state_mlstm_C, state_mlstm_n,
        jnp.broadcast_to(state_mlstm_m[..., None], state_mlstm_m.shape + (B_HEAD_DIM,)), state_ffn_conv,
    )
    meta = jnp.broadcast_to(meta_tokens[None].astype(x_prompt.dtype), (n_prompt, N_META, D_MODEL))
    x_lead = jnp.concatenate([meta, x_sample], axis=0)
    x_main = x_prompt

    main_states, lead_states = [], []
    for l in range(DEPTH):
        final = l == DEPTH - 1
        x_lead, st_lead = _layer(x_lead, l, l, st_lead_in, w, in_tile=n_lead * N_META, mix_chunk=N_META,
                                 mix_subs=1, ffn_streams=n_lead, ffn_tile=N_META, final=final,
                                 zero_streams=n_prompt)
        x_main, st_main = _layer(x_main, l, 0, tuple(s[None] for s in st_lead), w, in_tile=MAIN_TILE,
                                 mix_chunk=MAIN_CHUNK, mix_subs=MAIN_SUBCHUNKS, ffn_streams=1, ffn_tile=MAIN_TILE, final=final,
                                 zero_streams=0)
        main_states.append(st_main)
        lead_states.append(st_lead)

    def collect(per_layer, first):
        shift, wkv, bconv, c, n, m, fconv = (jnp.stack([st[i] for st in per_layer])[:, first:] for i in range(7))
        return (shift[:, :, 0, :], wkv, bconv, c, n, m[..., 0], fconv)

    return (x_main, x_lead[n_prompt:]) + collect(main_states, 0) + collect(lead_states, n_prompt)
```

```python
import functools
import math

import numpy as np

import jax
import jax.numpy as jnp
from jax import lax
from jax.experimental import pallas as pl
from jax.experimental.pallas import tpu as pltpu

D_MODEL = 1024
DEPTH = 2
N_META = 16
A_HEADS = 8
A_HEAD_DIM = 64
A_WIDTH = 512
A_DECAY_LORA = 64
A_AAA_LORA = 64
A_GATE_LORA = 128
A_COLS = 1792
B_HEADS = 4
B_HEAD_DIM = 128
B_WIDTH = 512
B_CONV = 4
B_MAIN = 4 * B_WIDTH
GATE_PAD = 128
D_MIX = A_WIDTH + B_WIDTH
D_FF = 2816
FFN_CONV = 3
RMS_EPS = 1e-6
GN_EPS = 64e-5
CARRY = 8
ROW_BLOCK = 16
LANES = 128
MXU_DIM = 256
MIX_STREAMS = 4
FF_BOUNDS = (0, 6 * MXU_DIM, D_FF)
MAIN_TILE = 512
MAIN_PROJ_TILE = 1024
MAIN_CHUNK = 64
MAIN_SUBCHUNKS = 2
VMEM_LIMIT = 56 * 1024 * 1024

F32 = jnp.float32
BF16 = jnp.bfloat16
NT_DIMS = (((1,), (1,)), ((), ()))
TN_DIMS = (((0,), (0,)), ((), ()))


def _dot(a, b):
    return jnp.dot(a, b, preferred_element_type=F32)


def _dot_nt(a, b):
    return lax.dot_general(a, b, NT_DIMS, preferred_element_type=F32)


def _dot_tn(a, b):
    return lax.dot_general(a, b, TN_DIMS, preferred_element_type=F32)


def _sigmoid(x):
    return 0.5 * jnp.tanh(0.5 * x) + 0.5


def _silu(x):
    half = 0.5 * x
    return half + half * jnp.tanh(half)


def _softplus(x):
    return jnp.maximum(x, 0.0) + jnp.log(1.0 + jnp.exp(-jnp.abs(x)))


def _rms(x, g):
    return x * lax.rsqrt(jnp.mean(x * x, axis=-1, keepdims=True) + RMS_EPS) * g


def _tril(n, strict):
    row = lax.broadcasted_iota(jnp.int32, (n, n), 0)
    col = lax.broadcasted_iota(jnp.int32, (n, n), 1)
    return (col < row) if strict else (col <= row)


def _split3(x):
    hi = x.astype(BF16)
    rest = x - hi.astype(F32)
    mid = rest.astype(BF16)
    lo = (rest - mid.astype(F32)).astype(BF16)
    return hi, mid, lo


def _shift_rows(x, prev, s):
    rolled = pltpu.roll(x, s, 0)
    from_prev = lax.broadcasted_iota(jnp.int32, prev.shape, 0) < s
    head = jnp.where(from_prev, pltpu.roll(prev, s, 0), rolled[0:CARRY])
    return jnp.concatenate([head, rolled[CARRY:]], axis=0)


def _rowwise(fn, n_rows, *xs):
    pieces = [fn(*[x[i:i + ROW_BLOCK] for x in xs]) for i in range(0, n_rows, ROW_BLOCK)]
    return [jnp.concatenate(col, axis=0) for col in zip(*pieces)]


def _layer_spec(a, l, resident=False):
    index = lambda *grid: (l,) + (0,) * (a.ndim - 1)
    if resident:
        return pl.BlockSpec((None,) + a.shape[1:], index, pipeline_mode=pl.Buffered(1))
    return pl.BlockSpec((None,) + a.shape[1:], index)


def _state_spec(a, l, nb, skip_first=False):
    block = (lambda i: jnp.maximum(i - 1, 0)) if skip_first else (lambda i: i)
    return pl.BlockSpec((None, nb) + a.shape[2:], lambda i, c: (l, block(i)) + (0,) * (a.ndim - 2))


def _unless(zero_state, load):
    if zero_state is None:
        load()
    else:
        pl.when(jnp.logical_not(zero_state))(load)


def _in_proj_kernel(x_ref, g_ref, w_ref, wg_ref, oa_ref, ob_ref, og_ref):
    h = _rms(x_ref[...], g_ref[...]).astype(BF16)
    oa_ref[...] = _dot(h, w_ref[:, 0:A_COLS])
    ob_ref[...] = _dot(h, w_ref[:, A_COLS:A_COLS + B_MAIN])
    og_ref[...] = _dot(h, wg_ref[...])


def _in_proj(x2d, l, g, w, wg, tm):
    m = x2d.shape[0]
    return pl.pallas_call(
        _in_proj_kernel,
        grid=(m // tm,),
        in_specs=[
            pl.BlockSpec((tm, D_MODEL), lambda i: (i, 0)),
            _layer_spec(g, l), _layer_spec(w, l, resident=True), _layer_spec(wg, l, resident=True),
        ],
        out_specs=[
            pl.BlockSpec((tm, A_COLS), lambda i: (i, 0)),
            pl.BlockSpec((tm, B_MAIN), lambda i: (i, 0)),
            pl.BlockSpec((tm, GATE_PAD), lambda i: (i, 0)),
        ],
        out_shape=[
            jax.ShapeDtypeStruct((m, A_COLS), F32),
            jax.ShapeDtypeStruct((m, B_MAIN), F32),
            jax.ShapeDtypeStruct((m, GATE_PAD), F32),
        ],
        compiler_params=pltpu.CompilerParams(
            dimension_semantics=("arbitrary",), vmem_limit_bytes=VMEM_LIMIT),
        name="in_proj",
    )(x2d, g, w, wg)


def _rwkv_heads_per_group(chunk):
    return min(A_HEADS, MXU_DIM // chunk)


def _mix_consts(chunk, streams):
    hg = _rwkv_heads_per_group(chunk)
    hl, w, rows = hg * chunk, hg * A_HEAD_DIM, streams * chunk
    ix = lambda n: (np.arange(n)[:, None], np.arange(n)[None, :])
    r, c = ix(MXU_DIM)
    seg = r // A_HEAD_DIM == c // A_HEAD_DIM
    r, c = ix(rows)
    tril = (r // chunk == c // chunk) & (c <= r)
    r, c = np.arange(2 * chunk)[:, None], np.arange(hl)[None, :]
    amask = np.where(r < chunk, c % chunk < r, c % chunk <= r - chunk)
    r, c = np.arange(chunk)[:, None], np.arange(hl)[None, :]
    eye = c % chunk == r
    r, c = np.arange(hl)[:, None], np.arange(w)[None, :]
    bdl = r // chunk == c // A_HEAD_DIM
    r, c = ix(hl)
    bdsq = r // chunk == c // chunk
    r, c = ix(w)
    smask = r // A_HEAD_DIM == c // A_HEAD_DIM

    def pack(dtype, **masks):
        width = max(m.shape[1] for m in masks.values())
        rows = np.cumsum([0] + [m.shape[0] for m in masks.values()])
        slab = np.concatenate([np.pad(m, ((0, 0), (0, width - m.shape[1]))) for m in masks.values()], axis=0)
        return jnp.asarray(slab, dtype), tuple((name, (int(r0),) + m.shape) for (name, m), r0 in zip(masks.items(), rows))

    slab16, where16 = pack(BF16, seg=seg, tril=tril, bdl=bdl, bdsq=bdsq)
    slab32, where32 = pack(F32, amask=amask, eye=eye, smask=smask)
    return (slab16, slab32), (where16, where32)


def _rwkv_stages(p_ref, shift_ref, s0_ref, wts, consts, y_ref, shift_out_ref, s_out_ref, pbuf, bds,
                 *, chunk, hg, row0, first, last, zero_state):
    mu_ref, w0_ref, w2_ref, a0_ref, a2_ref, g2_ref, kk_ref, ka_ref, rk_ref, lnw_ref, lnb_ref = wts
    L = chunk
    nb = p_ref.shape[0]
    rows = nb * L
    hl = hg * L
    wid = hg * A_HEAD_DIM
    groups = A_HEADS // hg
    diag = lambda h: (h // hg, slice((h % hg) * A_HEAD_DIM, (h % hg + 1) * A_HEAD_DIM))
    win = slice(row0, row0 + L)

    if first is not None:
        @pl.when(first)
        def _():
            bds[...] = jnp.zeros(bds.shape, F32)
            pbuf[...] = jnp.zeros(pbuf.shape, F32)

            def load():
                for b in range(nb):
                    pbuf[b, CARRY - 1:CARRY, :] = shift_ref[b]
                    for h in range(A_HEADS):
                        g, sl = diag(h)
                        bds[b, g, sl, sl] = s0_ref[b, h]
            _unless(zero_state, load)

    ps, prevs = [], []
    for b in range(nb):
        pb = p_ref[b, win, :]
        prevs.append(_shift_rows(pb, pbuf[b], 1))
        pbuf[b] = pb[L - CARRY:L, :]
        shift_out_ref[b] = pb[L - 1:L, :]
        ps.append(pb)
    i1, i2, i3 = A_WIDTH, 2 * A_WIDTH, 3 * A_WIDTH
    i4, i5 = i3 + A_DECAY_LORA, i3 + A_DECAY_LORA + A_AAA_LORA
    mu = mu_ref[...]

    def token_shift(p, prev):
        pm = p + (prev - p) * mu
        return (pm[:, :i1], pm[:, i1:i2], pm[:, i2:i3], jnp.tanh(pm[:, i3:i4]).astype(BF16),
                pm[:, i4:i5].astype(BF16), _sigmoid(pm[:, i5:]).astype(BF16))

    r, k, v, wl16, al16, gl16 = _rowwise(token_shift, rows, jnp.concatenate(ps, axis=0), jnp.concatenate(prevs, axis=0))
    yield
    seg = consts["seg"]()
    half = A_WIDTH // 2
    segsum = lambda t16: jnp.concatenate([_dot(t16[:, :half], seg), _dot(t16[:, half:], seg)], axis=1)
    z = w0_ref[...] + _dot(wl16, w2_ref[...])
    a_pre = a0_ref[...] + _dot(al16, a2_ref[...])
    gate = _dot(gl16, g2_ref[...])
    yield
    k_k, k_a, r_k = kk_ref[...], ka_ref[...], rk_ref[...]

    def decay_and_keys(k, z, a_pre):
        lw = -math.exp(-0.5) * _sigmoid(z)
        lw_hi = lw.astype(BF16)
        a = _sigmoid(a_pre)
        kk0 = k * k_k
        return (lw, lw_hi, (lw - lw_hi.astype(F32)).astype(BF16), a, kk0, (kk0 * kk0).astype(BF16),
                k * (1.0 + (a - 1.0) * k_a))

    lw, lw_hi, lw_lo, a, kk0, kk_sq16, kmod = _rowwise(decay_and_keys, rows, k, z, a_pre)
    yield
    tril = consts["tril"]()
    cum = _dot(tril, lw_hi) + _dot(tril, lw_lo)
    kk_ss = segsum(kk_sq16)
    yield

    rs = lambda b: slice(b * L, (b + 1) * L)
    cum_last = [cum[(b + 1) * L - 1:(b + 1) * L] for b in range(nb)]
    g_last = [jnp.exp(cum_last[b]) for b in range(nb)]

    def scaled_operands(b):
        def fn(r, v, kk0, kk_ss, a, kmod, lw, cum):
            kk = kk0 * lax.rsqrt(kk_ss + 1e-12)
            bvec = kk * a
            g_inv = jnp.exp(-cum)
            g_tail = jnp.exp(cum_last[b] - cum)
            return ((kk * jnp.exp(cum - lw)).astype(BF16), (r * jnp.exp(cum)).astype(BF16),
                    (kmod * g_inv).astype(BF16), (bvec * g_inv).astype(BF16), v.astype(BF16),
                    (kmod * g_tail).astype(BF16), (bvec * g_tail).astype(BF16), (r * kmod * r_k).astype(BF16))
        return fn

    per_stream = [_rowwise(scaled_operands(b), L, *[t[rs(b)] for t in (r, v, kk0, kk_ss, a, kmod, lw, cum)])
                  for b in range(nb)]
    kkg_s, rg_s, kd_s, bd_s, v_s, kt, bt, bonus_s = zip(*per_stream)
    bonus_ss = segsum(jnp.concatenate(bonus_s, axis=0))
    yield PROLOGUE_DONE

    keep = consts["amask"]() > 0.5
    eye = consts["eye"]()
    bdl = consts["bdl"]()
    bdsq = consts["bdsq"]()
    smask = consts["smask"]()
    def block_diag(t, mask, seg):
        per_tile = max(1, LANES // seg)
        tile = per_tile * seg
        rows = []
        for h in range(hg):
            cols = [t[:, j:j + tile] * mask[h * L:(h + 1) * L, j:j + tile] if j // tile == h // per_tile
                    else jnp.zeros((L, tile), t.dtype) for j in range(0, t.shape[1], tile)]
            rows.append(cols[0] if len(cols) == 1 else jnp.concatenate(cols, axis=1))
        return jnp.concatenate(rows, axis=0)

    lane_bd = lambda t: block_diag(t, bdl, A_HEAD_DIM)
    sq_bd = lambda t: block_diag(t, bdsq, L)

    chains = [(b, g) for b in range(nb) for g in range(groups)]
    ls = lambda g: slice(g * wid, (g + 1) * wid)
    each = lambda f: [f(i, b, g) for i, (b, g) in enumerate(chains)]

    kkg = each(lambda i, b, g: kkg_s[b][:, ls(g)])
    rg = each(lambda i, b, g: rg_s[b][:, ls(g)])
    v16 = each(lambda i, b, g: v_s[b][:, ls(g)])
    lhs = each(lambda i, b, g: jnp.concatenate([kkg[i], rg[i]], axis=0))
    a_k = each(lambda i, b, g: jnp.where(keep, _dot_nt(lhs[i], lane_bd(kd_s[b][:, ls(g)])), 0.0))
    yield
    a_b = each(lambda i, b, g: jnp.where(keep, _dot_nt(lhs[i], lane_bd(bd_s[b][:, ls(g)])), 0.0))
    a_kk = each(lambda i, b, g: a_k[i][:L].astype(BF16))
    a_rk = each(lambda i, b, g: a_k[i][L:].astype(BF16))
    a_rb = each(lambda i, b, g: a_b[i][L:].astype(BF16))
    yield

    npow = each(lambda i, b, g: -a_b[i][:L])
    tinv = each(lambda i, b, g: eye + npow[i])
    n16 = each(lambda i, b, g: npow[i].astype(BF16))
    npow = each(lambda i, b, g: _dot(n16[i], sq_bd(n16[i])))
    yield
    span = 2
    while span < L:
        n16 = each(lambda i, b, g: npow[i].astype(BF16))
        t_bd = each(lambda i, b, g: sq_bd(tinv[i].astype(BF16)))
        if 2 * span < L:
            both = each(lambda i, b, g: _dot(n16[i], jnp.concatenate([sq_bd(n16[i]), t_bd[i]], axis=1)))
            npow = each(lambda i, b, g: both[i][:, :hl])
            prod = each(lambda i, b, g: both[i][:, hl:])
        else:
            prod = each(lambda i, b, g: _dot(n16[i], t_bd[i]))
        tinv = each(lambda i, b, g: tinv[i] + prod[i])
        span *= 2
        yield

    yield READS_STATE
    s16 = each(lambda i, b, g: bds[b, g].astype(BF16))
    v_bd = each(lambda i, b, g: lane_bd(v16[i]))
    rhs = each(lambda i, b, g: (_dot_nt(kkg[i], s16[i]) + _dot(a_kk[i], v_bd[i])).astype(BF16))
    yield
    u16 = each(lambda i, b, g: _dot(tinv[i].astype(BF16), lane_bd(rhs[i])).astype(BF16))
    yield
    yc_ = each(lambda i, b, g: _dot_nt(rg[i], s16[i]) + _dot(a_rk[i], v_bd[i]) - _dot(a_rb[i], lane_bd(u16[i])))
    yield
    upd = each(lambda i, b, g: _dot_tn(jnp.concatenate([v16[i], -u16[i]], axis=0),
                                       jnp.concatenate([kt[b][:, ls(g)], bt[b][:, ls(g)]], axis=0)))
    for i, (b, g) in enumerate(chains):
        bds[b, g] = bds[b, g] * g_last[b][:, ls(g)] + upd[i] * smask
    yield WROTE_STATE
    ys = [jnp.concatenate(yc_[b * groups:(b + 1) * groups], axis=1) for b in range(nb)]
    y = jnp.concatenate(ys, axis=0)

    inv_d = 1.0 / A_HEAD_DIM
    y_sum = segsum(y.astype(BF16))

    def centre(y, y_sum):
        yc = y - y_sum * inv_d
        return yc, (yc * yc).astype(BF16)

    yc, yc_sq16 = _rowwise(centre, rows, y, y_sum)
    var_sum = segsum(yc_sq16)
    yield
    ln_w, ln_b = lnw_ref[...], lnb_ref[...]

    def finish(yc, var_sum, bonus_ss, v, gate):
        return ((yc * lax.rsqrt(var_sum * inv_d + GN_EPS) * ln_w + ln_b + bonus_ss * v) * gate,)

    out, = _rowwise(finish, rows, yc, var_sum, bonus_ss, v, gate)
    for b in range(nb):
        y_ref[b, win, 0:A_WIDTH] = out[b * L:(b + 1) * L]

    if last is not None:
        @pl.when(last)
        def _():
            for b in range(nb):
                for h in range(A_HEADS):
                    g, sl = diag(h)
                    s_out_ref[b, h] = bds[b, g, sl, sl]


def _mlstm_stages(pb_ref, pg_ref, conv0_ref, c0_ref, n0_ref, m0_ref, wts, consts,
                  y_ref, conv_out_ref, c_ref, n_ref, m_ref, xbuf, *, chunk, row0, first, zero_state):
    cw_ref, cb_ref, gb_ref, hnw_ref = wts
    win = slice(row0, row0 + chunk)
    L = chunk
    nb = pb_ref.shape[0]
    rows = nb * L
    hist = B_CONV - 1

    if first is not None:
        @pl.when(first)
        def _():
            xbuf[...] = jnp.zeros(xbuf.shape, F32)
            c_ref[...] = jnp.zeros(c_ref.shape, F32)
            n_ref[...] = jnp.zeros(n_ref.shape, F32)
            m_ref[...] = jnp.zeros(m_ref.shape, F32)

            def load():
                for b in range(nb):
                    xbuf[b, CARRY - hist:CARRY, :] = conv0_ref[b]
                c_ref[...] = c0_ref[...]
                n_ref[...] = n0_ref[...]
                m_ref[...] = m0_ref[...]
            _unless(zero_state, load)

    qk = []
    for b in range(nb):
        x = pb_ref[b, win, 0:2 * B_WIDTH]
        prev = xbuf[b]
        acc = cb_ref[...] + cw_ref[hist:hist + 1, :] * x
        for j in range(hist):
            acc = acc + cw_ref[j:j + 1, :] * _shift_rows(x, prev, hist - j)
        xbuf[b] = x[L - CARRY:L, :]
        conv_out_ref[b] = x[L - hist:L, :]
        qk.append(_silu(acc))
        yield

    gates = jnp.concatenate([pg_ref[b, win, :] for b in range(nb)], axis=0) + gb_ref[...]
    lane = lax.broadcasted_iota(jnp.int32, (rows, GATE_PAD), 1)
    tril = consts["tril"]()
    cum = sum(_dot(tril, part) for part in _split3(-_softplus(-gates)))
    gcols = jnp.where(lane < B_HEADS, gates, cum)
    grows = gcols.T
    causal = _tril(L, False)
    yield

    chains = [(b, h) for b in range(nb) for h in range(B_HEADS)]
    each = lambda f: [f(i, b, h) for i, (b, h) in enumerate(chains)]
    rs = lambda b: slice(b * L, (b + 1) * L)
    hs = lambda h, base=0: slice(base + h * B_HEAD_DIM, base + (h + 1) * B_HEAD_DIM)

    q = each(lambda i, b, h: qk[b][:, hs(h)])
    k = each(lambda i, b, h: qk[b][:, hs(h, B_WIDTH)] * (B_HEAD_DIM ** -0.5))
    q16 = each(lambda i, b, h: q[i].astype(BF16))
    v16 = each(lambda i, b, h: pb_ref[b, win, hs(h, 2 * B_WIDTH)].astype(BF16))
    li_col = each(lambda i, b, h: gcols[rs(b), h:h + 1])
    b_col = each(lambda i, b, h: gcols[rs(b), B_HEADS + h:B_HEADS + h + 1])
    li_row = each(lambda i, b, h: grows[h:h + 1, rs(b)])
    b_row = each(lambda i, b, h: grows[B_HEADS + h:B_HEADS + h + 1, rs(b)])
    yield READS_STATE
    c_prev = each(lambda i, b, h: c_ref[b, h])
    n_prev = each(lambda i, b, h: n_ref[b, h:h + 1, :])
    m_prev = each(lambda i, b, h: m_ref[b, h:h + 1, 0:1])

    dm = each(lambda i, b, h: jnp.where(causal, b_col[i] - b_row[i] + li_row[i], -jnp.inf))
    inter = each(lambda i, b, h: b_col[i] + m_prev[i])
    mt = each(lambda i, b, h: jnp.maximum(inter[i], jnp.max(dm[i], axis=-1, keepdims=True)))
    wo = each(lambda i, b, h: jnp.exp(inter[i] - mt[i]))
    yield
    s = each(lambda i, b, h: _dot_nt(q16[i], k[i].astype(BF16)) * jnp.exp(dm[i] - mt[i]))
    yield
    num = each(lambda i, b, h: wo[i] * _dot(q16[i], c_prev[i].astype(BF16)) + _dot(s[i].astype(BF16), v16[i]))
    yield
    den = each(lambda i, b, h: wo[i] * jnp.sum(q[i] * n_prev[i], axis=-1, keepdims=True)
               + jnp.sum(s[i], axis=-1, keepdims=True))
    hh = each(lambda i, b, h: num[i] * (1.0 / jnp.maximum(jnp.abs(den[i]), jnp.exp(-mt[i]))))
    yield

    m_new = each(lambda i, b, h: mt[i][L - 1:L, :])
    b_last = each(lambda i, b, h: b_col[i][L - 1:L, :])
    kw = each(lambda i, b, h: k[i] * jnp.exp(b_last[i] - b_col[i] + li_col[i] - m_new[i]))
    dec = each(lambda i, b, h: jnp.exp(b_last[i] + m_prev[i] - m_new[i]))
    yield
    c_new = each(lambda i, b, h: dec[i] * c_prev[i] + _dot_tn(kw[i].astype(BF16), v16[i]))
    n_new = each(lambda i, b, h: dec[i] * n_prev[i] + jnp.sum(kw[i], axis=0, keepdims=True))
    yield
    out = each(lambda i, b, h: hh[i] * lax.rsqrt(jnp.mean(hh[i] * hh[i], axis=-1, keepdims=True) + RMS_EPS)
               * hnw_ref[:, hs(h)] * _sigmoid(pb_ref[b, win, hs(h, 3 * B_WIDTH)]))
    for i, (b, h) in enumerate(chains):
        c_ref[b, h] = c_new[i]
        n_ref[b, h:h + 1, :] = n_new[i]
        m_ref[b, h:h + 1, :] = jnp.broadcast_to(m_new[i], (1, B_HEAD_DIM))
        y_ref[b, win, hs(h, A_WIDTH)] = out[i]


PROLOGUE_DONE, READS_STATE, WROTE_STATE, DONE = "prologue_done", "reads_state", "wrote_state", "done"
N_RWKV_W = 11
N_MLSTM_W = 4
N_CONSTS = 2


def _mix_kernel(*refs, chunk, hg, where, zero_first):
    it = iter(refs)
    take = lambda n: [next(it) for _ in range(n)]
    pa_ref, pb_ref, pg_ref, shift_ref, s0_ref, conv0_ref, c0_ref, n0_ref, m0_ref = take(9)
    rwkv_w, mlstm_w, slabs = take(N_RWKV_W), take(N_MLSTM_W), take(N_CONSTS)
    consts = {name: functools.partial(lambda ref, r0, rows, cols: ref[r0:r0 + rows, 0:cols], ref, *at)
              for ref, masks in zip(slabs, where) for name, at in masks}
    y_ref, shift_out_ref, s_out_ref, conv_out_ref, c_ref, n_ref, m_ref, pbuf, bds, xbuf = take(10)
    first = pl.program_id(1) == 0
    last = pl.program_id(1) == pl.num_programs(1) - 1
    zero_state = pl.program_id(0) == 0 if zero_first else None
    subs = pa_ref.shape[1] // chunk
    bodies = []
    for sub in range(subs):
        is_first = first if sub == 0 else None
        bodies.append([
            _rwkv_stages(pa_ref, shift_ref, s0_ref, rwkv_w, consts, y_ref, shift_out_ref, s_out_ref, pbuf, bds,
                         chunk=chunk, hg=hg, row0=sub * chunk, first=is_first,
                         last=last if sub == subs - 1 else None, zero_state=zero_state),
            _mlstm_stages(pb_ref, pg_ref, conv0_ref, c0_ref, n0_ref, m0_ref, mlstm_w, consts,
                          y_ref, conv_out_ref, c_ref, n_ref, m_ref, xbuf, chunk=chunk, row0=sub * chunk,
                          first=is_first, zero_state=zero_state),
        ])
    live = [(sub, kind) for sub in range(subs) for kind in range(2)]
    wrote, waiting, started = set(), set(), {0}
    while live:
        for key in list(live):
            sub, kind = key
            if sub not in started or (key in waiting and (sub - 1, kind) not in wrote):
                continue
            waiting.discard(key)
            mark = next(bodies[sub][kind], DONE)
            if mark == PROLOGUE_DONE:
                started.add(sub + 1)
            elif mark == READS_STATE and sub > 0:
                waiting.add(key)
            elif mark in (WROTE_STATE, DONE):
                wrote.add(key)
                if mark == DONE:
                    live.remove(key)


def _mix(pa, pb, pg, l, sl, states, rwkv_w, mlstm_w, chunk, subs, zero_first):
    shift0, s0, conv0, c0, n0, m0 = states
    bn, t, _ = pa.shape
    nb = MIX_STREAMS
    hg = _rwkv_heads_per_group(chunk)
    hist = B_CONV - 1
    consts, where = _mix_consts(chunk, nb)
    full = lambda a: pl.BlockSpec(a.shape, lambda i, c: (0,) * a.ndim)
    tok = lambda n: pl.BlockSpec((nb, subs * chunk, n), lambda i, c: (i, c, 0))
    per_stream = lambda *dims: pl.BlockSpec((nb,) + dims, lambda i, c: (i,) + (0,) * len(dims))
    state_shapes = [(1, A_COLS), (A_HEADS, A_HEAD_DIM, A_HEAD_DIM), (hist, 2 * B_WIDTH),
                    (B_HEADS, B_HEAD_DIM, B_HEAD_DIM), (B_HEADS, B_HEAD_DIM), (B_HEADS, B_HEAD_DIM)]
    return pl.pallas_call(
        functools.partial(_mix_kernel, chunk=chunk, hg=hg, where=where, zero_first=zero_first),
        grid=(bn // nb, t // (subs * chunk)),
        in_specs=[tok(A_COLS), tok(B_MAIN), tok(GATE_PAD)]
        + [_state_spec(a, sl, nb, zero_first) for a in states]
        + [_layer_spec(a, l) for a in rwkv_w + mlstm_w] + [full(a) for a in consts],
        out_specs=[tok(D_MIX)] + [per_stream(*dims) for dims in state_shapes],
        out_shape=[jax.ShapeDtypeStruct((bn, t, D_MIX), F32)]
        + [jax.ShapeDtypeStruct((bn,) + dims, F32) for dims in state_shapes],
        scratch_shapes=[
            pltpu.VMEM((nb, CARRY, A_COLS), F32),
            pltpu.VMEM((nb, A_HEADS // hg, hg * A_HEAD_DIM, hg * A_HEAD_DIM), F32),
            pltpu.VMEM((nb, CARRY, 2 * B_WIDTH), F32),
        ],
        compiler_params=pltpu.CompilerParams(
            dimension_semantics=("arbitrary", "arbitrary"), vmem_limit_bytes=VMEM_LIMIT),
        name="mixers",
    )(pa, pb, pg, *states, *rwkv_w, *mlstm_w, *consts)


def _ffn_kernel(x_ref, y_ref, f0_ref, wout_ref, g_ref, wup_ref, cw_ref, cb_ref,
                wdown_ref, gfin_ref, o_ref, fout_ref, ubuf, *, tile, final, zero_streams):
    nb = x_ref.shape[0]
    hist = FFN_CONV - 1
    rows_of = lambda ref: ref[0] if nb == 1 else jnp.concatenate([ref[b] for b in range(nb)], axis=0)

    @pl.when(pl.program_id(1) == 0)
    def _():
        ubuf[...] = jnp.zeros(ubuf.shape, F32)
        for b in range(zero_streams, nb):
            ubuf[b, CARRY - hist:CARRY, :] = f0_ref[b - zero_streams]

    x1 = rows_of(x_ref) + _dot(rows_of(y_ref).astype(BF16), wout_ref[...])
    h2 = _rms(x1, g_ref[...]).astype(BF16)
    acts = []
    for lo, hi in zip(FF_BOUNDS[:-1], FF_BOUNDS[1:]):
        cols = slice(lo, hi)
        u = _dot(h2, wup_ref[:, cols])
        gate = _dot(h2, wup_ref[:, D_FF + lo:D_FF + hi])
        convs = []
        for b in range(nb):
            ub = u[b * tile:(b + 1) * tile]
            prev = ubuf[b, :, cols]
            acc = cb_ref[:, cols] + cw_ref[hist:hist + 1, cols] * ub
            for j in range(hist):
                acc = acc + cw_ref[j:j + 1, cols] * _shift_rows(ub, prev, hist - j)
            ubuf[b, :, cols] = ub[tile - CARRY:tile]
            fout_ref[b, :, cols] = ub[tile - hist:tile]
            convs.append(acc)
        conv = convs[0] if nb == 1 else jnp.concatenate(convs, axis=0)
        acts.append((_silu(conv) * gate).astype(BF16))
    x2 = x1 + _dot(jnp.concatenate(acts, axis=1), wdown_ref[...])
    out = _rms(x2, gfin_ref[...]) if final else x2
    for b in range(nb):
        o_ref[b] = out[b * tile:(b + 1) * tile]


def _ffn(x, y, l, sl, f0, wts, gfin, nb, tile, final, zero_streams):
    bn, t, _ = x.shape
    hist = FFN_CONV - 1
    tok = pl.BlockSpec((nb, tile, D_MODEL), lambda i, c: (i, c, 0))
    fspec = pl.BlockSpec((nb, hist, D_FF), lambda i, c: (i, 0, 0))
    return pl.pallas_call(
        functools.partial(_ffn_kernel, tile=tile, final=final, zero_streams=zero_streams),
        grid=(bn // nb, t // tile),
        in_specs=[tok, tok, _state_spec(f0, sl, nb - zero_streams)]
        + [_layer_spec(a, l, resident=True) for a in wts] + [_layer_spec(gfin, 0, resident=True)],
        out_specs=[tok, fspec],
        out_shape=[
            jax.ShapeDtypeStruct((bn, t, D_MODEL), F32),
            jax.ShapeDtypeStruct((bn, hist, D_FF), F32),
        ],
        scratch_shapes=[pltpu.VMEM((nb, CARRY, D_FF), F32)],
        compiler_params=pltpu.CompilerParams(
            dimension_semantics=("arbitrary", "arbitrary"), vmem_limit_bytes=VMEM_LIMIT),
        name="out_ffn",
    )(x, y, f0, *wts, gfin)


def _layer(x, l, sl, st, w, *, in_tile, mix_chunk, mix_subs, ffn_streams, ffn_tile, final, zero_streams):
    bn, t, _ = x.shape
    pa, pb, pg = _in_proj(x.reshape(bn * t, D_MODEL), l, w["norm_mix"], w["w_in"], w["w_gate"], in_tile)
    pa = pa.reshape(bn, t, A_COLS)
    pb = pb.reshape(bn, t, B_MAIN)
    pg = pg.reshape(bn, t, GATE_PAD)
    y, *mix_states = _mix(pa, pb, pg, l, sl, st[:6], w["rwkv"], w["mlstm"], mix_chunk, mix_subs,
                          zero_streams > 0)
    x, fconv1 = _ffn(x, y, l, sl, st[6], w["ffn"], w["norm_final"], ffn_streams, ffn_tile, final, zero_streams)
    return x, tuple(mix_states) + (fconv1,)


def kernel(x_prompt, x_sample, state_rwkv_shift, state_rwkv_wkv, state_mlstm_conv, state_mlstm_C,
           state_mlstm_n, state_mlstm_m, state_ffn_conv, meta_tokens, norm_mix, w_in, a_mu, a_w0, a_w2,
           a_a0, a_a2, a_g2, a_k_k, a_k_a, a_r_k, a_ln_w, a_ln_b, b_conv_w, b_conv_b, b_i_bias, b_f_bias,
           b_hn_w, w_out, norm_ffn, w_up, ffn_conv_w, ffn_conv_b, w_down, norm_final):
    n_prompt = x_prompt.shape[0]
    n_sample = x_sample.shape[0]
    n_lead = n_prompt + n_sample
    assert x_sample.shape[1] == N_META
    assert n_prompt == MIX_STREAMS and n_lead % MIX_STREAMS == 0

    bf = lambda a: a.astype(BF16)
    vec = lambda a: a[:, None, :]
    n_gate = 2 * B_HEADS
    gate_pad = ((0, 0), (0, GATE_PAD - n_gate))
    w = {
        "norm_mix": vec(norm_mix),
        "w_in": bf(w_in),
        "w_gate": bf(jnp.pad(w_in[:, :, A_COLS + B_MAIN:], ((0, 0),) + gate_pad)),
        "rwkv": [vec(a_mu), vec(a_w0), bf(a_w2), vec(a_a0), bf(a_a2), bf(a_g2), vec(a_k_k), vec(a_k_a),
                 vec(a_r_k), vec(a_ln_w), vec(a_ln_b)],
        "mlstm": [b_conv_w, vec(b_conv_b), vec(jnp.pad(jnp.concatenate([b_i_bias, b_f_bias], axis=1), gate_pad)),
                  vec(b_hn_w)],
        "ffn": (bf(w_out), vec(norm_ffn), bf(w_up), ffn_conv_w, vec(ffn_conv_b), bf(w_down)),
        "norm_final": norm_final.reshape(1, 1, D_MODEL),
    }
    assert len(w["rwkv"]) == N_RWKV_W and len(w["mlstm"]) == N_MLSTM_W

    st_lead_in = (
        state_rwkv_shift[:, :, None, :], state_rwkv_wkv, state_mlstm_conv, state_mlstm_C, state_mlstm_n,
        jnp.broadcast_to(state_mlstm_m[..., None], state_mlstm_m.shape + (B_HEAD_DIM,)), state_ffn_conv,
    )
    meta = jnp.broadcast_to(meta_tokens[None].astype(x_prompt.dtype), (n_prompt, N_META, D_MODEL))
    x_lead = jnp.concatenate([meta, x_sample], axis=0)
    x_main = x_prompt

    main_states, lead_states = [], []
    for l in range(DEPTH):
        final = l == DEPTH - 1
        x_lead, st_lead = _layer(x_lead, l, l, st_lead_in, w, in_tile=n_lead * N_META, mix_chunk=N_META,
                                 mix_subs=1, ffn_streams=n_lead, ffn_tile=N_META, final=final,
                                 zero_streams=n_prompt)
        x_main, st_main = _layer(x_main, l, 0, tuple(s[None] for s in st_lead), w, in_tile=MAIN_PROJ_TILE,
                                 mix_chunk=MAIN_CHUNK, mix_subs=MAIN_SUBCHUNKS, ffn_streams=1, ffn_tile=MAIN_TILE, final=final,
                                 zero_streams=0)
        main_states.append(st_main)
        lead_states.append(st_lead)

    def collect(per_layer, first):
        shift, wkv, bconv, c, n, m, fconv = (jnp.stack([st[i] for st in per_layer])[:, first:] for i in range(7))
        return (shift[:, :, 0, :], wkv, bconv, c, n, m[..., 0], fconv)

    return (x_main, x_lead[n_prompt:]) + collect(main_states, 0) + collect(lead_states, n_prompt)
```

```python
import functools
import math

import numpy as np

import jax
import jax.numpy as jnp
from jax import lax
from jax.experimental import pallas as pl
from jax.experimental.pallas import tpu as pltpu

D_MODEL = 1024
DEPTH = 2
N_META = 16
A_HEADS = 8
A_HEAD_DIM = 64
A_WIDTH = 512
A_DECAY_LORA = 64
A_AAA_LORA = 64
A_GATE_LORA = 128
A_COLS = 1792
B_HEADS = 4
B_HEAD_DIM = 128
B_WIDTH = 512
B_CONV = 4
B_MAIN = 4 * B_WIDTH
GATE_PAD = 128
D_MIX = A_WIDTH + B_WIDTH
D_FF = 2816
FFN_CONV = 3
RMS_EPS = 1e-6
GN_EPS = 64e-5
CARRY = 8
ROW_BLOCK = 16
LANES = 128
MXU_DIM = 256
LEAD_STREAMS = 4
MAIN_STREAMS = 2
FF_BOUNDS = (0, 6 * MXU_DIM, D_FF)
MAIN_TILE = 512
MAIN_PROJ_TILE = 1024
MAIN_CHUNK = 64
MAIN_SUBCHUNKS = 4
VMEM_LIMIT = 56 * 1024 * 1024

F32 = jnp.float32
BF16 = jnp.bfloat16
NT_DIMS = (((1,), (1,)), ((), ()))
TN_DIMS = (((0,), (0,)), ((), ()))


def _dot(a, b):
    return jnp.dot(a, b, preferred_element_type=F32)


def _dot_nt(a, b):
    return lax.dot_general(a, b, NT_DIMS, preferred_element_type=F32)


def _dot_tn(a, b):
    return lax.dot_general(a, b, TN_DIMS, preferred_element_type=F32)


def _sigmoid(x):
    return 0.5 * jnp.tanh(0.5 * x) + 0.5


def _silu(x):
    half = 0.5 * x
    return half + half * jnp.tanh(half)


def _softplus(x):
    return jnp.maximum(x, 0.0) + jnp.log(1.0 + jnp.exp(-jnp.abs(x)))


def _rms(x, g):
    return x * lax.rsqrt(jnp.mean(x * x, axis=-1, keepdims=True) + RMS_EPS) * g


def _tril(n, strict):
    row = lax.broadcasted_iota(jnp.int32, (n, n), 0)
    col = lax.broadcasted_iota(jnp.int32, (n, n), 1)
    return (col < row) if strict else (col <= row)


def _split3(x):
    hi = x.astype(BF16)
    rest = x - hi.astype(F32)
    mid = rest.astype(BF16)
    lo = (rest - mid.astype(F32)).astype(BF16)
    return hi, mid, lo


def _shift_rows(x, prev, s):
    rolled = pltpu.roll(x, s, 0)
    from_prev = lax.broadcasted_iota(jnp.int32, prev.shape, 0) < s
    head = jnp.where(from_prev, pltpu.roll(prev, s, 0), rolled[0:CARRY])
    return jnp.concatenate([head, rolled[CARRY:]], axis=0)


def _rowwise(fn, n_rows, *xs):
    pieces = [fn(*[x[i:i + ROW_BLOCK] for x in xs]) for i in range(0, n_rows, ROW_BLOCK)]
    return [jnp.concatenate(col, axis=0) for col in zip(*pieces)]


def _layer_spec(a, l, resident=False):
    index = lambda *grid: (l,) + (0,) * (a.ndim - 1)
    if resident:
        return pl.BlockSpec((None,) + a.shape[1:], index, pipeline_mode=pl.Buffered(1))
    return pl.BlockSpec((None,) + a.shape[1:], index)


def _state_spec(a, l, nb, skip_first=False):
    block = (lambda i: jnp.maximum(i - 1, 0)) if skip_first else (lambda i: i)
    return pl.BlockSpec((None, nb) + a.shape[2:], lambda i, c: (l, block(i)) + (0,) * (a.ndim - 2))


def _unless(zero_state, load):
    if zero_state is None:
        load()
    else:
        pl.when(jnp.logical_not(zero_state))(load)


def _in_proj_kernel(x_ref, g_ref, w_ref, wg_ref, oa_ref, ob_ref, og_ref):
    h = _rms(x_ref[...], g_ref[...]).astype(BF16)
    oa_ref[...] = _dot(h, w_ref[:, 0:A_COLS])
    ob_ref[...] = _dot(h, w_ref[:, A_COLS:A_COLS + B_MAIN])
    og_ref[...] = _dot(h, wg_ref[...])


def _in_proj(x2d, l, g, w, wg, tm):
    m = x2d.shape[0]
    return pl.pallas_call(
        _in_proj_kernel,
        grid=(m // tm,),
        in_specs=[
            pl.BlockSpec((tm, D_MODEL), lambda i: (i, 0)),
            _layer_spec(g, l), _layer_spec(w, l, resident=True), _layer_spec(wg, l, resident=True),
        ],
        out_specs=[
            pl.BlockSpec((tm, A_COLS), lambda i: (i, 0)),
            pl.BlockSpec((tm, B_MAIN), lambda i: (i, 0)),
            pl.BlockSpec((tm, GATE_PAD), lambda i: (i, 0)),
        ],
        out_shape=[
            jax.ShapeDtypeStruct((m, A_COLS), F32),
            jax.ShapeDtypeStruct((m, B_MAIN), F32),
            jax.ShapeDtypeStruct((m, GATE_PAD), F32),
        ],
        compiler_params=pltpu.CompilerParams(
            dimension_semantics=("arbitrary",), vmem_limit_bytes=VMEM_LIMIT),
        name="in_proj",
    )(x2d, g, w, wg)


def _rwkv_heads_per_group(chunk):
    return min(A_HEADS, MXU_DIM // chunk)


def _mix_consts(chunk, streams):
    hg = _rwkv_heads_per_group(chunk)
    hl, w, rows = hg * chunk, hg * A_HEAD_DIM, streams * chunk
    ix = lambda n: (np.arange(n)[:, None], np.arange(n)[None, :])
    r, c = ix(MXU_DIM)
    seg = r // A_HEAD_DIM == c // A_HEAD_DIM
    r, c = ix(rows)
    tril = (r // chunk == c // chunk) & (c <= r)
    r, c = np.arange(2 * chunk)[:, None], np.arange(hl)[None, :]
    amask = np.where(r < chunk, c % chunk < r, c % chunk <= r - chunk)
    r, c = np.arange(chunk)[:, None], np.arange(hl)[None, :]
    eye = c % chunk == r
    r, c = np.arange(hl)[:, None], np.arange(w)[None, :]
    bdl = r // chunk == c // A_HEAD_DIM
    r, c = ix(hl)
    bdsq = r // chunk == c // chunk
    r, c = ix(w)
    smask = r // A_HEAD_DIM == c // A_HEAD_DIM

    def pack(dtype, **masks):
        width = max(m.shape[1] for m in masks.values())
        rows = np.cumsum([0] + [m.shape[0] for m in masks.values()])
        slab = np.concatenate([np.pad(m, ((0, 0), (0, width - m.shape[1]))) for m in masks.values()], axis=0)
        return jnp.asarray(slab, dtype), tuple((name, (int(r0),) + m.shape) for (name, m), r0 in zip(masks.items(), rows))

    slab16, where16 = pack(BF16, seg=seg, tril=tril, bdl=bdl, bdsq=bdsq)
    slab32, where32 = pack(F32, amask=amask, eye=eye, smask=smask)
    return (slab16, slab32), (where16, where32)


def _rwkv_stages(p_ref, shift_ref, s0_ref, wts, consts, y_ref, shift_out_ref, s_out_ref, pbuf, bds,
                 *, chunk, hg, row0, first, last, zero_state):
    mu_ref, w0_ref, w2_ref, a0_ref, a2_ref, g2_ref, kk_ref, ka_ref, rk_ref, lnw_ref, lnb_ref = wts
    L = chunk
    nb = p_ref.shape[0]
    rows = nb * L
    hl = hg * L
    wid = hg * A_HEAD_DIM
    groups = A_HEADS // hg
    diag = lambda h: (h // hg, slice((h % hg) * A_HEAD_DIM, (h % hg + 1) * A_HEAD_DIM))
    win = slice(row0, row0 + L)

    if first is not None:
        @pl.when(first)
        def _():
            bds[...] = jnp.zeros(bds.shape, F32)
            pbuf[...] = jnp.zeros(pbuf.shape, F32)

            def load():
                for b in range(nb):
                    pbuf[b, CARRY - 1:CARRY, :] = shift_ref[b]
                    for h in range(A_HEADS):
                        g, sl = diag(h)
                        bds[b, g, sl, sl] = s0_ref[b, h]
            _unless(zero_state, load)

    ps, prevs = [], []
    for b in range(nb):
        pb = p_ref[b, win, :]
        prevs.append(_shift_rows(pb, pbuf[b], 1))
        pbuf[b] = pb[L - CARRY:L, :]
        shift_out_ref[b] = pb[L - 1:L, :]
        ps.append(pb)
    i1, i2, i3 = A_WIDTH, 2 * A_WIDTH, 3 * A_WIDTH
    i4, i5 = i3 + A_DECAY_LORA, i3 + A_DECAY_LORA + A_AAA_LORA
    mu = mu_ref[...]

    def token_shift(p, prev):
        pm = p + (prev - p) * mu
        return (pm[:, :i1], pm[:, i1:i2], pm[:, i2:i3], jnp.tanh(pm[:, i3:i4]).astype(BF16),
                pm[:, i4:i5].astype(BF16), _sigmoid(pm[:, i5:]).astype(BF16))

    r, k, v, wl16, al16, gl16 = _rowwise(token_shift, rows, jnp.concatenate(ps, axis=0), jnp.concatenate(prevs, axis=0))
    yield
    seg = consts["seg"]()
    half = A_WIDTH // 2
    segsum = lambda t16: jnp.concatenate([_dot(t16[:, :half], seg), _dot(t16[:, half:], seg)], axis=1)
    z = w0_ref[...] + _dot(wl16, w2_ref[...])
    a_pre = a0_ref[...] + _dot(al16, a2_ref[...])
    gate = _dot(gl16, g2_ref[...])
    yield
    k_k, k_a, r_k = kk_ref[...], ka_ref[...], rk_ref[...]

    def decay_and_keys(k, z, a_pre):
        lw = -math.exp(-0.5) * _sigmoid(z)
        lw_hi = lw.astype(BF16)
        a = _sigmoid(a_pre)
        kk0 = k * k_k
        return (lw, lw_hi, (lw - lw_hi.astype(F32)).astype(BF16), a, kk0, (kk0 * kk0).astype(BF16),
                k * (1.0 + (a - 1.0) * k_a))

    lw, lw_hi, lw_lo, a, kk0, kk_sq16, kmod = _rowwise(decay_and_keys, rows, k, z, a_pre)
    yield
    tril = consts["tril"]()
    cum = _dot(tril, lw_hi) + _dot(tril, lw_lo)
    kk_ss = segsum(kk_sq16)
    yield

    rs = lambda b: slice(b * L, (b + 1) * L)
    cum_last = [cum[(b + 1) * L - 1:(b + 1) * L] for b in range(nb)]
    g_last = [jnp.exp(cum_last[b]) for b in range(nb)]

    def scaled_operands(b):
        def fn(r, v, kk0, kk_ss, a, kmod, lw, cum):
            kk = kk0 * lax.rsqrt(kk_ss + 1e-12)
            bvec = kk * a
            g_inv = jnp.exp(-cum)
            g_tail = jnp.exp(cum_last[b] - cum)
            return ((kk * jnp.exp(cum - lw)).astype(BF16), (r * jnp.exp(cum)).astype(BF16),
                    (kmod * g_inv).astype(BF16), (bvec * g_inv).astype(BF16), v.astype(BF16),
                    (kmod * g_tail).astype(BF16), (bvec * g_tail).astype(BF16), (r * kmod * r_k).astype(BF16))
        return fn

    per_stream = [_rowwise(scaled_operands(b), L, *[t[rs(b)] for t in (r, v, kk0, kk_ss, a, kmod, lw, cum)])
                  for b in range(nb)]
    kkg_s, rg_s, kd_s, bd_s, v_s, kt, bt, bonus_s = zip(*per_stream)
    bonus_ss = segsum(jnp.concatenate(bonus_s, axis=0))
    yield PROLOGUE_DONE

    keep = consts["amask"]() > 0.5
    eye = consts["eye"]()
    bdl = consts["bdl"]()
    bdsq = consts["bdsq"]()
    smask = consts["smask"]()
    def block_diag(t, mask, seg):
        per_tile = max(1, LANES // seg)
        tile = per_tile * seg
        rows = []
        for h in range(hg):
            cols = [t[:, j:j + tile] * mask[h * L:(h + 1) * L, j:j + tile] if j // tile == h // per_tile
                    else jnp.zeros((L, tile), t.dtype) for j in range(0, t.shape[1], tile)]
            rows.append(cols[0] if len(cols) == 1 else jnp.concatenate(cols, axis=1))
        return jnp.concatenate(rows, axis=0)

    lane_bd = lambda t: block_diag(t, bdl, A_HEAD_DIM)
    sq_bd = lambda t: block_diag(t, bdsq, L)

    chains = [(b, g) for b in range(nb) for g in range(groups)]
    ls = lambda g: slice(g * wid, (g + 1) * wid)
    each = lambda f: [f(i, b, g) for i, (b, g) in enumerate(chains)]

    kkg = each(lambda i, b, g: kkg_s[b][:, ls(g)])
    rg = each(lambda i, b, g: rg_s[b][:, ls(g)])
    v16 = each(lambda i, b, g: v_s[b][:, ls(g)])
    lhs = each(lambda i, b, g: jnp.concatenate([kkg[i], rg[i]], axis=0))
    a_k = each(lambda i, b, g: jnp.where(keep, _dot_nt(lhs[i], lane_bd(kd_s[b][:, ls(g)])), 0.0))
    yield
    a_b = each(lambda i, b, g: jnp.where(keep, _dot_nt(lhs[i], lane_bd(bd_s[b][:, ls(g)])), 0.0))
    a_kk = each(lambda i, b, g: a_k[i][:L].astype(BF16))
    a_rk = each(lambda i, b, g: a_k[i][L:].astype(BF16))
    a_rb = each(lambda i, b, g: a_b[i][L:].astype(BF16))
    yield

    npow = each(lambda i, b, g: -a_b[i][:L])
    tinv = each(lambda i, b, g: eye + npow[i])
    n16 = each(lambda i, b, g: npow[i].astype(BF16))
    npow = each(lambda i, b, g: _dot(n16[i], sq_bd(n16[i])))
    yield
    span = 2
    while span < L:
        n16 = each(lambda i, b, g: npow[i].astype(BF16))
        t_bd = each(lambda i, b, g: sq_bd(tinv[i].astype(BF16)))
        if 2 * span < L:
            both = each(lambda i, b, g: _dot(n16[i], jnp.concatenate([sq_bd(n16[i]), t_bd[i]], axis=1)))
            npow = each(lambda i, b, g: both[i][:, :hl])
            prod = each(lambda i, b, g: both[i][:, hl:])
        else:
            prod = each(lambda i, b, g: _dot(n16[i], t_bd[i]))
        tinv = each(lambda i, b, g: tinv[i] + prod[i])
        span *= 2
        yield

    yield READS_STATE
    s16 = each(lambda i, b, g: bds[b, g].astype(BF16))
    v_bd = each(lambda i, b, g: lane_bd(v16[i]))
    rhs = each(lambda i, b, g: (_dot_nt(kkg[i], s16[i]) + _dot(a_kk[i], v_bd[i])).astype(BF16))
    yield
    u16 = each(lambda i, b, g: _dot(tinv[i].astype(BF16), lane_bd(rhs[i])).astype(BF16))
    yield
    yc_ = each(lambda i, b, g: _dot_nt(rg[i], s16[i]) + _dot(a_rk[i], v_bd[i]) - _dot(a_rb[i], lane_bd(u16[i])))
    yield
    upd = each(lambda i, b, g: _dot_tn(jnp.concatenate([v16[i], -u16[i]], axis=0),
                                       jnp.concatenate([kt[b][:, ls(g)], bt[b][:, ls(g)]], axis=0)))
    for i, (b, g) in enumerate(chains):
        bds[b, g] = bds[b, g] * g_last[b][:, ls(g)] + upd[i] * smask
    yield WROTE_STATE
    ys = [jnp.concatenate(yc_[b * groups:(b + 1) * groups], axis=1) for b in range(nb)]
    y = jnp.concatenate(ys, axis=0)

    inv_d = 1.0 / A_HEAD_DIM
    y_sum = segsum(y.astype(BF16))

    def centre(y, y_sum):
        yc = y - y_sum * inv_d
        return yc, (yc * yc).astype(BF16)

    yc, yc_sq16 = _rowwise(centre, rows, y, y_sum)
    var_sum = segsum(yc_sq16)
    yield
    ln_w, ln_b = lnw_ref[...], lnb_ref[...]

    def finish(yc, var_sum, bonus_ss, v, gate):
        return ((yc * lax.rsqrt(var_sum * inv_d + GN_EPS) * ln_w + ln_b + bonus_ss * v) * gate,)

    out, = _rowwise(finish, rows, yc, var_sum, bonus_ss, v, gate)
    for b in range(nb):
        y_ref[b, win, 0:A_WIDTH] = out[b * L:(b + 1) * L]

    if last is not None:
        @pl.when(last)
        def _():
            for b in range(nb):
                for h in range(A_HEADS):
                    g, sl = diag(h)
                    s_out_ref[b, h] = bds[b, g, sl, sl]


def _mlstm_stages(pb_ref, pg_ref, conv0_ref, c0_ref, n0_ref, m0_ref, wts, consts,
                  y_ref, conv_out_ref, c_ref, n_ref, m_ref, xbuf, *, chunk, row0, first, zero_state):
    cw_ref, cb_ref, gb_ref, hnw_ref = wts
    win = slice(row0, row0 + chunk)
    L = chunk
    nb = pb_ref.shape[0]
    rows = nb * L
    hist = B_CONV - 1

    if first is not None:
        @pl.when(first)
        def _():
            xbuf[...] = jnp.zeros(xbuf.shape, F32)
            c_ref[...] = jnp.zeros(c_ref.shape, F32)
            n_ref[...] = jnp.zeros(n_ref.shape, F32)
            m_ref[...] = jnp.zeros(m_ref.shape, F32)

            def load():
                for b in range(nb):
                    xbuf[b, CARRY - hist:CARRY, :] = conv0_ref[b]
                c_ref[...] = c0_ref[...]
                n_ref[...] = n0_ref[...]
                m_ref[...] = m0_ref[...]
            _unless(zero_state, load)

    qk = []
    for b in range(nb):
        x = pb_ref[b, win, 0:2 * B_WIDTH]
        prev = xbuf[b]
        acc = cb_ref[...] + cw_ref[hist:hist + 1, :] * x
        for j in range(hist):
            acc = acc + cw_ref[j:j + 1, :] * _shift_rows(x, prev, hist - j)
        xbuf[b] = x[L - CARRY:L, :]
        conv_out_ref[b] = x[L - hist:L, :]
        qk.append(_silu(acc))
        yield

    gates = jnp.concatenate([pg_ref[b, win, :] for b in range(nb)], axis=0) + gb_ref[...]
    lane = lax.broadcasted_iota(jnp.int32, (rows, GATE_PAD), 1)
    tril = consts["tril"]()
    cum = sum(_dot(tril, part) for part in _split3(-_softplus(-gates)))
    gcols = jnp.where(lane < B_HEADS, gates, cum)
    grows = gcols.T
    causal = _tril(L, False)
    yield

    chains = [(b, h) for b in range(nb) for h in range(B_HEADS)]
    each = lambda f: [f(i, b, h) for i, (b, h) in enumerate(chains)]
    rs = lambda b: slice(b * L, (b + 1) * L)
    hs = lambda h, base=0: slice(base + h * B_HEAD_DIM, base + (h + 1) * B_HEAD_DIM)

    q = each(lambda i, b, h: qk[b][:, hs(h)])
    k = each(lambda i, b, h: qk[b][:, hs(h, B_WIDTH)] * (B_HEAD_DIM ** -0.5))
    q16 = each(lambda i, b, h: q[i].astype(BF16))
    v16 = each(lambda i, b, h: pb_ref[b, win, hs(h, 2 * B_WIDTH)].astype(BF16))
    li_col = each(lambda i, b, h: gcols[rs(b), h:h + 1])
    b_col = each(lambda i, b, h: gcols[rs(b), B_HEADS + h:B_HEADS + h + 1])
    li_row = each(lambda i, b, h: grows[h:h + 1, rs(b)])
    b_row = each(lambda i, b, h: grows[B_HEADS + h:B_HEADS + h + 1, rs(b)])
    yield READS_STATE
    c_prev = each(lambda i, b, h: c_ref[b, h])
    n_prev = each(lambda i, b, h: n_ref[b, h:h + 1, :])
    m_prev = each(lambda i, b, h: m_ref[b, h:h + 1, 0:1])

    dm = each(lambda i, b, h: jnp.where(causal, b_col[i] - b_row[i] + li_row[i], -jnp.inf))
    inter = each(lambda i, b, h: b_col[i] + m_prev[i])
    mt = each(lambda i, b, h: jnp.maximum(inter[i], jnp.max(dm[i], axis=-1, keepdims=True)))
    wo = each(lambda i, b, h: jnp.exp(inter[i] - mt[i]))
    yield
    s = each(lambda i, b, h: _dot_nt(q16[i], k[i].astype(BF16)) * jnp.exp(dm[i] - mt[i]))
    yield
    num = each(lambda i, b, h: wo[i] * _dot(q16[i], c_prev[i].astype(BF16)) + _dot(s[i].astype(BF16), v16[i]))
    yield
    den = each(lambda i, b, h: wo[i] * jnp.sum(q[i] * n_prev[i], axis=-1, keepdims=True)
               + jnp.sum(s[i], axis=-1, keepdims=True))
    hh = each(lambda i, b, h: num[i] * (1.0 / jnp.maximum(jnp.abs(den[i]), jnp.exp(-mt[i]))))
    yield

    m_new = each(lambda i, b, h: mt[i][L - 1:L, :])
    b_last = each(lambda i, b, h: b_col[i][L - 1:L, :])
    kw = each(lambda i, b, h: k[i] * jnp.exp(b_last[i] - b_col[i] + li_col[i] - m_new[i]))
    dec = each(lambda i, b, h: jnp.exp(b_last[i] + m_prev[i] - m_new[i]))
    yield
    c_new = each(lambda i, b, h: dec[i] * c_prev[i] + _dot_tn(kw[i].astype(BF16), v16[i]))
    n_new = each(lambda i, b, h: dec[i] * n_prev[i] + jnp.sum(kw[i], axis=0, keepdims=True))
    yield
    out = each(lambda i, b, h: hh[i] * lax.rsqrt(jnp.mean(hh[i] * hh[i], axis=-1, keepdims=True) + RMS_EPS)
               * hnw_ref[:, hs(h)] * _sigmoid(pb_ref[b, win, hs(h, 3 * B_WIDTH)]))
    for i, (b, h) in enumerate(chains):
        c_ref[b, h] = c_new[i]
        n_ref[b, h:h + 1, :] = n_new[i]
        m_ref[b, h:h + 1, :] = jnp.broadcast_to(m_new[i], (1, B_HEAD_DIM))
        y_ref[b, win, hs(h, A_WIDTH)] = out[i]


PROLOGUE_DONE, READS_STATE, WROTE_STATE, DONE = "prologue_done", "reads_state", "wrote_state", "done"
N_RWKV_W = 11
N_MLSTM_W = 4
N_CONSTS = 2


def _mix_kernel(*refs, chunk, hg, where, zero_first):
    it = iter(refs)
    take = lambda n: [next(it) for _ in range(n)]
    pa_ref, pb_ref, pg_ref, shift_ref, s0_ref, conv0_ref, c0_ref, n0_ref, m0_ref = take(9)
    rwkv_w, mlstm_w, slabs = take(N_RWKV_W), take(N_MLSTM_W), take(N_CONSTS)
    consts = {name: functools.partial(lambda ref, r0, rows, cols: ref[r0:r0 + rows, 0:cols], ref, *at)
              for ref, masks in zip(slabs, where) for name, at in masks}
    y_ref, shift_out_ref, s_out_ref, conv_out_ref, c_ref, n_ref, m_ref, pbuf, bds, xbuf = take(10)
    first = pl.program_id(1) == 0
    last = pl.program_id(1) == pl.num_programs(1) - 1
    zero_state = pl.program_id(0) == 0 if zero_first else None
    subs = pa_ref.shape[1] // chunk
    bodies = []
    for sub in range(subs):
        is_first = first if sub == 0 else None
        bodies.append([
            _rwkv_stages(pa_ref, shift_ref, s0_ref, rwkv_w, consts, y_ref, shift_out_ref, s_out_ref, pbuf, bds,
                         chunk=chunk, hg=hg, row0=sub * chunk, first=is_first,
                         last=last if sub == subs - 1 else None, zero_state=zero_state),
            _mlstm_stages(pb_ref, pg_ref, conv0_ref, c0_ref, n0_ref, m0_ref, mlstm_w, consts,
                          y_ref, conv_out_ref, c_ref, n_ref, m_ref, xbuf, chunk=chunk, row0=sub * chunk,
                          first=is_first, zero_state=zero_state),
        ])
    live = [(sub, kind) for sub in range(subs) for kind in range(2)]
    wrote, waiting, started = set(), set(), {0}
    while live:
        for key in list(live):
            sub, kind = key
            if sub not in started or (key in waiting and (sub - 1, kind) not in wrote):
                continue
            waiting.discard(key)
            mark = next(bodies[sub][kind], DONE)
            if mark == PROLOGUE_DONE:
                started.add(sub + 1)
            elif mark == READS_STATE and sub > 0:
                waiting.add(key)
            elif mark in (WROTE_STATE, DONE):
                wrote.add(key)
                if mark == DONE:
                    live.remove(key)


def _mix(pa, pb, pg, l, sl, states, rwkv_w, mlstm_w, nb, chunk, subs, zero_first):
    shift0, s0, conv0, c0, n0, m0 = states
    bn, t, _ = pa.shape
    hg = _rwkv_heads_per_group(chunk)
    hist = B_CONV - 1
    consts, where = _mix_consts(chunk, nb)
    full = lambda a: pl.BlockSpec(a.shape, lambda i, c: (0,) * a.ndim)
    tok = lambda n: pl.BlockSpec((nb, subs * chunk, n), lambda i, c: (i, c, 0))
    per_stream = lambda *dims: pl.BlockSpec((nb,) + dims, lambda i, c: (i,) + (0,) * len(dims))
    state_shapes = [(1, A_COLS), (A_HEADS, A_HEAD_DIM, A_HEAD_DIM), (hist, 2 * B_WIDTH),
                    (B_HEADS, B_HEAD_DIM, B_HEAD_DIM), (B_HEADS, B_HEAD_DIM), (B_HEADS, B_HEAD_DIM)]
    return pl.pallas_call(
        functools.partial(_mix_kernel, chunk=chunk, hg=hg, where=where, zero_first=zero_first),
        grid=(bn // nb, t // (subs * chunk)),
        in_specs=[tok(A_COLS), tok(B_MAIN), tok(GATE_PAD)]
        + [_state_spec(a, sl, nb, zero_first) for a in states]
        + [_layer_spec(a, l) for a in rwkv_w + mlstm_w] + [full(a) for a in consts],
        out_specs=[tok(D_MIX)] + [per_stream(*dims) for dims in state_shapes],
        out_shape=[jax.ShapeDtypeStruct((bn, t, D_MIX), F32)]
        + [jax.ShapeDtypeStruct((bn,) + dims, F32) for dims in state_shapes],
        scratch_shapes=[
            pltpu.VMEM((nb, CARRY, A_COLS), F32),
            pltpu.VMEM((nb, A_HEADS // hg, hg * A_HEAD_DIM, hg * A_HEAD_DIM), F32),
            pltpu.VMEM((nb, CARRY, 2 * B_WIDTH), F32),
        ],
        compiler_params=pltpu.CompilerParams(
            dimension_semantics=("arbitrary", "arbitrary"), vmem_limit_bytes=VMEM_LIMIT),
        name="mixers",
    )(pa, pb, pg, *states, *rwkv_w, *mlstm_w, *consts)


def _ffn_kernel(x_ref, y_ref, f0_ref, wout_ref, g_ref, wup_ref, cw_ref, cb_ref,
                wdown_ref, gfin_ref, o_ref, fout_ref, ubuf, *, tile, final, zero_streams):
    nb = x_ref.shape[0]
    hist = FFN_CONV - 1
    rows_of = lambda ref: ref[0] if nb == 1 else jnp.concatenate([ref[b] for b in range(nb)], axis=0)

    @pl.when(pl.program_id(1) == 0)
    def _():
        ubuf[...] = jnp.zeros(ubuf.shape, F32)
        for b in range(zero_streams, nb):
            ubuf[b, CARRY - hist:CARRY, :] = f0_ref[b - zero_streams]

    x1 = rows_of(x_ref) + _dot(rows_of(y_ref).astype(BF16), wout_ref[...])
    h2 = _rms(x1, g_ref[...]).astype(BF16)
    acts = []
    for lo, hi in zip(FF_BOUNDS[:-1], FF_BOUNDS[1:]):
        cols = slice(lo, hi)
        u = _dot(h2, wup_ref[:, cols])
        gate = _dot(h2, wup_ref[:, D_FF + lo:D_FF + hi])
        convs = []
        for b in range(nb):
            ub = u[b * tile:(b + 1) * tile]
            prev = ubuf[b, :, cols]
            acc = cb_ref[:, cols] + cw_ref[hist:hist + 1, cols] * ub
            for j in range(hist):
                acc = acc + cw_ref[j:j + 1, cols] * _shift_rows(ub, prev, hist - j)
            ubuf[b, :, cols] = ub[tile - CARRY:tile]
            fout_ref[b, :, cols] = ub[tile - hist:tile]
            convs.append(acc)
        conv = convs[0] if nb == 1 else jnp.concatenate(convs, axis=0)
        acts.append((_silu(conv) * gate).astype(BF16))
    x2 = x1 + _dot(jnp.concatenate(acts, axis=1), wdown_ref[...])
    out = _rms(x2, gfin_ref[...]) if final else x2
    for b in range(nb):
        o_ref[b] = out[b * tile:(b + 1) * tile]


def _ffn(x, y, l, sl, f0, wts, gfin, nb, tile, final, zero_streams):
    bn, t, _ = x.shape
    hist = FFN_CONV - 1
    tok = pl.BlockSpec((nb, tile, D_MODEL), lambda i, c: (i, c, 0))
    fspec = pl.BlockSpec((nb, hist, D_FF), lambda i, c: (i, 0, 0))
    return pl.pallas_call(
        functools.partial(_ffn_kernel, tile=tile, final=final, zero_streams=zero_streams),
        grid=(bn // nb, t // tile),
        in_specs=[tok, tok, _state_spec(f0, sl, nb - zero_streams)]
        + [_layer_spec(a, l, resident=True) for a in wts] + [_layer_spec(gfin, 0, resident=True)],
        out_specs=[tok, fspec],
        out_shape=[
            jax.ShapeDtypeStruct((bn, t, D_MODEL), F32),
            jax.ShapeDtypeStruct((bn, hist, D_FF), F32),
        ],
        scratch_shapes=[pltpu.VMEM((nb, CARRY, D_FF), F32)],
        compiler_params=pltpu.CompilerParams(
            dimension_semantics=("arbitrary", "arbitrary"), vmem_limit_bytes=VMEM_LIMIT),
        name="out_ffn",
    )(x, y, f0, *wts, gfin)


def _layer(x, l, sl, st, w, *, in_tile, mix_streams, mix_chunk, mix_subs, ffn_streams, ffn_tile, final,
           zero_streams):
    bn, t, _ = x.shape
    pa, pb, pg = _in_proj(x.reshape(bn * t, D_MODEL), l, w["norm_mix"], w["w_in"], w["w_gate"], in_tile)
    pa = pa.reshape(bn, t, A_COLS)
    pb = pb.reshape(bn, t, B_MAIN)
    pg = pg.reshape(bn, t, GATE_PAD)
    y, *mix_states = _mix(pa, pb, pg, l, sl, st[:6], w["rwkv"], w["mlstm"], mix_streams, mix_chunk,
                          mix_subs, zero_streams > 0)
    x, fconv1 = _ffn(x, y, l, sl, st[6], w["ffn"], w["norm_final"], ffn_streams, ffn_tile, final, zero_streams)
    return x, tuple(mix_states) + (fconv1,)


def kernel(x_prompt, x_sample, state_rwkv_shift, state_rwkv_wkv, state_mlstm_conv, state_mlstm_C,
           state_mlstm_n, state_mlstm_m, state_ffn_conv, meta_tokens, norm_mix, w_in, a_mu, a_w0, a_w2,
           a_a0, a_a2, a_g2, a_k_k, a_k_a, a_r_k, a_ln_w, a_ln_b, b_conv_w, b_conv_b, b_i_bias, b_f_bias,
           b_hn_w, w_out, norm_ffn, w_up, ffn_conv_w, ffn_conv_b, w_down, norm_final):
    n_prompt = x_prompt.shape[0]
    n_sample = x_sample.shape[0]
    n_lead = n_prompt + n_sample
    assert x_sample.shape[1] == N_META
    assert n_prompt == LEAD_STREAMS and n_lead % LEAD_STREAMS == 0 and n_prompt % MAIN_STREAMS == 0

    bf = lambda a: a.astype(BF16)
    vec = lambda a: a[:, None, :]
    n_gate = 2 * B_HEADS
    gate_pad = ((0, 0), (0, GATE_PAD - n_gate))
    w = {
        "norm_mix": vec(norm_mix),
        "w_in": bf(w_in),
        "w_gate": bf(jnp.pad(w_in[:, :, A_COLS + B_MAIN:], ((0, 0),) + gate_pad)),
        "rwkv": [vec(a_mu), vec(a_w0), bf(a_w2), vec(a_a0), bf(a_a2), bf(a_g2), vec(a_k_k), vec(a_k_a),
                 vec(a_r_k), vec(a_ln_w), vec(a_ln_b)],
        "mlstm": [b_conv_w, vec(b_conv_b), vec(jnp.pad(jnp.concatenate([b_i_bias, b_f_bias], axis=1), gate_pad)),
                  vec(b_hn_w)],
        "ffn": (bf(w_out), vec(norm_ffn), bf(w_up), ffn_conv_w, vec(ffn_conv_b), bf(w_down)),
        "norm_final": norm_final.reshape(1, 1, D_MODEL),
    }
    assert len(w["rwkv"]) == N_RWKV_W and len(w["mlstm"]) == N_MLSTM_W

    st_lead_in = (
        state_rwkv_shift[:, :, None, :], state_rwkv_wkv, state_mlstm_conv, state_mlstm_C, state_mlstm_n,
        jnp.broadcast_to(state_mlstm_m[..., None], state_mlstm_m.shape + (B_HEAD_DIM,)), state_ffn_conv,
    )
    meta = jnp.broadcast_to(meta_tokens[None].astype(x_prompt.dtype), (n_prompt, N_META, D_MODEL))
    x_lead = jnp.concatenate([meta, x_sample], axis=0)
    x_main = x_prompt

    main_states, lead_states = [], []
    for l in range(DEPTH):
        final = l == DEPTH - 1
        x_lead, st_lead = _layer(x_lead, l, l, st_lead_in, w, in_tile=n_lead * N_META,
                                 mix_streams=LEAD_STREAMS, mix_chunk=N_META,
                                 mix_subs=1, ffn_streams=n_lead, ffn_tile=N_META, final=final,
                                 zero_streams=n_prompt)
        x_main, st_main = _layer(x_main, l, 0, tuple(s[None] for s in st_lead), w, in_tile=MAIN_PROJ_TILE,
                                 mix_streams=MAIN_STREAMS, mix_chunk=MAIN_CHUNK, mix_subs=MAIN_SUBCHUNKS,
                                 ffn_streams=1, ffn_tile=MAIN_TILE, final=final, zero_streams=0)
        main_states.append(st_main)
        lead_states.append(st_lead)

    def collect(per_layer, first):
        shift, wkv, bconv, c, n, m, fconv = (jnp.stack([st[i] for st in per_layer])[:, first:] for i in range(7))
        return (shift[:, :, 0, :], wkv, bconv, c, n, m[..., 0], fconv)

    return (x_main, x_lead[n_prompt:]) + collect(main_states, 0) + collect(lead_states, n_prompt)
```

```python
import functools
import math

import numpy as np

import jax
import jax.numpy as jnp
from jax import lax
from jax.experimental import pallas as pl
from jax.experimental.pallas import tpu as pltpu

D_MODEL = 1024
DEPTH = 2
N_META = 16
A_HEADS = 8
A_HEAD_DIM = 64
A_WIDTH = 512
A_DECAY_LORA = 64
A_AAA_LORA = 64
A_GATE_LORA = 128
A_COLS = 1792
B_HEADS = 4
B_HEAD_DIM = 128
B_WIDTH = 512
B_CONV = 4
B_MAIN = 4 * B_WIDTH
GATE_PAD = 128
D_MIX = A_WIDTH + B_WIDTH
D_FF = 2816
FFN_CONV = 3
RMS_EPS = 1e-6
GN_EPS = 64e-5
CARRY = 8
ROW_BLOCK = 16
LANES = 128
MXU_DIM = 256
MIX_STREAMS = 4
FF_BOUNDS = (0, 6 * MXU_DIM, D_FF)
MAIN_TILE = 512
MAIN_PROJ_TILE = 1024
MAIN_CHUNK = 64
MAIN_SUBCHUNKS = 2
VMEM_LIMIT = 56 * 1024 * 1024

F32 = jnp.float32
BF16 = jnp.bfloat16
NT_DIMS = (((1,), (1,)), ((), ()))
TN_DIMS = (((0,), (0,)), ((), ()))


def _dot(a, b):
    return jnp.dot(a, b, preferred_element_type=F32)


def _dot_nt(a, b):
    return lax.dot_general(a, b, NT_DIMS, preferred_element_type=F32)


def _dot_tn(a, b):
    return lax.dot_general(a, b, TN_DIMS, preferred_element_type=F32)


def _sigmoid(x):
    return 0.5 * jnp.tanh(0.5 * x) + 0.5


def _silu(x):
    half = 0.5 * x
    return half + half * jnp.tanh(half)


def _softplus(x):
    return jnp.maximum(x, 0.0) + jnp.log(1.0 + jnp.exp(-jnp.abs(x)))


def _rms(x, g):
    return x * lax.rsqrt(jnp.mean(x * x, axis=-1, keepdims=True) + RMS_EPS) * g


def _tril(n, strict):
    row = lax.broadcasted_iota(jnp.int32, (n, n), 0)
    col = lax.broadcasted_iota(jnp.int32, (n, n), 1)
    return (col < row) if strict else (col <= row)


def _split3(x):
    hi = x.astype(BF16)
    rest = x - hi.astype(F32)
    mid = rest.astype(BF16)
    lo = (rest - mid.astype(F32)).astype(BF16)
    return hi, mid, lo


def _shift_rows(x, prev, s):
    rolled = pltpu.roll(x, s, 0)
    from_prev = lax.broadcasted_iota(jnp.int32, prev.shape, 0) < s
    head = jnp.where(from_prev, pltpu.roll(prev, s, 0), rolled[0:CARRY])
    return jnp.concatenate([head, rolled[CARRY:]], axis=0)


def _rowwise(fn, n_rows, *xs):
    pieces = [fn(*[x[i:i + ROW_BLOCK] for x in xs]) for i in range(0, n_rows, ROW_BLOCK)]
    return [jnp.concatenate(col, axis=0) for col in zip(*pieces)]


def _layer_spec(a, l, resident=False):
    index = lambda *grid: (l,) + (0,) * (a.ndim - 1)
    if resident:
        return pl.BlockSpec((None,) + a.shape[1:], index, pipeline_mode=pl.Buffered(1))
    return pl.BlockSpec((None,) + a.shape[1:], index)


def _state_spec(a, l, nb, skip_first=False):
    block = (lambda i: jnp.maximum(i - 1, 0)) if skip_first else (lambda i: i)
    return pl.BlockSpec((None, nb) + a.shape[2:], lambda i, c: (l, block(i)) + (0,) * (a.ndim - 2))


def _unless(zero_state, load):
    if zero_state is None:
        load()
    else:
        pl.when(jnp.logical_not(zero_state))(load)


def _in_proj_kernel(x_ref, g_ref, w_ref, wg_ref, oa_ref, ob_ref, og_ref):
    h = _rms(x_ref[...], g_ref[...]).astype(BF16)
    oa_ref[...] = _dot(h, w_ref[:, 0:A_COLS])
    ob_ref[...] = _dot(h, w_ref[:, A_COLS:A_COLS + B_MAIN])
    og_ref[...] = _dot(h, wg_ref[...])


def _in_proj(x2d, l, g, w, wg, tm):
    m = x2d.shape[0]
    return pl.pallas_call(
        _in_proj_kernel,
        grid=(m // tm,),
        in_specs=[
            pl.BlockSpec((tm, D_MODEL), lambda i: (i, 0)),
            _layer_spec(g, l), _layer_spec(w, l, resident=True), _layer_spec(wg, l, resident=True),
        ],
        out_specs=[
            pl.BlockSpec((tm, A_COLS), lambda i: (i, 0)),
            pl.BlockSpec((tm, B_MAIN), lambda i: (i, 0)),
            pl.BlockSpec((tm, GATE_PAD), lambda i: (i, 0)),
        ],
        out_shape=[
            jax.ShapeDtypeStruct((m, A_COLS), F32),
            jax.ShapeDtypeStruct((m, B_MAIN), F32),
            jax.ShapeDtypeStruct((m, GATE_PAD), F32),
        ],
        compiler_params=pltpu.CompilerParams(
            dimension_semantics=("arbitrary",), vmem_limit_bytes=VMEM_LIMIT),
        name="in_proj",
    )(x2d, g, w, wg)


def _rwkv_heads_per_group(chunk):
    return min(A_HEADS, MXU_DIM // chunk)


def _mix_consts(chunk, streams):
    hg = _rwkv_heads_per_group(chunk)
    hl, w, rows = hg * chunk, hg * A_HEAD_DIM, streams * chunk
    ix = lambda n: (np.arange(n)[:, None], np.arange(n)[None, :])
    r, c = ix(MXU_DIM)
    seg = r // A_HEAD_DIM == c // A_HEAD_DIM
    r, c = ix(rows)
    tril = (r // chunk == c // chunk) & (c <= r)
    r, c = np.arange(2 * chunk)[:, None], np.arange(hl)[None, :]
    amask = np.where(r < chunk, c % chunk < r, c % chunk <= r - chunk)
    r, c = np.arange(chunk)[:, None], np.arange(hl)[None, :]
    eye = c % chunk == r
    r, c = np.arange(hl)[:, None], np.arange(w)[None, :]
    bdl = r // chunk == c // A_HEAD_DIM
    r, c = ix(hl)
    bdsq = r // chunk == c // chunk
    r, c = ix(w)
    smask = r // A_HEAD_DIM == c // A_HEAD_DIM

    def pack(dtype, **masks):
        width = max(m.shape[1] for m in masks.values())
        rows = np.cumsum([0] + [m.shape[0] for m in masks.values()])
        slab = np.concatenate([np.pad(m, ((0, 0), (0, width - m.shape[1]))) for m in masks.values()], axis=0)
        return jnp.asarray(slab, dtype), tuple((name, (int(r0),) + m.shape) for (name, m), r0 in zip(masks.items(), rows))

    slab16, where16 = pack(BF16, seg=seg, tril=tril, bdl=bdl, bdsq=bdsq)
    slab32, where32 = pack(F32, amask=amask, eye=eye, smask=smask)
    return (slab16, slab32), (where16, where32)


def _rwkv_stages(p_ref, shift_ref, s0_ref, wts, consts, y_ref, shift_out_ref, s_out_ref, pbuf, bds,
                 *, chunk, hg, row0, first, last, zero_state):
    mu_ref, w0_ref, w2_ref, a0_ref, a2_ref, g2_ref, kk_ref, ka_ref, rk_ref, lnw_ref, lnb_ref = wts
    L = chunk
    nb = p_ref.shape[0]
    rows = nb * L
    hl = hg * L
    wid = hg * A_HEAD_DIM
    groups = A_HEADS // hg
    diag = lambda h: (h // hg, slice((h % hg) * A_HEAD_DIM, (h % hg + 1) * A_HEAD_DIM))
    win = slice(row0, row0 + L)

    if first is not None:
        @pl.when(first)
        def _():
            bds[...] = jnp.zeros(bds.shape, F32)
            pbuf[...] = jnp.zeros(pbuf.shape, F32)

            def load():
                for b in range(nb):
                    pbuf[b, CARRY - 1:CARRY, :] = shift_ref[b]
                    for h in range(A_HEADS):
                        g, sl = diag(h)
                        bds[b, g, sl, sl] = s0_ref[b, h]
            _unless(zero_state, load)

    ps, prevs = [], []
    for b in range(nb):
        pb = p_ref[b, win, :]
        prevs.append(_shift_rows(pb, pbuf[b], 1))
        pbuf[b] = pb[L - CARRY:L, :]
        shift_out_ref[b] = pb[L - 1:L, :]
        ps.append(pb)
    i1, i2, i3 = A_WIDTH, 2 * A_WIDTH, 3 * A_WIDTH
    i4, i5 = i3 + A_DECAY_LORA, i3 + A_DECAY_LORA + A_AAA_LORA
    mu = mu_ref[...]

    def token_shift(p, prev):
        pm = p + (prev - p) * mu
        return (pm[:, :i1], pm[:, i1:i2], pm[:, i2:i3], jnp.tanh(pm[:, i3:i4]).astype(BF16),
                pm[:, i4:i5].astype(BF16), _sigmoid(pm[:, i5:]).astype(BF16))

    r, k, v, wl16, al16, gl16 = _rowwise(token_shift, rows, jnp.concatenate(ps, axis=0), jnp.concatenate(prevs, axis=0))
    yield
    seg = consts["seg"]()
    half = A_WIDTH // 2
    segsum = lambda t16: jnp.concatenate([_dot(t16[:, :half], seg), _dot(t16[:, half:], seg)], axis=1)
    z = w0_ref[...] + _dot(wl16, w2_ref[...])
    a_pre = a0_ref[...] + _dot(al16, a2_ref[...])
    gate = _dot(gl16, g2_ref[...])
    yield
    k_k, k_a, r_k = kk_ref[...], ka_ref[...], rk_ref[...]

    def decay_and_keys(k, z, a_pre):
        lw = -math.exp(-0.5) * _sigmoid(z)
        lw_hi = lw.astype(BF16)
        a = _sigmoid(a_pre)
        kk0 = k * k_k
        return (lw, lw_hi, (lw - lw_hi.astype(F32)).astype(BF16), a, kk0, (kk0 * kk0).astype(BF16),
                k * (1.0 + (a - 1.0) * k_a))

    lw, lw_hi, lw_lo, a, kk0, kk_sq16, kmod = _rowwise(decay_and_keys, rows, k, z, a_pre)
    yield
    tril = consts["tril"]()
    cum = _dot(tril, lw_hi) + _dot(tril, lw_lo)
    kk_ss = segsum(kk_sq16)
    yield

    rs = lambda b: slice(b * L, (b + 1) * L)
    cum_last = [cum[(b + 1) * L - 1:(b + 1) * L] for b in range(nb)]
    g_last = [jnp.exp(cum_last[b]) for b in range(nb)]

    def scaled_operands(b):
        def fn(r, v, kk0, kk_ss, a, kmod, lw, cum):
            kk = kk0 * lax.rsqrt(kk_ss + 1e-12)
            bvec = kk * a
            g_inv = jnp.exp(-cum)
            g_tail = jnp.exp(cum_last[b] - cum)
            return ((kk * jnp.exp(cum - lw)).astype(BF16), (r * jnp.exp(cum)).astype(BF16),
                    (kmod * g_inv).astype(BF16), (bvec * g_inv).astype(BF16), v.astype(BF16),
                    (kmod * g_tail).astype(BF16), (bvec * g_tail).astype(BF16), (r * kmod * r_k).astype(BF16))
        return fn

    per_stream = [_rowwise(scaled_operands(b), L, *[t[rs(b)] for t in (r, v, kk0, kk_ss, a, kmod, lw, cum)])
                  for b in range(nb)]
    kkg_s, rg_s, kd_s, bd_s, v_s, kt, bt, bonus_s = zip(*per_stream)
    bonus_ss = segsum(jnp.concatenate(bonus_s, axis=0))
    yield PROLOGUE_DONE

    keep = consts["amask"]() > 0.5
    eye = consts["eye"]()
    bdl = consts["bdl"]()
    bdsq = consts["bdsq"]()
    smask = consts["smask"]()
    def block_diag(t, mask, seg):
        per_tile = max(1, LANES // seg)
        tile = per_tile * seg
        rows = []
        for h in range(hg):
            cols = [t[:, j:j + tile] * mask[h * L:(h + 1) * L, j:j + tile] if j // tile == h // per_tile
                    else jnp.zeros((L, tile), t.dtype) for j in range(0, t.shape[1], tile)]
            rows.append(cols[0] if len(cols) == 1 else jnp.concatenate(cols, axis=1))
        return jnp.concatenate(rows, axis=0)

    lane_bd = lambda t: block_diag(t, bdl, A_HEAD_DIM)
    sq_bd = lambda t: block_diag(t, bdsq, L)

    chains = [(b, g) for b in range(nb) for g in range(groups)]
    ls = lambda g: slice(g * wid, (g + 1) * wid)
    each = lambda f: [f(i, b, g) for i, (b, g) in enumerate(chains)]

    kkg = each(lambda i, b, g: kkg_s[b][:, ls(g)])
    rg = each(lambda i, b, g: rg_s[b][:, ls(g)])
    v16 = each(lambda i, b, g: v_s[b][:, ls(g)])
    lhs = each(lambda i, b, g: jnp.concatenate([kkg[i], rg[i]], axis=0))
    a_k = each(lambda i, b, g: jnp.where(keep, _dot_nt(lhs[i], lane_bd(kd_s[b][:, ls(g)])), 0.0))
    yield
    a_b = each(lambda i, b, g: jnp.where(keep, _dot_nt(lhs[i], lane_bd(bd_s[b][:, ls(g)])), 0.0))
    a_kk = each(lambda i, b, g: a_k[i][:L].astype(BF16))
    a_rk = each(lambda i, b, g: a_k[i][L:].astype(BF16))
    a_rb = each(lambda i, b, g: a_b[i][L:].astype(BF16))
    yield

    npow = each(lambda i, b, g: -a_b[i][:L])
    tinv = each(lambda i, b, g: eye + npow[i])
    n16 = each(lambda i, b, g: npow[i].astype(BF16))
    npow = each(lambda i, b, g: _dot(n16[i], sq_bd(n16[i])))
    yield
    span = 2
    while span < L:
        n16 = each(lambda i, b, g: npow[i].astype(BF16))
        t_bd = each(lambda i, b, g: sq_bd(tinv[i].astype(BF16)))
        if 2 * span < L:
            both = each(lambda i, b, g: _dot(n16[i], jnp.concatenate([sq_bd(n16[i]), t_bd[i]], axis=1)))
            npow = each(lambda i, b, g: both[i][:, :hl])
            prod = each(lambda i, b, g: both[i][:, hl:])
        else:
            prod = each(lambda i, b, g: _dot(n16[i], t_bd[i]))
        tinv = each(lambda i, b, g: tinv[i] + prod[i])
        span *= 2
        yield

    yield READS_STATE
    s16 = each(lambda i, b, g: bds[b, g].astype(BF16))
    v_bd = each(lambda i, b, g: lane_bd(v16[i]))
    rhs = each(lambda i, b, g: (_dot_nt(kkg[i], s16[i]) + _dot(a_kk[i], v_bd[i])).astype(BF16))
    yield
    u16 = each(lambda i, b, g: _dot(tinv[i].astype(BF16), lane_bd(rhs[i])).astype(BF16))
    yield
    yc_ = each(lambda i, b, g: _dot_nt(rg[i], s16[i]) + _dot(a_rk[i], v_bd[i]) - _dot(a_rb[i], lane_bd(u16[i])))
    yield
    upd = each(lambda i, b, g: _dot_tn(jnp.concatenate([v16[i], -u16[i]], axis=0),
                                       jnp.concatenate([kt[b][:, ls(g)], bt[b][:, ls(g)]], axis=0)))
    for i, (b, g) in enumerate(chains):
        bds[b, g] = bds[b, g] * g_last[b][:, ls(g)] + upd[i] * smask
    yield WROTE_STATE
    ys = [jnp.concatenate(yc_[b * groups:(b + 1) * groups], axis=1) for b in range(nb)]
    y = jnp.concatenate(ys, axis=0)

    inv_d = 1.0 / A_HEAD_DIM
    y_sum = segsum(y.astype(BF16))

    def centre(y, y_sum):
        yc = y - y_sum * inv_d
        return yc, (yc * yc).astype(BF16)

    yc, yc_sq16 = _rowwise(centre, rows, y, y_sum)
    var_sum = segsum(yc_sq16)
    yield
    ln_w, ln_b = lnw_ref[...], lnb_ref[...]

    def finish(yc, var_sum, bonus_ss, v, gate):
        return ((yc * lax.rsqrt(var_sum * inv_d + GN_EPS) * ln_w + ln_b + bonus_ss * v) * gate,)

    out, = _rowwise(finish, rows, yc, var_sum, bonus_ss, v, gate)
    for b in range(nb):
        y_ref[b, win, 0:A_WIDTH] = out[b * L:(b + 1) * L]

    if last is not None:
        @pl.when(last)
        def _():
            for b in range(nb):
                for h in range(A_HEADS):
                    g, sl = diag(h)
                    s_out_ref[b, h] = bds[b, g, sl, sl]


def _mlstm_stages(pb_ref, pg_ref, conv0_ref, c0_ref, n0_ref, m0_ref, wts, consts,
                  y_ref, conv_out_ref, c_ref, n_ref, m_ref, xbuf, *, chunk, row0, first, zero_state):
    cw_ref, cb_ref, gb_ref, hnw_ref = wts
    win = slice(row0, row0 + chunk)
    L = chunk
    nb = pb_ref.shape[0]
    rows = nb * L
    hist = B_CONV - 1

    if first is not None:
        @pl.when(first)
        def _():
            xbuf[...] = jnp.zeros(xbuf.shape, F32)
            c_ref[...] = jnp.zeros(c_ref.shape, F32)
            n_ref[...] = jnp.zeros(n_ref.shape, F32)
            m_ref[...] = jnp.zeros(m_ref.shape, F32)

            def load():
                for b in range(nb):
                    xbuf[b, CARRY - hist:CARRY, :] = conv0_ref[b]
                c_ref[...] = c0_ref[...]
                n_ref[...] = n0_ref[...]
                m_ref[...] = m0_ref[...]
            _unless(zero_state, load)

    qk = []
    for b in range(nb):
        x = pb_ref[b, win, 0:2 * B_WIDTH]
        prev = xbuf[b]
        acc = cb_ref[...] + cw_ref[hist:hist + 1, :] * x
        for j in range(hist):
            acc = acc + cw_ref[j:j + 1, :] * _shift_rows(x, prev, hist - j)
        xbuf[b] = x[L - CARRY:L, :]
        conv_out_ref[b] = x[L - hist:L, :]
        qk.append(_silu(acc))
        yield

    gates = jnp.concatenate([pg_ref[b, win, :] for b in range(nb)], axis=0) + gb_ref[...]
    lane = lax.broadcasted_iota(jnp.int32, (rows, GATE_PAD), 1)
    tril = consts["tril"]()
    cum = sum(_dot(tril, part) for part in _split3(-_softplus(-gates)))
    gcols = jnp.where(lane < B_HEADS, gates, cum)
    grows = gcols.T
    causal = _tril(L, False)
    yield

    chains = [(b, h) for b in range(nb) for h in range(B_HEADS)]
    each = lambda f: [f(i, b, h) for i, (b, h) in enumerate(chains)]
    rs = lambda b: slice(b * L, (b + 1) * L)
    hs = lambda h, base=0: slice(base + h * B_HEAD_DIM, base + (h + 1) * B_HEAD_DIM)

    q = each(lambda i, b, h: qk[b][:, hs(h)])
    k = each(lambda i, b, h: qk[b][:, hs(h, B_WIDTH)] * (B_HEAD_DIM ** -0.5))
    q16 = each(lambda i, b, h: q[i].astype(BF16))
    v16 = each(lambda i, b, h: pb_ref[b, win, hs(h, 2 * B_WIDTH)].astype(BF16))
    li_col = each(lambda i, b, h: gcols[rs(b), h:h + 1])
    b_col = each(lambda i, b, h: gcols[rs(b), B_HEADS + h:B_HEADS + h + 1])
    li_row = each(lambda i, b, h: grows[h:h + 1, rs(b)])
    b_row = each(lambda i, b, h: grows[B_HEADS + h:B_HEADS + h + 1, rs(b)])
    yield READS_STATE
    c_prev = each(lambda i, b, h: c_ref[b, h])
    n_prev = each(lambda i, b, h: n_ref[b, h:h + 1, :])
    m_prev = each(lambda i, b, h: m_ref[b, h:h + 1, 0:1])

    dm = each(lambda i, b, h: jnp.where(causal, b_col[i] - b_row[i] + li_row[i], -jnp.inf))
    inter = each(lambda i, b, h: b_col[i] + m_prev[i])
    mt = each(lambda i, b, h: jnp.maximum(inter[i], jnp.max(dm[i], axis=-1, keepdims=True)))
    wo = each(lambda i, b, h: jnp.exp(inter[i] - mt[i]))
    yield
    s = each(lambda i, b, h: _dot_nt(q16[i], k[i].astype(BF16)) * jnp.exp(dm[i] - mt[i]))
    yield
    num = each(lambda i, b, h: wo[i] * _dot(q16[i], c_prev[i].astype(BF16)) + _dot(s[i].astype(BF16), v16[i]))
    yield
    den = each(lambda i, b, h: wo[i] * jnp.sum(q[i] * n_prev[i], axis=-1, keepdims=True)
               + jnp.sum(s[i], axis=-1, keepdims=True))
    hh = each(lambda i, b, h: num[i] * (1.0 / jnp.maximum(jnp.abs(den[i]), jnp.exp(-mt[i]))))
    yield

    m_new = each(lambda i, b, h: mt[i][L - 1:L, :])
    b_last = each(lambda i, b, h: b_col[i][L - 1:L, :])
    kw = each(lambda i, b, h: k[i] * jnp.exp(b_last[i] - b_col[i] + li_col[i] - m_new[i]))
    dec = each(lambda i, b, h: jnp.exp(b_last[i] + m_prev[i] - m_new[i]))
    yield
    c_new = each(lambda i, b, h: dec[i] * c_prev[i] + _dot_tn(kw[i].astype(BF16), v16[i]))
    n_new = each(lambda i, b, h: dec[i] * n_prev[i] + jnp.sum(kw[i], axis=0, keepdims=True))
    yield
    out = each(lambda i, b, h: hh[i] * lax.rsqrt(jnp.mean(hh[i] * hh[i], axis=-1, keepdims=True) + RMS_EPS)
               * hnw_ref[:, hs(h)] * _sigmoid(pb_ref[b, win, hs(h, 3 * B_WIDTH)]))
    for i, (b, h) in enumerate(chains):
        c_ref[b, h] = c_new[i]
        n_ref[b, h:h + 1, :] = n_new[i]
        m_ref[b, h:h + 1, :] = jnp.broadcast_to(m_new[i], (1, B_HEAD_DIM))
        y_ref[b, win, hs(h, A_WIDTH)] = out[i]


PROLOGUE_DONE, READS_STATE, WROTE_STATE, DONE = "prologue_done", "reads_state", "wrote_state", "done"
N_RWKV_W = 11
N_MLSTM_W = 4
N_CONSTS = 2


def _mix_kernel(*refs, chunk, hg, where, zero_first):
    it = iter(refs)
    take = lambda n: [next(it) for _ in range(n)]
    pa_ref, pb_ref, pg_ref, shift_ref, s0_ref, conv0_ref, c0_ref, n0_ref, m0_ref = take(9)
    rwkv_w, mlstm_w, slabs = take(N_RWKV_W), take(N_MLSTM_W), take(N_CONSTS)
    consts = {name: functools.partial(lambda ref, r0, rows, cols: ref[r0:r0 + rows, 0:cols], ref, *at)
              for ref, masks in zip(slabs, where) for name, at in masks}
    y_ref, shift_out_ref, s_out_ref, conv_out_ref, c_ref, n_ref, m_ref, pbuf, bds, xbuf = take(10)
    first = pl.program_id(1) == 0
    last = pl.program_id(1) == pl.num_programs(1) - 1
    zero_state = pl.program_id(0) == 0 if zero_first else None
    subs = pa_ref.shape[1] // chunk
    bodies = []
    for sub in range(subs):
        is_first = first if sub == 0 else None
        bodies.append([
            _rwkv_stages(pa_ref, shift_ref, s0_ref, rwkv_w, consts, y_ref, shift_out_ref, s_out_ref, pbuf, bds,
                         chunk=chunk, hg=hg, row0=sub * chunk, first=is_first,
                         last=last if sub == subs - 1 else None, zero_state=zero_state),
            _mlstm_stages(pb_ref, pg_ref, conv0_ref, c0_ref, n0_ref, m0_ref, mlstm_w, consts,
                          y_ref, conv_out_ref, c_ref, n_ref, m_ref, xbuf, chunk=chunk, row0=sub * chunk,
                          first=is_first, zero_state=zero_state),
        ])
    live = [(sub, kind) for sub in range(subs) for kind in range(2)]
    wrote, waiting, started = set(), set(), {(0, 0), (0, 1)}
    while live:
        for key in list(live):
            sub, kind = key
            if key not in started or (key in waiting and (sub - 1, kind) not in wrote):
                continue
            waiting.discard(key)
            mark = next(bodies[sub][kind], DONE)
            if mark == PROLOGUE_DONE:
                started.add((sub + 1, 0))
                if sub > 0:
                    started.add((sub, 1))
            elif mark == READS_STATE and sub > 0:
                waiting.add(key)
            elif mark in (WROTE_STATE, DONE):
                wrote.add(key)
                if mark == DONE:
                    live.remove(key)


def _mix(pa, pb, pg, l, sl, states, rwkv_w, mlstm_w, chunk, subs, zero_first):
    bn, t, _ = pa.shape
    nb = MIX_STREAMS
    hg = _rwkv_heads_per_group(chunk)
    hist = B_CONV - 1
    consts, where = _mix_consts(chunk, nb)
    full = lambda a: pl.BlockSpec(a.shape, lambda i, c: (0,) * a.ndim)
    tok = lambda n: pl.BlockSpec((nb, subs * chunk, n), lambda i, c: (i, c, 0))
    per_stream = lambda *dims: pl.BlockSpec((nb,) + dims, lambda i, c: (i,) + (0,) * len(dims))
    state_shapes = [(1, A_COLS), (A_HEADS, A_HEAD_DIM, A_HEAD_DIM), (hist, 2 * B_WIDTH),
                    (B_HEADS, B_HEAD_DIM, B_HEAD_DIM), (B_HEADS, B_HEAD_DIM), (B_HEADS, B_HEAD_DIM)]
    return pl.pallas_call(
        functools.partial(_mix_kernel, chunk=chunk, hg=hg, where=where, zero_first=zero_first),
        grid=(bn // nb, t // (subs * chunk)),
        in_specs=[tok(A_COLS), tok(B_MAIN), tok(GATE_PAD)]
        + [_state_spec(a, sl, nb, zero_first) for a in states]
        + [_layer_spec(a, l) for a in rwkv_w + mlstm_w] + [full(a) for a in consts],
        out_specs=[tok(D_MIX)] + [per_stream(*dims) for dims in state_shapes],
        out_shape=[jax.ShapeDtypeStruct((bn, t, D_MIX), F32)]
        + [jax.ShapeDtypeStruct((bn,) + dims, F32) for dims in state_shapes],
        scratch_shapes=[
            pltpu.VMEM((nb, CARRY, A_COLS), F32),
            pltpu.VMEM((nb, A_HEADS // hg, hg * A_HEAD_DIM, hg * A_HEAD_DIM), F32),
            pltpu.VMEM((nb, CARRY, 2 * B_WIDTH), F32),
        ],
        compiler_params=pltpu.CompilerParams(
            dimension_semantics=("arbitrary", "arbitrary"), vmem_limit_bytes=VMEM_LIMIT),
        name="mixers",
    )(pa, pb, pg, *states, *rwkv_w, *mlstm_w, *consts)


def _ffn_kernel(x_ref, y_ref, f0_ref, wout_ref, g_ref, wup_ref, cw_ref, cb_ref,
                wdown_ref, gfin_ref, o_ref, fout_ref, ubuf, *, tile, final, zero_streams):
    nb = x_ref.shape[0]
    hist = FFN_CONV - 1
    rows_of = lambda ref: ref[0] if nb == 1 else jnp.concatenate([ref[b] for b in range(nb)], axis=0)

    @pl.when(pl.program_id(1) == 0)
    def _():
        ubuf[...] = jnp.zeros(ubuf.shape, F32)
        for b in range(zero_streams, nb):
            ubuf[b, CARRY - hist:CARRY, :] = f0_ref[b - zero_streams]

    x1 = rows_of(x_ref) + _dot(rows_of(y_ref).astype(BF16), wout_ref[...])
    h2 = _rms(x1, g_ref[...]).astype(BF16)
    acts = []
    for lo, hi in zip(FF_BOUNDS[:-1], FF_BOUNDS[1:]):
        cols = slice(lo, hi)
        u = _dot(h2, wup_ref[:, cols])
        gate = _dot(h2, wup_ref[:, D_FF + lo:D_FF + hi])
        convs = []
        for b in range(nb):
            ub = u[b * tile:(b + 1) * tile]
            prev = ubuf[b, :, cols]
            acc = cb_ref[:, cols] + cw_ref[hist:hist + 1, cols] * ub
            for j in range(hist):
                acc = acc + cw_ref[j:j + 1, cols] * _shift_rows(ub, prev, hist - j)
            ubuf[b, :, cols] = ub[tile - CARRY:tile]
            fout_ref[b, :, cols] = ub[tile - hist:tile]
            convs.append(acc)
        conv = convs[0] if nb == 1 else jnp.concatenate(convs, axis=0)
        acts.append((_silu(conv) * gate).astype(BF16))
    x2 = x1 + _dot(jnp.concatenate(acts, axis=1), wdown_ref[...])
    out = _rms(x2, gfin_ref[...]) if final else x2
    for b in range(nb):
        o_ref[b] = out[b * tile:(b + 1) * tile]


def _ffn(x, y, l, sl, f0, wts, gfin, nb, tile, final, zero_streams):
    bn, t, _ = x.shape
    hist = FFN_CONV - 1
    tok = pl.BlockSpec((nb, tile, D_MODEL), lambda i, c: (i, c, 0))
    fspec = pl.BlockSpec((nb, hist, D_FF), lambda i, c: (i, 0, 0))
    return pl.pallas_call(
        functools.partial(_ffn_kernel, tile=tile, final=final, zero_streams=zero_streams),
        grid=(bn // nb, t // tile),
        in_specs=[tok, tok, _state_spec(f0, sl, nb - zero_streams)]
        + [_layer_spec(a, l, resident=True) for a in wts] + [_layer_spec(gfin, 0, resident=True)],
        out_specs=[tok, fspec],
        out_shape=[
            jax.ShapeDtypeStruct((bn, t, D_MODEL), F32),
            jax.ShapeDtypeStruct((bn, hist, D_FF), F32),
        ],
        scratch_shapes=[pltpu.VMEM((nb, CARRY, D_FF), F32)],
        compiler_params=pltpu.CompilerParams(
            dimension_semantics=("arbitrary", "arbitrary"), vmem_limit_bytes=VMEM_LIMIT),
        name="out_ffn",
    )(x, y, f0, *wts, gfin)


def _layer(x, l, sl, st, w, *, in_tile, mix_chunk, mix_subs, ffn_streams, ffn_tile, final, zero_streams):
    bn, t, _ = x.shape
    pa, pb, pg = _in_proj(x.reshape(bn * t, D_MODEL), l, w["norm_mix"], w["w_in"], w["w_gate"], in_tile)
    pa = pa.reshape(bn, t, A_COLS)
    pb = pb.reshape(bn, t, B_MAIN)
    pg = pg.reshape(bn, t, GATE_PAD)
    y, *mix_states = _mix(pa, pb, pg, l, sl, st[:6], w["rwkv"], w["mlstm"], mix_chunk, mix_subs,
                          zero_streams > 0)
    x, fconv1 = _ffn(x, y, l, sl, st[6], w["ffn"], w["norm_final"], ffn_streams, ffn_tile, final, zero_streams)
    return x, tuple(mix_states) + (fconv1,)


def kernel(x_prompt, x_sample, state_rwkv_shift, state_rwkv_wkv, state_mlstm_conv, state_mlstm_C,
           state_mlstm_n, state_mlstm_m, state_ffn_conv, meta_tokens, norm_mix, w_in, a_mu, a_w0, a_w2,
           a_a0, a_a2, a_g2, a_k_k, a_k_a, a_r_k, a_ln_w, a_ln_b, b_conv_w, b_conv_b, b_i_bias, b_f_bias,
           b_hn_w, w_out, norm_ffn, w_up, ffn_conv_w, ffn_conv_b, w_down, norm_final):
    n_prompt = x_prompt.shape[0]
    n_sample = x_sample.shape[0]
    n_lead = n_prompt + n_sample
    assert x_sample.shape[1] == N_META
    assert n_prompt == MIX_STREAMS and n_lead % MIX_STREAMS == 0

    bf = lambda a: a.astype(BF16)
    vec = lambda a: a[:, None, :]
    n_gate = 2 * B_HEADS
    gate_pad = ((0, 0), (0, GATE_PAD - n_gate))
    w = {
        "norm_mix": vec(norm_mix),
        "w_in": bf(w_in),
        "w_gate": bf(jnp.pad(w_in[:, :, A_COLS + B_MAIN:], ((0, 0),) + gate_pad)),
        "rwkv": [vec(a_mu), vec(a_w0), bf(a_w2), vec(a_a0), bf(a_a2), bf(a_g2), vec(a_k_k), vec(a_k_a),
                 vec(a_r_k), vec(a_ln_w), vec(a_ln_b)],
        "mlstm": [b_conv_w, vec(b_conv_b), vec(jnp.pad(jnp.concatenate([b_i_bias, b_f_bias], axis=1), gate_pad)),
                  vec(b_hn_w)],
        "ffn": (bf(w_out), vec(norm_ffn), bf(w_up), ffn_conv_w, vec(ffn_conv_b), bf(w_down)),
        "norm_final": norm_final.reshape(1, 1, D_MODEL),
    }
    assert len(w["rwkv"]) == N_RWKV_W and len(w["mlstm"]) == N_MLSTM_W

    st_lead_in = (
        state_rwkv_shift[:, :, None, :], state_rwkv_wkv, state_mlstm_conv, state_mlstm_C, state_mlstm_n,
        jnp.broadcast_to(state_mlstm_m[..., None], state_mlstm_m.shape + (B_HEAD_DIM,)), state_ffn_conv,
    )
    meta = jnp.broadcast_to(meta_tokens[None].astype(x_prompt.dtype), (n_prompt, N_META, D_MODEL))
    x_lead = jnp.concatenate([meta, x_sample], axis=0)
    x_main = x_prompt

    main_states, lead_states = [], []
    for l in range(DEPTH):
        final = l == DEPTH - 1
        x_lead, st_lead = _layer(x_lead, l, l, st_lead_in, w, in_tile=n_lead * N_META, mix_chunk=N_META,
                                 mix_subs=1, ffn_streams=n_lead, ffn_tile=N_META, final=final,
                                 zero_streams=n_prompt)
        x_main, st_main = _layer(x_main, l, 0, tuple(s[None] for s in st_lead), w, in_tile=MAIN_PROJ_TILE,
                                 mix_chunk=MAIN_CHUNK, mix_subs=MAIN_SUBCHUNKS, ffn_streams=1, ffn_tile=MAIN_TILE, final=final,
                                 zero_streams=0)
        main_states.append(st_main)
        lead_states.append(st_lead)

    def collect(per_layer, first):
        shift, wkv, bconv, c, n, m, fconv = (jnp.stack([st[i] for st in per_layer])[:, first:] for i in range(7))
        return (shift[:, :, 0, :], wkv, bconv, c, n, m[..., 0], fconv)

    return (x_main, x_lead[n_prompt:]) + collect(main_states, 0) + collect(lead_states, n_prompt)
```

```python
import functools
import math

import numpy as np

import jax
import jax.numpy as jnp
from jax import lax
from jax.experimental import pallas as pl
from jax.experimental.pallas import tpu as pltpu

D_MODEL = 1024
DEPTH = 2
N_META = 16
A_HEADS = 8
A_HEAD_DIM = 64
A_WIDTH = 512
A_DECAY_LORA = 64
A_AAA_LORA = 64
A_GATE_LORA = 128
A_COLS = 1792
B_HEADS = 4
B_HEAD_DIM = 128
B_WIDTH = 512
B_CONV = 4
B_MAIN = 4 * B_WIDTH
GATE_PAD = 128
D_MIX = A_WIDTH + B_WIDTH
D_FF = 2816
FFN_CONV = 3
RMS_EPS = 1e-6
GN_EPS = 64e-5
CARRY = 8
ROW_BLOCK = 16
LANES = 128
MXU_DIM = 256
MIX_STREAMS = 4
FF_BOUNDS = (0, 6 * MXU_DIM, D_FF)
MAIN_TILE = 512
MAIN_PROJ_TILE = 1024
MAIN_CHUNK = 64
MAIN_SUBCHUNKS = 2
VMEM_LIMIT = 56 * 1024 * 1024

F32 = jnp.float32
BF16 = jnp.bfloat16
NT_DIMS = (((1,), (1,)), ((), ()))
TN_DIMS = (((0,), (0,)), ((), ()))


def _dot(a, b):
    return jnp.dot(a, b, preferred_element_type=F32)


def _dot_nt(a, b):
    return lax.dot_general(a, b, NT_DIMS, preferred_element_type=F32)


def _dot_tn(a, b):
    return lax.dot_general(a, b, TN_DIMS, preferred_element_type=F32)


def _sigmoid(x):
    return 0.5 * jnp.tanh(0.5 * x) + 0.5


def _silu(x):
    half = 0.5 * x
    return half + half * jnp.tanh(half)


def _softplus(x):
    return jnp.maximum(x, 0.0) + jnp.log(1.0 + jnp.exp(-jnp.abs(x)))


def _rms(x, g):
    return x * lax.rsqrt(jnp.mean(x * x, axis=-1, keepdims=True) + RMS_EPS) * g


def _tril(n, strict):
    row = lax.broadcasted_iota(jnp.int32, (n, n), 0)
    col = lax.broadcasted_iota(jnp.int32, (n, n), 1)
    return (col < row) if strict else (col <= row)


def _split3(x):
    hi = x.astype(BF16)
    rest = x - hi.astype(F32)
    mid = rest.astype(BF16)
    lo = (rest - mid.astype(F32)).astype(BF16)
    return hi, mid, lo


def _shift_rows(x, prev, s):
    rolled = pltpu.roll(x, s, 0)
    from_prev = lax.broadcasted_iota(jnp.int32, prev.shape, 0) < s
    head = jnp.where(from_prev, pltpu.roll(prev, s, 0), rolled[0:CARRY])
    return jnp.concatenate([head, rolled[CARRY:]], axis=0)


def _rowwise(fn, n_rows, *xs):
    pieces = [fn(*[x[i:i + ROW_BLOCK] for x in xs]) for i in range(0, n_rows, ROW_BLOCK)]
    return [jnp.concatenate(col, axis=0) for col in zip(*pieces)]


def _layer_spec(a, l, resident=False):
    index = lambda *grid: (l,) + (0,) * (a.ndim - 1)
    if resident:
        return pl.BlockSpec((None,) + a.shape[1:], index, pipeline_mode=pl.Buffered(1))
    return pl.BlockSpec((None,) + a.shape[1:], index)


def _state_spec(a, l, nb, skip_first=False):
    block = (lambda i: jnp.maximum(i - 1, 0)) if skip_first else (lambda i: i)
    return pl.BlockSpec((None, nb) + a.shape[2:], lambda i, c: (l, block(i)) + (0,) * (a.ndim - 2))


def _unless(zero_state, load):
    if zero_state is None:
        load()
    else:
        pl.when(jnp.logical_not(zero_state))(load)


def _in_proj_kernel(x_ref, g_ref, w_ref, wg_ref, oa_ref, ob_ref, og_ref):
    h = _rms(x_ref[...], g_ref[...]).astype(BF16)
    oa_ref[...] = _dot(h, w_ref[:, 0:A_COLS])
    ob_ref[...] = _dot(h, w_ref[:, A_COLS:A_COLS + B_MAIN])
    og_ref[...] = _dot(h, wg_ref[...])


def _in_proj(x2d, l, g, w, wg, tm):
    m = x2d.shape[0]
    return pl.pallas_call(
        _in_proj_kernel,
        grid=(m // tm,),
        in_specs=[
            pl.BlockSpec((tm, D_MODEL), lambda i: (i, 0)),
            _layer_spec(g, l), _layer_spec(w, l, resident=True), _layer_spec(wg, l, resident=True),
        ],
        out_specs=[
            pl.BlockSpec((tm, A_COLS), lambda i: (i, 0)),
            pl.BlockSpec((tm, B_MAIN), lambda i: (i, 0)),
            pl.BlockSpec((tm, GATE_PAD), lambda i: (i, 0)),
        ],
        out_shape=[
            jax.ShapeDtypeStruct((m, A_COLS), F32),
            jax.ShapeDtypeStruct((m, B_MAIN), F32),
            jax.ShapeDtypeStruct((m, GATE_PAD), F32),
        ],
        compiler_params=pltpu.CompilerParams(
            dimension_semantics=("arbitrary",), vmem_limit_bytes=VMEM_LIMIT),
        name="in_proj",
    )(x2d, g, w, wg)


def _rwkv_heads_per_group(chunk):
    return min(A_HEADS, MXU_DIM // chunk, MXU_DIM // A_HEAD_DIM)


def _mix_consts(chunk, streams):
    hg = _rwkv_heads_per_group(chunk)
    hl, w, rows = hg * chunk, hg * A_HEAD_DIM, streams * chunk
    ix = lambda n: (np.arange(n)[:, None], np.arange(n)[None, :])
    r, c = ix(MXU_DIM)
    seg = r // A_HEAD_DIM == c // A_HEAD_DIM
    r, c = ix(rows)
    tril = (r // chunk == c // chunk) & (c <= r)
    r, c = np.arange(2 * chunk)[:, None], np.arange(hl)[None, :]
    amask = np.where(r < chunk, c % chunk < r, c % chunk <= r - chunk)
    r, c = np.arange(chunk)[:, None], np.arange(hl)[None, :]
    eye = c % chunk == r
    r, c = np.arange(hl)[:, None], np.arange(w)[None, :]
    bdl = r // chunk == c // A_HEAD_DIM
    r, c = ix(hl)
    bdsq = r // chunk == c // chunk
    r, c = ix(w)
    smask = r // A_HEAD_DIM == c // A_HEAD_DIM

    def pack(dtype, **masks):
        width = max(m.shape[1] for m in masks.values())
        rows = np.cumsum([0] + [m.shape[0] for m in masks.values()])
        slab = np.concatenate([np.pad(m, ((0, 0), (0, width - m.shape[1]))) for m in masks.values()], axis=0)
        return jnp.asarray(slab, dtype), tuple((name, (int(r0),) + m.shape) for (name, m), r0 in zip(masks.items(), rows))

    slab16, where16 = pack(BF16, seg=seg, tril=tril, bdl=bdl, bdsq=bdsq)
    slab32, where32 = pack(F32, amask=amask, eye=eye, smask=smask)
    return (slab16, slab32), (where16, where32)


def _rwkv_stages(p_ref, shift_ref, s0_ref, wts, consts, y_ref, shift_out_ref, s_out_ref, pbuf, bds,
                 *, chunk, hg, row0, first, last, zero_state):
    mu_ref, w0_ref, w2_ref, a0_ref, a2_ref, g2_ref, kk_ref, ka_ref, rk_ref, lnw_ref, lnb_ref = wts
    L = chunk
    nb = p_ref.shape[0]
    rows = nb * L
    hl = hg * L
    wid = hg * A_HEAD_DIM
    groups = A_HEADS // hg
    diag = lambda h: (h // hg, slice((h % hg) * A_HEAD_DIM, (h % hg + 1) * A_HEAD_DIM))
    win = slice(row0, row0 + L)

    if first is not None:
        @pl.when(first)
        def _():
            bds[...] = jnp.zeros(bds.shape, F32)
            pbuf[...] = jnp.zeros(pbuf.shape, F32)

            def load():
                for b in range(nb):
                    pbuf[b, CARRY - 1:CARRY, :] = shift_ref[b]
                    for h in range(A_HEADS):
                        g, sl = diag(h)
                        bds[b, g, sl, sl] = s0_ref[b, h]
            _unless(zero_state, load)

    ps, prevs = [], []
    for b in range(nb):
        pb = p_ref[b, win, :]
        prevs.append(_shift_rows(pb, pbuf[b], 1))
        pbuf[b] = pb[L - CARRY:L, :]
        shift_out_ref[b] = pb[L - 1:L, :]
        ps.append(pb)
    i1, i2, i3 = A_WIDTH, 2 * A_WIDTH, 3 * A_WIDTH
    i4, i5 = i3 + A_DECAY_LORA, i3 + A_DECAY_LORA + A_AAA_LORA
    mu = mu_ref[...]

    def token_shift(p, prev):
        pm = p + (prev - p) * mu
        return (pm[:, :i1], pm[:, i1:i2], pm[:, i2:i3], jnp.tanh(pm[:, i3:i4]).astype(BF16),
                pm[:, i4:i5].astype(BF16), _sigmoid(pm[:, i5:]).astype(BF16))

    r, k, v, wl16, al16, gl16 = _rowwise(token_shift, rows, jnp.concatenate(ps, axis=0), jnp.concatenate(prevs, axis=0))
    yield
    seg = consts["seg"]()
    half = A_WIDTH // 2
    segsum = lambda t16: jnp.concatenate([_dot(t16[:, :half], seg), _dot(t16[:, half:], seg)], axis=1)
    z = w0_ref[...] + _dot(wl16, w2_ref[...])
    a_pre = a0_ref[...] + _dot(al16, a2_ref[...])
    gate = _dot(gl16, g2_ref[...])
    yield
    k_k, k_a, r_k = kk_ref[...], ka_ref[...], rk_ref[...]

    def decay_and_keys(k, z, a_pre):
        lw = -math.exp(-0.5) * _sigmoid(z)
        lw_hi = lw.astype(BF16)
        a = _sigmoid(a_pre)
        kk0 = k * k_k
        return (lw, lw_hi, (lw - lw_hi.astype(F32)).astype(BF16), a, kk0, (kk0 * kk0).astype(BF16),
                k * (1.0 + (a - 1.0) * k_a))

    lw, lw_hi, lw_lo, a, kk0, kk_sq16, kmod = _rowwise(decay_and_keys, rows, k, z, a_pre)
    yield
    tril = consts["tril"]()
    cum = _dot(tril, lw_hi) + _dot(tril, lw_lo)
    kk_ss = segsum(kk_sq16)
    yield

    rs = lambda b: slice(b * L, (b + 1) * L)
    cum_last = [cum[(b + 1) * L - 1:(b + 1) * L] for b in range(nb)]
    g_last = [jnp.exp(cum_last[b]) for b in range(nb)]

    def scaled_operands(b):
        def fn(r, v, kk0, kk_ss, a, kmod, lw, cum):
            kk = kk0 * lax.rsqrt(kk_ss + 1e-12)
            bvec = kk * a
            g_inv = jnp.exp(-cum)
            g_tail = jnp.exp(cum_last[b] - cum)
            return ((kk * jnp.exp(cum - lw)).astype(BF16), (r * jnp.exp(cum)).astype(BF16),
                    (kmod * g_inv).astype(BF16), (bvec * g_inv).astype(BF16), v.astype(BF16),
                    (kmod * g_tail).astype(BF16), (bvec * g_tail).astype(BF16), (r * kmod * r_k).astype(BF16))
        return fn

    per_stream = [_rowwise(scaled_operands(b), L, *[t[rs(b)] for t in (r, v, kk0, kk_ss, a, kmod, lw, cum)])
                  for b in range(nb)]
    kkg_s, rg_s, kd_s, bd_s, v_s, kt, bt, bonus_s = zip(*per_stream)
    bonus_ss = segsum(jnp.concatenate(bonus_s, axis=0))
    yield PROLOGUE_DONE

    keep = consts["amask"]() > 0.5
    eye = consts["eye"]()
    bdl = consts["bdl"]()
    bdsq = consts["bdsq"]()
    smask = consts["smask"]()
    def block_diag(t, mask, seg):
        per_tile = max(1, LANES // seg)
        tile = per_tile * seg
        rows = []
        for h in range(hg):
            cols = [t[:, j:j + tile] * mask[h * L:(h + 1) * L, j:j + tile] if j // tile == h // per_tile
                    else jnp.zeros((L, tile), t.dtype) for j in range(0, t.shape[1], tile)]
            rows.append(cols[0] if len(cols) == 1 else jnp.concatenate(cols, axis=1))
        return jnp.concatenate(rows, axis=0)

    lane_bd = lambda t: block_diag(t, bdl, A_HEAD_DIM)
    sq_bd = lambda t: block_diag(t, bdsq, L)

    chains = [(b, g) for b in range(nb) for g in range(groups)]
    ls = lambda g: slice(g * wid, (g + 1) * wid)
    each = lambda f: [f(i, b, g) for i, (b, g) in enumerate(chains)]

    kkg = each(lambda i, b, g: kkg_s[b][:, ls(g)])
    rg = each(lambda i, b, g: rg_s[b][:, ls(g)])
    v16 = each(lambda i, b, g: v_s[b][:, ls(g)])
    lhs = each(lambda i, b, g: jnp.concatenate([kkg[i], rg[i]], axis=0))
    a_k = each(lambda i, b, g: jnp.where(keep, _dot_nt(lhs[i], lane_bd(kd_s[b][:, ls(g)])), 0.0))
    yield
    a_b = each(lambda i, b, g: jnp.where(keep, _dot_nt(lhs[i], lane_bd(bd_s[b][:, ls(g)])), 0.0))
    a_kk = each(lambda i, b, g: a_k[i][:L].astype(BF16))
    a_rk = each(lambda i, b, g: a_k[i][L:].astype(BF16))
    a_rb = each(lambda i, b, g: a_b[i][L:].astype(BF16))
    yield

    npow = each(lambda i, b, g: -a_b[i][:L])
    tinv = each(lambda i, b, g: eye + npow[i])
    n16 = each(lambda i, b, g: npow[i].astype(BF16))
    npow = each(lambda i, b, g: _dot(n16[i], sq_bd(n16[i])))
    yield
    span = 2
    while span < L:
        n16 = each(lambda i, b, g: npow[i].astype(BF16))
        t_bd = each(lambda i, b, g: sq_bd(tinv[i].astype(BF16)))
        if 2 * span < L:
            both = each(lambda i, b, g: _dot(n16[i], jnp.concatenate([sq_bd(n16[i]), t_bd[i]], axis=1)))
            npow = each(lambda i, b, g: both[i][:, :hl])
            prod = each(lambda i, b, g: both[i][:, hl:])
        else:
            prod = each(lambda i, b, g: _dot(n16[i], t_bd[i]))
        tinv = each(lambda i, b, g: tinv[i] + prod[i])
        span *= 2
        yield

    yield READS_STATE
    s16 = each(lambda i, b, g: bds[b, g].astype(BF16))
    v_bd = each(lambda i, b, g: lane_bd(v16[i]))
    rhs = each(lambda i, b, g: (_dot_nt(kkg[i], s16[i]) + _dot(a_kk[i], v_bd[i])).astype(BF16))
    yield
    u16 = each(lambda i, b, g: _dot(tinv[i].astype(BF16), lane_bd(rhs[i])).astype(BF16))
    yield
    yc_ = each(lambda i, b, g: _dot_nt(rg[i], s16[i]) + _dot(a_rk[i], v_bd[i]) - _dot(a_rb[i], lane_bd(u16[i])))
    yield
    upd = each(lambda i, b, g: _dot_tn(jnp.concatenate([v16[i], -u16[i]], axis=0),
                                       jnp.concatenate([kt[b][:, ls(g)], bt[b][:, ls(g)]], axis=0)))
    for i, (b, g) in enumerate(chains):
        bds[b, g] = bds[b, g] * g_last[b][:, ls(g)] + upd[i] * smask
    yield WROTE_STATE
    ys = [jnp.concatenate(yc_[b * groups:(b + 1) * groups], axis=1) for b in range(nb)]
    y = jnp.concatenate(ys, axis=0)

    inv_d = 1.0 / A_HEAD_DIM
    y_sum = segsum(y.astype(BF16))

    def centre(y, y_sum):
        yc = y - y_sum * inv_d
        return yc, (yc * yc).astype(BF16)

    yc, yc_sq16 = _rowwise(centre, rows, y, y_sum)
    var_sum = segsum(yc_sq16)
    yield
    ln_w, ln_b = lnw_ref[...], lnb_ref[...]

    def finish(yc, var_sum, bonus_ss, v, gate):
        return ((yc * lax.rsqrt(var_sum * inv_d + GN_EPS) * ln_w + ln_b + bonus_ss * v) * gate,)

    out, = _rowwise(finish, rows, yc, var_sum, bonus_ss, v, gate)
    for b in range(nb):
        y_ref[b, win, 0:A_WIDTH] = out[b * L:(b + 1) * L]

    if last is not None:
        @pl.when(last)
        def _():
            for b in range(nb):
                for h in range(A_HEADS):
                    g, sl = diag(h)
                    s_out_ref[b, h] = bds[b, g, sl, sl]


def _mlstm_stages(pb_ref, pg_ref, conv0_ref, c0_ref, n0_ref, m0_ref, wts, consts,
                  y_ref, conv_out_ref, c_ref, n_ref, m_ref, xbuf, *, chunk, row0, first, zero_state):
    cw_ref, cb_ref, gb_ref, hnw_ref = wts
    win = slice(row0, row0 + chunk)
    L = chunk
    nb = pb_ref.shape[0]
    rows = nb * L
    hist = B_CONV - 1

    if first is not None:
        @pl.when(first)
        def _():
            xbuf[...] = jnp.zeros(xbuf.shape, F32)
            c_ref[...] = jnp.zeros(c_ref.shape, F32)
            n_ref[...] = jnp.zeros(n_ref.shape, F32)
            m_ref[...] = jnp.zeros(m_ref.shape, F32)

            def load():
                for b in range(nb):
                    xbuf[b, CARRY - hist:CARRY, :] = conv0_ref[b]
                c_ref[...] = c0_ref[...]
                n_ref[...] = n0_ref[...]
                m_ref[...] = m0_ref[...]
            _unless(zero_state, load)

    qk = []
    for b in range(nb):
        x = pb_ref[b, win, 0:2 * B_WIDTH]
        prev = xbuf[b]
        acc = cb_ref[...] + cw_ref[hist:hist + 1, :] * x
        for j in range(hist):
            acc = acc + cw_ref[j:j + 1, :] * _shift_rows(x, prev, hist - j)
        xbuf[b] = x[L - CARRY:L, :]
        conv_out_ref[b] = x[L - hist:L, :]
        qk.append(_silu(acc))
        yield

    gates = jnp.concatenate([pg_ref[b, win, :] for b in range(nb)], axis=0) + gb_ref[...]
    lane = lax.broadcasted_iota(jnp.int32, (rows, GATE_PAD), 1)
    tril = consts["tril"]()
    cum = sum(_dot(tril, part) for part in _split3(-_softplus(-gates)))
    gcols = jnp.where(lane < B_HEADS, gates, cum)
    grows = gcols.T
    causal = _tril(L, False)
    yield

    chains = [(b, h) for b in range(nb) for h in range(B_HEADS)]
    each = lambda f: [f(i, b, h) for i, (b, h) in enumerate(chains)]
    rs = lambda b: slice(b * L, (b + 1) * L)
    hs = lambda h, base=0: slice(base + h * B_HEAD_DIM, base + (h + 1) * B_HEAD_DIM)

    q = each(lambda i, b, h: qk[b][:, hs(h)])
    k = each(lambda i, b, h: qk[b][:, hs(h, B_WIDTH)] * (B_HEAD_DIM ** -0.5))
    q16 = each(lambda i, b, h: q[i].astype(BF16))
    v16 = each(lambda i, b, h: pb_ref[b, win, hs(h, 2 * B_WIDTH)].astype(BF16))
    li_col = each(lambda i, b, h: gcols[rs(b), h:h + 1])
    b_col = each(lambda i, b, h: gcols[rs(b), B_HEADS + h:B_HEADS + h + 1])
    li_row = each(lambda i, b, h: grows[h:h + 1, rs(b)])
    b_row = each(lambda i, b, h: grows[B_HEADS + h:B_HEADS + h + 1, rs(b)])
    yield READS_STATE
    c_prev = each(lambda i, b, h: c_ref[b, h])
    n_prev = each(lambda i, b, h: n_ref[b, h:h + 1, :])
    m_prev = each(lambda i, b, h: m_ref[b, h:h + 1, 0:1])

    dm = each(lambda i, b, h: jnp.where(causal, b_col[i] - b_row[i] + li_row[i], -jnp.inf))
    inter = each(lambda i, b, h: b_col[i] + m_prev[i])
    mt = each(lambda i, b, h: jnp.maximum(inter[i], jnp.max(dm[i], axis=-1, keepdims=True)))
    wo = each(lambda i, b, h: jnp.exp(inter[i] - mt[i]))
    yield
    s = each(lambda i, b, h: _dot_nt(q16[i], k[i].astype(BF16)) * jnp.exp(dm[i] - mt[i]))
    yield
    num = each(lambda i, b, h: wo[i] * _dot(q16[i], c_prev[i].astype(BF16)) + _dot(s[i].astype(BF16), v16[i]))
    yield
    den = each(lambda i, b, h: wo[i] * jnp.sum(q[i] * n_prev[i], axis=-1, keepdims=True)
               + jnp.sum(s[i], axis=-1, keepdims=True))
    hh = each(lambda i, b, h: num[i] * (1.0 / jnp.maximum(jnp.abs(den[i]), jnp.exp(-mt[i]))))
    yield

    m_new = each(lambda i, b, h: mt[i][L - 1:L, :])
    b_last = each(lambda i, b, h: b_col[i][L - 1:L, :])
    kw = each(lambda i, b, h: k[i] * jnp.exp(b_last[i] - b_col[i] + li_col[i] - m_new[i]))
    dec = each(lambda i, b, h: jnp.exp(b_last[i] + m_prev[i] - m_new[i]))
    yield
    c_new = each(lambda i, b, h: dec[i] * c_prev[i] + _dot_tn(kw[i].astype(BF16), v16[i]))
    n_new = each(lambda i, b, h: dec[i] * n_prev[i] + jnp.sum(kw[i], axis=0, keepdims=True))
    yield
    out = each(lambda i, b, h: hh[i] * lax.rsqrt(jnp.mean(hh[i] * hh[i], axis=-1, keepdims=True) + RMS_EPS)
               * hnw_ref[:, hs(h)] * _sigmoid(pb_ref[b, win, hs(h, 3 * B_WIDTH)]))
    for i, (b, h) in enumerate(chains):
        c_ref[b, h] = c_new[i]
        n_ref[b, h:h + 1, :] = n_new[i]
        m_ref[b, h:h + 1, :] = jnp.broadcast_to(m_new[i], (1, B_HEAD_DIM))
        y_ref[b, win, hs(h, A_WIDTH)] = out[i]


PROLOGUE_DONE, READS_STATE, WROTE_STATE, DONE = "prologue_done", "reads_state", "wrote_state", "done"
N_RWKV_W = 11
N_MLSTM_W = 4
N_CONSTS = 2


def _mix_kernel(*refs, chunk, hg, where, zero_first):
    it = iter(refs)
    take = lambda n: [next(it) for _ in range(n)]
    pa_ref, pb_ref, pg_ref, shift_ref, s0_ref, conv0_ref, c0_ref, n0_ref, m0_ref = take(9)
    rwkv_w, mlstm_w, slabs = take(N_RWKV_W), take(N_MLSTM_W), take(N_CONSTS)
    consts = {name: functools.partial(lambda ref, r0, rows, cols: ref[r0:r0 + rows, 0:cols], ref, *at)
              for ref, masks in zip(slabs, where) for name, at in masks}
    y_ref, shift_out_ref, s_out_ref, conv_out_ref, c_ref, n_ref, m_ref, pbuf, bds, xbuf = take(10)
    first = pl.program_id(1) == 0
    last = pl.program_id(1) == pl.num_programs(1) - 1
    zero_state = pl.program_id(0) == 0 if zero_first else None
    subs = pa_ref.shape[1] // chunk
    bodies = []
    for sub in range(subs):
        is_first = first if sub == 0 else None
        bodies.append([
            _rwkv_stages(pa_ref, shift_ref, s0_ref, rwkv_w, consts, y_ref, shift_out_ref, s_out_ref, pbuf, bds,
                         chunk=chunk, hg=hg, row0=sub * chunk, first=is_first,
                         last=last if sub == subs - 1 else None, zero_state=zero_state),
            _mlstm_stages(pb_ref, pg_ref, conv0_ref, c0_ref, n0_ref, m0_ref, mlstm_w, consts,
                          y_ref, conv_out_ref, c_ref, n_ref, m_ref, xbuf, chunk=chunk, row0=sub * chunk,
                          first=is_first, zero_state=zero_state),
        ])
    live = [(sub, kind) for sub in range(subs) for kind in range(2)]
    wrote, waiting, started = set(), set(), {(0, 0), (0, 1)}
    while live:
        for key in list(live):
            sub, kind = key
            if key not in started or (key in waiting and (sub - 1, kind) not in wrote):
                continue
            waiting.discard(key)
            mark = next(bodies[sub][kind], DONE)
            if mark == PROLOGUE_DONE:
                started.add((sub + 1, 0))
                if sub > 0:
                    started.add((sub, 1))
            elif mark == READS_STATE and sub > 0:
                waiting.add(key)
            elif mark in (WROTE_STATE, DONE):
                wrote.add(key)
                if mark == DONE:
                    live.remove(key)


def _mix(pa, pb, pg, l, sl, states, rwkv_w, mlstm_w, chunk, subs, zero_first):
    bn, t, _ = pa.shape
    nb = MIX_STREAMS
    hg = _rwkv_heads_per_group(chunk)
    hist = B_CONV - 1
    consts, where = _mix_consts(chunk, nb)
    full = lambda a: pl.BlockSpec(a.shape, lambda i, c: (0,) * a.ndim)
    tok = lambda n: pl.BlockSpec((nb, subs * chunk, n), lambda i, c: (i, c, 0))
    per_stream = lambda *dims: pl.BlockSpec((nb,) + dims, lambda i, c: (i,) + (0,) * len(dims))
    state_shapes = [(1, A_COLS), (A_HEADS, A_HEAD_DIM, A_HEAD_DIM), (hist, 2 * B_WIDTH),
                    (B_HEADS, B_HEAD_DIM, B_HEAD_DIM), (B_HEADS, B_HEAD_DIM), (B_HEADS, B_HEAD_DIM)]
    return pl.pallas_call(
        functools.partial(_mix_kernel, chunk=chunk, hg=hg, where=where, zero_first=zero_first),
        grid=(bn // nb, t // (subs * chunk)),
        in_specs=[tok(A_COLS), tok(B_MAIN), tok(GATE_PAD)]
        + [_state_spec(a, sl, nb, zero_first) for a in states]
        + [_layer_spec(a, l) for a in rwkv_w + mlstm_w] + [full(a) for a in consts],
        out_specs=[tok(D_MIX)] + [per_stream(*dims) for dims in state_shapes],
        out_shape=[jax.ShapeDtypeStruct((bn, t, D_MIX), F32)]
        + [jax.ShapeDtypeStruct((bn,) + dims, F32) for dims in state_shapes],
        scratch_shapes=[
            pltpu.VMEM((nb, CARRY, A_COLS), F32),
            pltpu.VMEM((nb, A_HEADS // hg, hg * A_HEAD_DIM, hg * A_HEAD_DIM), F32),
            pltpu.VMEM((nb, CARRY, 2 * B_WIDTH), F32),
        ],
        compiler_params=pltpu.CompilerParams(
            dimension_semantics=("arbitrary", "arbitrary"), vmem_limit_bytes=VMEM_LIMIT),
        name="mixers",
    )(pa, pb, pg, *states, *rwkv_w, *mlstm_w, *consts)


def _ffn_kernel(x_ref, y_ref, f0_ref, wout_ref, g_ref, wup_ref, cw_ref, cb_ref,
                wdown_ref, gfin_ref, o_ref, fout_ref, ubuf, *, tile, final, zero_streams):
    nb = x_ref.shape[0]
    hist = FFN_CONV - 1
    rows_of = lambda ref: ref[0] if nb == 1 else jnp.concatenate([ref[b] for b in range(nb)], axis=0)

    @pl.when(pl.program_id(1) == 0)
    def _():
        ubuf[...] = jnp.zeros(ubuf.shape, F32)
        for b in range(zero_streams, nb):
            ubuf[b, CARRY - hist:CARRY, :] = f0_ref[b - zero_streams]

    x1 = rows_of(x_ref) + _dot(rows_of(y_ref).astype(BF16), wout_ref[...])
    h2 = _rms(x1, g_ref[...]).astype(BF16)
    acts = []
    for lo, hi in zip(FF_BOUNDS[:-1], FF_BOUNDS[1:]):
        cols = slice(lo, hi)
        u = _dot(h2, wup_ref[:, cols])
        gate = _dot(h2, wup_ref[:, D_FF + lo:D_FF + hi])
        convs = []
        for b in range(nb):
            ub = u[b * tile:(b + 1) * tile]
            prev = ubuf[b, :, cols]
            acc = cb_ref[:, cols] + cw_ref[hist:hist + 1, cols] * ub
            for j in range(hist):
                acc = acc + cw_ref[j:j + 1, cols] * _shift_rows(ub, prev, hist - j)
            ubuf[b, :, cols] = ub[tile - CARRY:tile]
            fout_ref[b, :, cols] = ub[tile - hist:tile]
            convs.append(acc)
        conv = convs[0] if nb == 1 else jnp.concatenate(convs, axis=0)
        acts.append((_silu(conv) * gate).astype(BF16))
    x2 = x1 + _dot(jnp.concatenate(acts, axis=1), wdown_ref[...])
    out = _rms(x2, gfin_ref[...]) if final else x2
    for b in range(nb):
        o_ref[b] = out[b * tile:(b + 1) * tile]


def _ffn(x, y, l, sl, f0, wts, gfin, nb, tile, final, zero_streams):
    bn, t, _ = x.shape
    hist = FFN_CONV - 1
    tok = pl.BlockSpec((nb, tile, D_MODEL), lambda i, c: (i, c, 0))
    fspec = pl.BlockSpec((nb, hist, D_FF), lambda i, c: (i, 0, 0))
    return pl.pallas_call(
        functools.partial(_ffn_kernel, tile=tile, final=final, zero_streams=zero_streams),
        grid=(bn // nb, t // tile),
        in_specs=[tok, tok, _state_spec(f0, sl, nb - zero_streams)]
        + [_layer_spec(a, l, resident=True) for a in wts] + [_layer_spec(gfin, 0, resident=True)],
        out_specs=[tok, fspec],
        out_shape=[
            jax.ShapeDtypeStruct((bn, t, D_MODEL), F32),
            jax.ShapeDtypeStruct((bn, hist, D_FF), F32),
        ],
        scratch_shapes=[pltpu.VMEM((nb, CARRY, D_FF), F32)],
        compiler_params=pltpu.CompilerParams(
            dimension_semantics=("arbitrary", "arbitrary"), vmem_limit_bytes=VMEM_LIMIT),
        name="out_ffn",
    )(x, y, f0, *wts, gfin)


def _layer(x, l, sl, st, w, *, in_tile, mix_chunk, mix_subs, ffn_streams, ffn_tile, final, zero_streams):
    bn, t, _ = x.shape
    pa, pb, pg = _in_proj(x.reshape(bn * t, D_MODEL), l, w["norm_mix"], w["w_in"], w["w_gate"], in_tile)
    pa = pa.reshape(bn, t, A_COLS)
    pb = pb.reshape(bn, t, B_MAIN)
    pg = pg.reshape(bn, t, GATE_PAD)
    y, *mix_states = _mix(pa, pb, pg, l, sl, st[:6], w["rwkv"], w["mlstm"], mix_chunk, mix_subs,
                          zero_streams > 0)
    x, fconv1 = _ffn(x, y, l, sl, st[6], w["ffn"], w["norm_final"], ffn_streams, ffn_tile, final, zero_streams)
    return x, tuple(mix_states) + (fconv1,)


def kernel(x_prompt, x_sample, state_rwkv_shift, state_rwkv_wkv, state_mlstm_conv, state_mlstm_C,
           state_mlstm_n, state_mlstm_m, state_ffn_conv, meta_tokens, norm_mix, w_in, a_mu, a_w0, a_w2,
           a_a0, a_a2, a_g2, a_k_k, a_k_a, a_r_k, a_ln_w, a_ln_b, b_conv_w, b_conv_b, b_i_bias, b_f_bias,
           b_hn_w, w_out, norm_ffn, w_up, ffn_conv_w, ffn_conv_b, w_down, norm_final):
    n_prompt = x_prompt.shape[0]
    n_sample = x_sample.shape[0]
    n_lead = n_prompt + n_sample
    assert x_sample.shape[1] == N_META
    assert n_prompt == MIX_STREAMS and n_lead % MIX_STREAMS == 0

    bf = lambda a: a.astype(BF16)
    vec = lambda a: a[:, None, :]
    n_gate = 2 * B_HEADS
    gate_pad = ((0, 0), (0, GATE_PAD - n_gate))
    w = {
        "norm_mix": vec(norm_mix),
        "w_in": bf(w_in),
        "w_gate": bf(jnp.pad(w_in[:, :, A_COLS + B_MAIN:], ((0, 0),) + gate_pad)),
        "rwkv": [vec(a_mu), vec(a_w0), bf(a_w2), vec(a_a0), bf(a_a2), bf(a_g2), vec(a_k_k), vec(a_k_a),
                 vec(a_r_k), vec(a_ln_w), vec(a_ln_b)],
        "mlstm": [b_conv_w, vec(b_conv_b), vec(jnp.pad(jnp.concatenate([b_i_bias, b_f_bias], axis=1), gate_pad)),
                  vec(b_hn_w)],
        "ffn": (bf(w_out), vec(norm_ffn), bf(w_up), ffn_conv_w, vec(ffn_conv_b), bf(w_down)),
        "norm_final": norm_final.reshape(1, 1, D_MODEL),
    }
    assert len(w["rwkv"]) == N_RWKV_W and len(w["mlstm"]) == N_MLSTM_W

    st_lead_in = (
        state_rwkv_shift[:, :, None, :], state_rwkv_wkv, state_mlstm_conv, state_mlstm_C, state_mlstm_n,
        jnp.broadcast_to(state_mlstm_m[..., None], state_mlstm_m.shape + (B_HEAD_DIM,)), state_ffn_conv,
    )
    meta = jnp.broadcast_to(meta_tokens[None].astype(x_prompt.dtype), (n_prompt, N_META, D_MODEL))
    x_lead = jnp.concatenate([meta, x_sample], axis=0)
    x_main = x_prompt

    main_states, lead_states = [], []
    for l in range(DEPTH):
        final = l == DEPTH - 1
        x_lead, st_lead = _layer(x_lead, l, l, st_lead_in, w, in_tile=n_lead * N_META, mix_chunk=N_META,
                                 mix_subs=1, ffn_streams=n_lead, ffn_tile=N_META, final=final,
                                 zero_streams=n_prompt)
        x_main, st_main = _layer(x_main, l, 0, tuple(s[None] for s in st_lead), w, in_tile=MAIN_PROJ_TILE,
                                 mix_chunk=MAIN_CHUNK, mix_subs=MAIN_SUBCHUNKS, ffn_streams=1, ffn_tile=MAIN_TILE, final=final,
                                 zero_streams=0)
        main_states.append(st_main)
        lead_states.append(st_lead)

    def collect(per_layer, first):
        shift, wkv, bconv, c, n, m, fconv = (jnp.stack([st[i] for st in per_layer])[:, first:] for i in range(7))
        return (shift[:, :, 0, :], wkv, bconv, c, n, m[..., 0], fconv)

    return (x_main, x_lead[n_prompt:]) + collect(main_states, 0) + collect(lead_states, n_prompt)
```

```python
import functools
import math

import numpy as np

import jax
import jax.numpy as jnp
from jax import lax
from jax.experimental import pallas as pl
from jax.experimental.pallas import tpu as pltpu

D_MODEL = 1024
DEPTH = 2
N_META = 16
A_HEADS = 8
A_HEAD_DIM = 64
A_WIDTH = 512
A_DECAY_LORA = 64
A_AAA_LORA = 64
A_GATE_LORA = 128
A_COLS = 1792
B_HEADS = 4
B_HEAD_DIM = 128
B_WIDTH = 512
B_CONV = 4
B_MAIN = 4 * B_WIDTH
GATE_PAD = 128
D_MIX = A_WIDTH + B_WIDTH
D_FF = 2816
FFN_CONV = 3
RMS_EPS = 1e-6
GN_EPS = 64e-5
CARRY = 8
ROW_BLOCK = 16
LANES = 128
MXU_DIM = 256
MIX_STREAMS = 4
FF_BOUNDS = (0, 6 * MXU_DIM, D_FF)
MAIN_TILE = 512
MAIN_PROJ_TILE = 1024
MAIN_CHUNK = 64
MAIN_SUBCHUNKS = 2
VMEM_LIMIT = 56 * 1024 * 1024

F32 = jnp.float32
BF16 = jnp.bfloat16
NT_DIMS = (((1,), (1,)), ((), ()))
TN_DIMS = (((0,), (0,)), ((), ()))


def _dot(a, b):
    return jnp.dot(a, b, preferred_element_type=F32)


def _dot_nt(a, b):
    return lax.dot_general(a, b, NT_DIMS, preferred_element_type=F32)


def _dot_tn(a, b):
    return lax.dot_general(a, b, TN_DIMS, preferred_element_type=F32)


def _sigmoid(x):
    return 0.5 * jnp.tanh(0.5 * x) + 0.5


def _silu(x):
    half = 0.5 * x
    return half + half * jnp.tanh(half)


def _softplus(x):
    return jnp.maximum(x, 0.0) + jnp.log(1.0 + jnp.exp(-jnp.abs(x)))


def _rms(x, g):
    return x * lax.rsqrt(jnp.mean(x * x, axis=-1, keepdims=True) + RMS_EPS) * g


def _tril(n, strict):
    row = lax.broadcasted_iota(jnp.int32, (n, n), 0)
    col = lax.broadcasted_iota(jnp.int32, (n, n), 1)
    return (col < row) if strict else (col <= row)


def _split3(x):
    hi = x.astype(BF16)
    rest = x - hi.astype(F32)
    mid = rest.astype(BF16)
    lo = (rest - mid.astype(F32)).astype(BF16)
    return hi, mid, lo


def _shift_rows(x, prev, s):
    rolled = pltpu.roll(x, s, 0)
    from_prev = lax.broadcasted_iota(jnp.int32, prev.shape, 0) < s
    head = jnp.where(from_prev, pltpu.roll(prev, s, 0), rolled[0:CARRY])
    return jnp.concatenate([head, rolled[CARRY:]], axis=0)


def _rowwise(fn, n_rows, *xs):
    pieces = [fn(*[x[i:i + ROW_BLOCK] for x in xs]) for i in range(0, n_rows, ROW_BLOCK)]
    return [jnp.concatenate(col, axis=0) for col in zip(*pieces)]


def _layer_spec(a, l, resident=False):
    index = lambda *grid: (l,) + (0,) * (a.ndim - 1)
    if resident:
        return pl.BlockSpec((None,) + a.shape[1:], index, pipeline_mode=pl.Buffered(1))
    return pl.BlockSpec((None,) + a.shape[1:], index)


def _state_spec(a, l, nb, skip_first=False):
    block = (lambda i: jnp.maximum(i - 1, 0)) if skip_first else (lambda i: i)
    return pl.BlockSpec((None, nb) + a.shape[2:], lambda i, c: (l, block(i)) + (0,) * (a.ndim - 2))


def _unless(zero_state, load):
    if zero_state is None:
        load()
    else:
        pl.when(jnp.logical_not(zero_state))(load)


def _in_proj_kernel(x_ref, g_ref, w_ref, wg_ref, oa_ref, ob_ref, og_ref):
    h = _rms(x_ref[...], g_ref[...]).astype(BF16)
    oa_ref[...] = _dot(h, w_ref[:, 0:A_COLS])
    ob_ref[...] = _dot(h, w_ref[:, A_COLS:A_COLS + B_MAIN])
    og_ref[...] = _dot(h, wg_ref[...])


def _in_proj(x2d, l, g, w, wg, tm):
    m = x2d.shape[0]
    return pl.pallas_call(
        _in_proj_kernel,
        grid=(m // tm,),
        in_specs=[
            pl.BlockSpec((tm, D_MODEL), lambda i: (i, 0)),
            _layer_spec(g, l), _layer_spec(w, l, resident=True), _layer_spec(wg, l, resident=True),
        ],
        out_specs=[
            pl.BlockSpec((tm, A_COLS), lambda i: (i, 0)),
            pl.BlockSpec((tm, B_MAIN), lambda i: (i, 0)),
            pl.BlockSpec((tm, GATE_PAD), lambda i: (i, 0)),
        ],
        out_shape=[
            jax.ShapeDtypeStruct((m, A_COLS), F32),
            jax.ShapeDtypeStruct((m, B_MAIN), F32),
            jax.ShapeDtypeStruct((m, GATE_PAD), F32),
        ],
        compiler_params=pltpu.CompilerParams(
            dimension_semantics=("arbitrary",), vmem_limit_bytes=VMEM_LIMIT),
        name="in_proj",
    )(x2d, g, w, wg)


def _rwkv_heads_per_group(chunk):
    return min(A_HEADS, MXU_DIM // chunk, MXU_DIM // A_HEAD_DIM)


def _mix_consts(chunk, streams):
    hg = _rwkv_heads_per_group(chunk)
    hl, w, rows = hg * chunk, hg * A_HEAD_DIM, streams * chunk
    ix = lambda n: (np.arange(n)[:, None], np.arange(n)[None, :])
    r, c = ix(MXU_DIM)
    seg = r // A_HEAD_DIM == c // A_HEAD_DIM
    r, c = ix(rows)
    tril = (r // chunk == c // chunk) & (c <= r)
    r, c = np.arange(2 * chunk)[:, None], np.arange(hl)[None, :]
    amask = np.where(r < chunk, c % chunk < r, c % chunk <= r - chunk)
    r, c = np.arange(chunk)[:, None], np.arange(hl)[None, :]
    eye = c % chunk == r
    r, c = np.arange(hl)[:, None], np.arange(w)[None, :]
    bdl = r // chunk == c // A_HEAD_DIM
    r, c = ix(hl)
    bdsq = r // chunk == c // chunk
    r, c = ix(w)
    smask = r // A_HEAD_DIM == c // A_HEAD_DIM

    def pack(dtype, **masks):
        width = max(m.shape[1] for m in masks.values())
        rows = np.cumsum([0] + [m.shape[0] for m in masks.values()])
        slab = np.concatenate([np.pad(m, ((0, 0), (0, width - m.shape[1]))) for m in masks.values()], axis=0)
        return jnp.asarray(slab, dtype), tuple((name, (int(r0),) + m.shape) for (name, m), r0 in zip(masks.items(), rows))

    slab16, where16 = pack(BF16, seg=seg, tril=tril, bdl=bdl, bdsq=bdsq)
    slab32, where32 = pack(F32, amask=amask, eye=eye, smask=smask)
    return (slab16, slab32), (where16, where32)


def _rwkv_stages(p_ref, shift_ref, s0_ref, wts, consts, y_ref, shift_out_ref, s_out_ref, pbuf, bds,
                 *, chunk, hg, row0, first, last, zero_state):
    mu_ref, w0_ref, w2_ref, a0_ref, a2_ref, g2_ref, kk_ref, ka_ref, rk_ref, lnw_ref, lnb_ref = wts
    L = chunk
    nb = p_ref.shape[0]
    rows = nb * L
    hl = hg * L
    wid = hg * A_HEAD_DIM
    groups = A_HEADS // hg
    diag = lambda h: (h // hg, slice((h % hg) * A_HEAD_DIM, (h % hg + 1) * A_HEAD_DIM))
    win = slice(row0, row0 + L)

    if first is not None:
        @pl.when(first)
        def _():
            bds[...] = jnp.zeros(bds.shape, F32)
            pbuf[...] = jnp.zeros(pbuf.shape, F32)

            def load():
                for b in range(nb):
                    pbuf[b, CARRY - 1:CARRY, :] = shift_ref[b]
                    for h in range(A_HEADS):
                        g, sl = diag(h)
                        bds[b, g, sl, sl] = s0_ref[b, h]
            _unless(zero_state, load)

    ps, prevs = [], []
    for b in range(nb):
        pb = p_ref[b, win, :]
        prevs.append(_shift_rows(pb, pbuf[b], 1))
        pbuf[b] = pb[L - CARRY:L, :]
        shift_out_ref[b] = pb[L - 1:L, :]
        ps.append(pb)
    i1, i2, i3 = A_WIDTH, 2 * A_WIDTH, 3 * A_WIDTH
    i4, i5 = i3 + A_DECAY_LORA, i3 + A_DECAY_LORA + A_AAA_LORA
    mu = mu_ref[...]

    def token_shift(p, prev):
        pm = p + (prev - p) * mu
        return (pm[:, :i1], pm[:, i1:i2], pm[:, i2:i3], jnp.tanh(pm[:, i3:i4]).astype(BF16),
                pm[:, i4:i5].astype(BF16), _sigmoid(pm[:, i5:]).astype(BF16))

    r, k, v, wl16, al16, gl16 = _rowwise(token_shift, rows, jnp.concatenate(ps, axis=0), jnp.concatenate(prevs, axis=0))
    yield
    seg = consts["seg"]()
    half = A_WIDTH // 2
    segsum = lambda t16: jnp.concatenate([_dot(t16[:, :half], seg), _dot(t16[:, half:], seg)], axis=1)
    z = w0_ref[...] + _dot(wl16, w2_ref[...])
    a_pre = a0_ref[...] + _dot(al16, a2_ref[...])
    gate = _dot(gl16, g2_ref[...])
    yield
    k_k, k_a, r_k = kk_ref[...], ka_ref[...], rk_ref[...]

    def decay_and_keys(k, z, a_pre):
        lw = -math.exp(-0.5) * _sigmoid(z)
        lw_hi = lw.astype(BF16)
        a = _sigmoid(a_pre)
        kk0 = k * k_k
        return (lw, lw_hi, (lw - lw_hi.astype(F32)).astype(BF16), a, kk0, (kk0 * kk0).astype(BF16),
                k * (1.0 + (a - 1.0) * k_a))

    lw, lw_hi, lw_lo, a, kk0, kk_sq16, kmod = _rowwise(decay_and_keys, rows, k, z, a_pre)
    yield
    tril = consts["tril"]()
    cum = _dot(tril, lw_hi) + _dot(tril, lw_lo)
    kk_ss = segsum(kk_sq16)
    yield

    rs = lambda b: slice(b * L, (b + 1) * L)
    cum_last = [cum[(b + 1) * L - 1:(b + 1) * L] for b in range(nb)]
    g_last = [jnp.exp(cum_last[b]) for b in range(nb)]

    def scaled_operands(b):
        def fn(r, v, kk0, kk_ss, a, kmod, lw, cum):
            kk = kk0 * lax.rsqrt(kk_ss + 1e-12)
            bvec = kk * a
            g_inv = jnp.exp(-cum)
            g_tail = g_last[b] * g_inv
            return ((kk * jnp.exp(cum - lw)).astype(BF16), (r * jnp.exp(cum)).astype(BF16),
                    (kmod * g_inv).astype(BF16), (bvec * g_inv).astype(BF16), v.astype(BF16),
                    (kmod * g_tail).astype(BF16), (bvec * g_tail).astype(BF16), (r * kmod * r_k).astype(BF16))
        return fn

    per_stream = [_rowwise(scaled_operands(b), L, *[t[rs(b)] for t in (r, v, kk0, kk_ss, a, kmod, lw, cum)])
                  for b in range(nb)]
    kkg_s, rg_s, kd_s, bd_s, v_s, kt, bt, bonus_s = zip(*per_stream)
    bonus_ss = segsum(jnp.concatenate(bonus_s, axis=0))
    yield PROLOGUE_DONE

    keep = consts["amask"]() > 0.5
    eye = consts["eye"]()
    bdl = consts["bdl"]()
    bdsq = consts["bdsq"]()
    smask = consts["smask"]()
    def block_diag(t, mask, seg):
        per_tile = max(1, LANES // seg)
        tile = per_tile * seg
        rows = []
        for h in range(hg):
            cols = [t[:, j:j + tile] * mask[h * L:(h + 1) * L, j:j + tile] if j // tile == h // per_tile
                    else jnp.zeros((L, tile), t.dtype) for j in range(0, t.shape[1], tile)]
            rows.append(cols[0] if len(cols) == 1 else jnp.concatenate(cols, axis=1))
        return jnp.concatenate(rows, axis=0)

    lane_bd = lambda t: block_diag(t, bdl, A_HEAD_DIM)
    sq_bd = lambda t: block_diag(t, bdsq, L)

    chains = [(b, g) for b in range(nb) for g in range(groups)]
    ls = lambda g: slice(g * wid, (g + 1) * wid)
    each = lambda f: [f(i, b, g) for i, (b, g) in enumerate(chains)]

    kkg = each(lambda i, b, g: kkg_s[b][:, ls(g)])
    rg = each(lambda i, b, g: rg_s[b][:, ls(g)])
    v16 = each(lambda i, b, g: v_s[b][:, ls(g)])
    lhs = each(lambda i, b, g: jnp.concatenate([kkg[i], rg[i]], axis=0))
    a_k = each(lambda i, b, g: jnp.where(keep, _dot_nt(lhs[i], lane_bd(kd_s[b][:, ls(g)])), 0.0))
    yield
    a_b = each(lambda i, b, g: jnp.where(keep, _dot_nt(lhs[i], lane_bd(bd_s[b][:, ls(g)])), 0.0))
    a_kk = each(lambda i, b, g: a_k[i][:L].astype(BF16))
    a_rk = each(lambda i, b, g: a_k[i][L:].astype(BF16))
    a_rb = each(lambda i, b, g: a_b[i][L:].astype(BF16))
    yield

    npow = each(lambda i, b, g: -a_b[i][:L])
    tinv = each(lambda i, b, g: eye + npow[i])
    n16 = each(lambda i, b, g: npow[i].astype(BF16))
    npow = each(lambda i, b, g: _dot(n16[i], sq_bd(n16[i])))
    yield
    span = 2
    while span < L:
        n16 = each(lambda i, b, g: npow[i].astype(BF16))
        t_bd = each(lambda i, b, g: sq_bd(tinv[i].astype(BF16)))
        if 2 * span < L:
            both = each(lambda i, b, g: _dot(n16[i], jnp.concatenate([sq_bd(n16[i]), t_bd[i]], axis=1)))
            npow = each(lambda i, b, g: both[i][:, :hl])
            prod = each(lambda i, b, g: both[i][:, hl:])
        else:
            prod = each(lambda i, b, g: _dot(n16[i], t_bd[i]))
        tinv = each(lambda i, b, g: tinv[i] + prod[i])
        span *= 2
        yield

    yield READS_STATE
    s16 = each(lambda i, b, g: bds[b, g].astype(BF16))
    v_bd = each(lambda i, b, g: lane_bd(v16[i]))
    rhs = each(lambda i, b, g: (_dot_nt(kkg[i], s16[i]) + _dot(a_kk[i], v_bd[i])).astype(BF16))
    yield
    u16 = each(lambda i, b, g: _dot(tinv[i].astype(BF16), lane_bd(rhs[i])).astype(BF16))
    yield
    yc_ = each(lambda i, b, g: _dot_nt(rg[i], s16[i]) + _dot(a_rk[i], v_bd[i]) - _dot(a_rb[i], lane_bd(u16[i])))
    yield
    upd = each(lambda i, b, g: _dot_tn(jnp.concatenate([v16[i], -u16[i]], axis=0),
                                       jnp.concatenate([kt[b][:, ls(g)], bt[b][:, ls(g)]], axis=0)))
    for i, (b, g) in enumerate(chains):
        bds[b, g] = bds[b, g] * g_last[b][:, ls(g)] + upd[i] * smask
    yield WROTE_STATE
    ys = [jnp.concatenate(yc_[b * groups:(b + 1) * groups], axis=1) for b in range(nb)]
    y = jnp.concatenate(ys, axis=0)

    inv_d = 1.0 / A_HEAD_DIM
    y_sum = segsum(y.astype(BF16))

    def centre(y, y_sum):
        yc = y - y_sum * inv_d
        return yc, (yc * yc).astype(BF16)

    yc, yc_sq16 = _rowwise(centre, rows, y, y_sum)
    var_sum = segsum(yc_sq16)
    yield
    ln_w, ln_b = lnw_ref[...], lnb_ref[...]

    def finish(yc, var_sum, bonus_ss, v, gate):
        return ((yc * lax.rsqrt(var_sum * inv_d + GN_EPS) * ln_w + ln_b + bonus_ss * v) * gate,)

    out, = _rowwise(finish, rows, yc, var_sum, bonus_ss, v, gate)
    for b in range(nb):
        y_ref[b, win, 0:A_WIDTH] = out[b * L:(b + 1) * L]

    if last is not None:
        @pl.when(last)
        def _():
            for b in range(nb):
                for h in range(A_HEADS):
                    g, sl = diag(h)
                    s_out_ref[b, h] = bds[b, g, sl, sl]


def _mlstm_stages(pb_ref, pg_ref, conv0_ref, c0_ref, n0_ref, m0_ref, wts, consts,
                  y_ref, conv_out_ref, c_ref, n_ref, m_ref, xbuf, *, chunk, row0, first, zero_state):
    cw_ref, cb_ref, gb_ref, hnw_ref = wts
    win = slice(row0, row0 + chunk)
    L = chunk
    nb = pb_ref.shape[0]
    rows = nb * L
    hist = B_CONV - 1

    if first is not None:
        @pl.when(first)
        def _():
            xbuf[...] = jnp.zeros(xbuf.shape, F32)
            c_ref[...] = jnp.zeros(c_ref.shape, F32)
            n_ref[...] = jnp.zeros(n_ref.shape, F32)
            m_ref[...] = jnp.zeros(m_ref.shape, F32)

            def load():
                for b in range(nb):
                    xbuf[b, CARRY - hist:CARRY, :] = conv0_ref[b]
                c_ref[...] = c0_ref[...]
                n_ref[...] = n0_ref[...]
                m_ref[...] = m0_ref[...]
            _unless(zero_state, load)

    qk = []
    for b in range(nb):
        x = pb_ref[b, win, 0:2 * B_WIDTH]
        prev = xbuf[b]
        acc = cb_ref[...] + cw_ref[hist:hist + 1, :] * x
        for j in range(hist):
            acc = acc + cw_ref[j:j + 1, :] * _shift_rows(x, prev, hist - j)
        xbuf[b] = x[L - CARRY:L, :]
        conv_out_ref[b] = x[L - hist:L, :]
        qk.append(_silu(acc))
        yield

    gates = jnp.concatenate([pg_ref[b, win, :] for b in range(nb)], axis=0) + gb_ref[...]
    lane = lax.broadcasted_iota(jnp.int32, (rows, GATE_PAD), 1)
    tril = consts["tril"]()
    cum = sum(_dot(tril, part) for part in _split3(-_softplus(-gates)))
    gcols = jnp.where(lane < B_HEADS, gates, cum)
    grows = gcols.T
    causal = _tril(L, False)
    yield

    chains = [(b, h) for b in range(nb) for h in range(B_HEADS)]
    each = lambda f: [f(i, b, h) for i, (b, h) in enumerate(chains)]
    rs = lambda b: slice(b * L, (b + 1) * L)
    hs = lambda h, base=0: slice(base + h * B_HEAD_DIM, base + (h + 1) * B_HEAD_DIM)

    q = each(lambda i, b, h: qk[b][:, hs(h)])
    k = each(lambda i, b, h: qk[b][:, hs(h, B_WIDTH)] * (B_HEAD_DIM ** -0.5))
    q16 = each(lambda i, b, h: q[i].astype(BF16))
    v16 = each(lambda i, b, h: pb_ref[b, win, hs(h, 2 * B_WIDTH)].astype(BF16))
    li_col = each(lambda i, b, h: gcols[rs(b), h:h + 1])
    b_col = each(lambda i, b, h: gcols[rs(b), B_HEADS + h:B_HEADS + h + 1])
    li_row = each(lambda i, b, h: grows[h:h + 1, rs(b)])
    b_row = each(lambda i, b, h: grows[B_HEADS + h:B_HEADS + h + 1, rs(b)])
    yield READS_STATE
    c_prev = each(lambda i, b, h: c_ref[b, h])
    n_prev = each(lambda i, b, h: n_ref[b, h:h + 1, :])
    m_prev = each(lambda i, b, h: m_ref[b, h:h + 1, 0:1])

    dm = each(lambda i, b, h: jnp.where(causal, b_col[i] - b_row[i] + li_row[i], -jnp.inf))
    inter = each(lambda i, b, h: b_col[i] + m_prev[i])
    mt = each(lambda i, b, h: jnp.maximum(inter[i], jnp.max(dm[i], axis=-1, keepdims=True)))
    wo = each(lambda i, b, h: jnp.exp(inter[i] - mt[i]))
    yield
    s = each(lambda i, b, h: _dot_nt(q16[i], k[i].astype(BF16)) * jnp.exp(dm[i] - mt[i]))
    yield
    num = each(lambda i, b, h: wo[i] * _dot(q16[i], c_prev[i].astype(BF16)) + _dot(s[i].astype(BF16), v16[i]))
    yield
    den = each(lambda i, b, h: wo[i] * jnp.sum(q[i] * n_prev[i], axis=-1, keepdims=True)
               + jnp.sum(s[i], axis=-1, keepdims=True))
    hh = each(lambda i, b, h: num[i] * (1.0 / jnp.maximum(jnp.abs(den[i]), jnp.exp(-mt[i]))))
    yield

    m_new = each(lambda i, b, h: mt[i][L - 1:L, :])
    b_last = each(lambda i, b, h: b_col[i][L - 1:L, :])
    kw = each(lambda i, b, h: k[i] * jnp.exp(b_last[i] - b_col[i] + li_col[i] - m_new[i]))
    dec = each(lambda i, b, h: jnp.exp(b_last[i] + m_prev[i] - m_new[i]))
    yield
    c_new = each(lambda i, b, h: dec[i] * c_prev[i] + _dot_tn(kw[i].astype(BF16), v16[i]))
    n_new = each(lambda i, b, h: dec[i] * n_prev[i] + jnp.sum(kw[i], axis=0, keepdims=True))
    yield
    out = each(lambda i, b, h: hh[i] * lax.rsqrt(jnp.mean(hh[i] * hh[i], axis=-1, keepdims=True) + RMS_EPS)
               * hnw_ref[:, hs(h)] * _sigmoid(pb_ref[b, win, hs(h, 3 * B_WIDTH)]))
    for i, (b, h) in enumerate(chains):
        c_ref[b, h] = c_new[i]
        n_ref[b, h:h + 1, :] = n_new[i]
        m_ref[b, h:h + 1, :] = jnp.broadcast_to(m_new[i], (1, B_HEAD_DIM))
        y_ref[b, win, hs(h, A_WIDTH)] = out[i]


PROLOGUE_DONE, READS_STATE, WROTE_STATE, DONE = "prologue_done", "reads_state", "wrote_state", "done"
N_RWKV_W = 11
N_MLSTM_W = 4
N_CONSTS = 2


def _mix_kernel(*refs, chunk, hg, where, zero_first):
    it = iter(refs)
    take = lambda n: [next(it) for _ in range(n)]
    pa_ref, pb_ref, pg_ref, shift_ref, s0_ref, conv0_ref, c0_ref, n0_ref, m0_ref = take(9)
    rwkv_w, mlstm_w, slabs = take(N_RWKV_W), take(N_MLSTM_W), take(N_CONSTS)
    consts = {name: functools.partial(lambda ref, r0, rows, cols: ref[r0:r0 + rows, 0:cols], ref, *at)
              for ref, masks in zip(slabs, where) for name, at in masks}
    y_ref, shift_out_ref, s_out_ref, conv_out_ref, c_ref, n_ref, m_ref, pbuf, bds, xbuf = take(10)
    first = pl.program_id(1) == 0
    last = pl.program_id(1) == pl.num_programs(1) - 1
    zero_state = pl.program_id(0) == 0 if zero_first else None
    subs = pa_ref.shape[1] // chunk
    bodies = []
    for sub in range(subs):
        is_first = first if sub == 0 else None
        bodies.append([
            _rwkv_stages(pa_ref, shift_ref, s0_ref, rwkv_w, consts, y_ref, shift_out_ref, s_out_ref, pbuf, bds,
                         chunk=chunk, hg=hg, row0=sub * chunk, first=is_first,
                         last=last if sub == subs - 1 else None, zero_state=zero_state),
            _mlstm_stages(pb_ref, pg_ref, conv0_ref, c0_ref, n0_ref, m0_ref, mlstm_w, consts,
                          y_ref, conv_out_ref, c_ref, n_ref, m_ref, xbuf, chunk=chunk, row0=sub * chunk,
                          first=is_first, zero_state=zero_state),
        ])
    live = [(sub, kind) for sub in range(subs) for kind in range(2)]
    wrote, waiting, started = set(), set(), {(0, 0), (0, 1)}
    while live:
        for key in list(live):
            sub, kind = key
            if key not in started or (key in waiting and (sub - 1, kind) not in wrote):
                continue
            waiting.discard(key)
            mark = next(bodies[sub][kind], DONE)
            if mark == PROLOGUE_DONE:
                started.add((sub + 1, 0))
                if sub > 0:
                    started.add((sub, 1))
            elif mark == READS_STATE and sub > 0:
                waiting.add(key)
            elif mark in (WROTE_STATE, DONE):
                wrote.add(key)
                if mark == DONE:
                    live.remove(key)


def _mix(pa, pb, pg, l, sl, states, rwkv_w, mlstm_w, chunk, subs, zero_first):
    bn, t, _ = pa.shape
    nb = MIX_STREAMS
    hg = _rwkv_heads_per_group(chunk)
    hist = B_CONV - 1
    consts, where = _mix_consts(chunk, nb)
    full = lambda a: pl.BlockSpec(a.shape, lambda i, c: (0,) * a.ndim)
    tok = lambda n: pl.BlockSpec((nb, subs * chunk, n), lambda i, c: (i, c, 0))
    per_stream = lambda *dims: pl.BlockSpec((nb,) + dims, lambda i, c: (i,) + (0,) * len(dims))
    state_shapes = [(1, A_COLS), (A_HEADS, A_HEAD_DIM, A_HEAD_DIM), (hist, 2 * B_WIDTH),
                    (B_HEADS, B_HEAD_DIM, B_HEAD_DIM), (B_HEADS, B_HEAD_DIM), (B_HEADS, B_HEAD_DIM)]
    return pl.pallas_call(
        functools.partial(_mix_kernel, chunk=chunk, hg=hg, where=where, zero_first=zero_first),
        grid=(bn // nb, t // (subs * chunk)),
        in_specs=[tok(A_COLS), tok(B_MAIN), tok(GATE_PAD)]
        + [_state_spec(a, sl, nb, zero_first) for a in states]
        + [_layer_spec(a, l) for a in rwkv_w + mlstm_w] + [full(a) for a in consts],
        out_specs=[tok(D_MIX)] + [per_stream(*dims) for dims in state_shapes],
        out_shape=[jax.ShapeDtypeStruct((bn, t, D_MIX), F32)]
        + [jax.ShapeDtypeStruct((bn,) + dims, F32) for dims in state_shapes],
        scratch_shapes=[
            pltpu.VMEM((nb, CARRY, A_COLS), F32),
            pltpu.VMEM((nb, A_HEADS // hg, hg * A_HEAD_DIM, hg * A_HEAD_DIM), F32),
            pltpu.VMEM((nb, CARRY, 2 * B_WIDTH), F32),
        ],
        compiler_params=pltpu.CompilerParams(
            dimension_semantics=("arbitrary", "arbitrary"), vmem_limit_bytes=VMEM_LIMIT),
        name="mixers",
    )(pa, pb, pg, *states, *rwkv_w, *mlstm_w, *consts)


def _ffn_kernel(x_ref, y_ref, f0_ref, wout_ref, g_ref, wup_ref, cw_ref, cb_ref,
                wdown_ref, gfin_ref, o_ref, fout_ref, ubuf, *, tile, final, zero_streams):
    nb = x_ref.shape[0]
    hist = FFN_CONV - 1
    rows_of = lambda ref: ref[0] if nb == 1 else jnp.concatenate([ref[b] for b in range(nb)], axis=0)

    @pl.when(pl.program_id(1) == 0)
    def _():
        ubuf[...] = jnp.zeros(ubuf.shape, F32)
        for b in range(zero_streams, nb):
            ubuf[b, CARRY - hist:CARRY, :] = f0_ref[b - zero_streams]

    x1 = rows_of(x_ref) + _dot(rows_of(y_ref).astype(BF16), wout_ref[...])
    h2 = _rms(x1, g_ref[...]).astype(BF16)
    acts = []
    for lo, hi in zip(FF_BOUNDS[:-1], FF_BOUNDS[1:]):
        cols = slice(lo, hi)
        u = _dot(h2, wup_ref[:, cols])
        gate = _dot(h2, wup_ref[:, D_FF + lo:D_FF + hi])
        convs = []
        for b in range(nb):
            ub = u[b * tile:(b + 1) * tile]
            prev = ubuf[b, :, cols]
            acc = cb_ref[:, cols] + cw_ref[hist:hist + 1, cols] * ub
            for j in range(hist):
                acc = acc + cw_ref[j:j + 1, cols] * _shift_rows(ub, prev, hist - j)
            ubuf[b, :, cols] = ub[tile - CARRY:tile]
            fout_ref[b, :, cols] = ub[tile - hist:tile]
            convs.append(acc)
        conv = convs[0] if nb == 1 else jnp.concatenate(convs, axis=0)
        acts.append((_silu(conv) * gate).astype(BF16))
    x2 = x1 + _dot(jnp.concatenate(acts, axis=1), wdown_ref[...])
    out = _rms(x2, gfin_ref[...]) if final else x2
    for b in range(nb):
        o_ref[b] = out[b * tile:(b + 1) * tile]


def _ffn(x, y, l, sl, f0, wts, gfin, nb, tile, final, zero_streams):
    bn, t, _ = x.shape
    hist = FFN_CONV - 1
    tok = pl.BlockSpec((nb, tile, D_MODEL), lambda i, c: (i, c, 0))
    fspec = pl.BlockSpec((nb, hist, D_FF), lambda i, c: (i, 0, 0))
    return pl.pallas_call(
        functools.partial(_ffn_kernel, tile=tile, final=final, zero_streams=zero_streams),
        grid=(bn // nb, t // tile),
        in_specs=[tok, tok, _state_spec(f0, sl, nb - zero_streams)]
        + [_layer_spec(a, l, resident=True) for a in wts] + [_layer_spec(gfin, 0, resident=True)],
        out_specs=[tok, fspec],
        out_shape=[
            jax.ShapeDtypeStruct((bn, t, D_MODEL), F32),
            jax.ShapeDtypeStruct((bn, hist, D_FF), F32),
        ],
        scratch_shapes=[pltpu.VMEM((nb, CARRY, D_FF), F32)],
        compiler_params=pltpu.CompilerParams(
            dimension_semantics=("arbitrary", "arbitrary"), vmem_limit_bytes=VMEM_LIMIT),
        name="out_ffn",
    )(x, y, f0, *wts, gfin)


def _layer(x, l, sl, st, w, *, in_tile, mix_chunk, mix_subs, ffn_streams, ffn_tile, final, zero_streams):
    bn, t, _ = x.shape
    pa, pb, pg = _in_proj(x.reshape(bn * t, D_MODEL), l, w["norm_mix"], w["w_in"], w["w_gate"], in_tile)
    pa = pa.reshape(bn, t, A_COLS)
    pb = pb.reshape(bn, t, B_MAIN)
    pg = pg.reshape(bn, t, GATE_PAD)
    y, *mix_states = _mix(pa, pb, pg, l, sl, st[:6], w["rwkv"], w["mlstm"], mix_chunk, mix_subs,
                          zero_streams > 0)
    x, fconv1 = _ffn(x, y, l, sl, st[6], w["ffn"], w["norm_final"], ffn_streams, ffn_tile, final, zero_streams)
    return x, tuple(mix_states) + (fconv1,)


def kernel(x_prompt, x_sample, state_rwkv_shift, state_rwkv_wkv, state_mlstm_conv, state_mlstm_C,
           state_mlstm_n, state_mlstm_m, state_ffn_conv, meta_tokens, norm_mix, w_in, a_mu, a_w0, a_w2,
           a_a0, a_a2, a_g2, a_k_k, a_k_a, a_r_k, a_ln_w, a_ln_b, b_conv_w, b_conv_b, b_i_bias, b_f_bias,
           b_hn_w, w_out, norm_ffn, w_up, ffn_conv_w, ffn_conv_b, w_down, norm_final):
    n_prompt = x_prompt.shape[0]
    n_sample = x_sample.shape[0]
    n_lead = n_prompt + n_sample
    assert x_sample.shape[1] == N_META
    assert n_prompt == MIX_STREAMS and n_lead % MIX_STREAMS == 0

    bf = lambda a: a.astype(BF16)
    vec = lambda a: a[:, None, :]
    n_gate = 2 * B_HEADS
    gate_pad = ((0, 0), (0, GATE_PAD - n_gate))
    w = {
        "norm_mix": vec(norm_mix),
        "w_in": bf(w_in),
        "w_gate": bf(jnp.pad(w_in[:, :, A_COLS + B_MAIN:], ((0, 0),) + gate_pad)),
        "rwkv": [vec(a_mu), vec(a_w0), bf(a_w2), vec(a_a0), bf(a_a2), bf(a_g2), vec(a_k_k), vec(a_k_a),
                 vec(a_r_k), vec(a_ln_w), vec(a_ln_b)],
        "mlstm": [b_conv_w, vec(b_conv_b), vec(jnp.pad(jnp.concatenate([b_i_bias, b_f_bias], axis=1), gate_pad)),
                  vec(b_hn_w)],
        "ffn": (bf(w_out), vec(norm_ffn), bf(w_up), ffn_conv_w, vec(ffn_conv_b), bf(w_down)),
        "norm_final": norm_final.reshape(1, 1, D_MODEL),
    }
    assert len(w["rwkv"]) == N_RWKV_W and len(w["mlstm"]) == N_MLSTM_W

    st_lead_in = (
        state_rwkv_shift[:, :, None, :], state_rwkv_wkv, state_mlstm_conv, state_mlstm_C, state_mlstm_n,
        jnp.broadcast_to(state_mlstm_m[..., None], state_mlstm_m.shape + (B_HEAD_DIM,)), state_ffn_conv,
    )
    meta = jnp.broadcast_to(meta_tokens[None].astype(x_prompt.dtype), (n_prompt, N_META, D_MODEL))
    x_lead = jnp.concatenate([meta, x_sample], axis=0)
    x_main = x_prompt

    main_states, lead_states = [], []
    for l in range(DEPTH):
        final = l == DEPTH - 1
        x_lead, st_lead = _layer(x_lead, l, l, st_lead_in, w, in_tile=n_lead * N_META, mix_chunk=N_META,
                                 mix_subs=1, ffn_streams=n_lead, ffn_tile=N_META, final=final,
                                 zero_streams=n_prompt)
        x_main, st_main = _layer(x_main, l, 0, tuple(s[None] for s in st_lead), w, in_tile=MAIN_PROJ_TILE,
                                 mix_chunk=MAIN_CHUNK, mix_subs=MAIN_SUBCHUNKS, ffn_streams=1, ffn_tile=MAIN_TILE, final=final,
                                 zero_streams=0)
        main_states.append(st_main)
        lead_states.append(st_lead)

    def collect(per_layer, first):
        shift, wkv, bconv, c, n, m, fconv = (jnp.stack([st[i] for st in per_layer])[:, first:] for i in range(7))
        return (shift[:, :, 0, :], wkv, bconv, c, n, m[..., 0], fconv)

    return (x_main, x_lead[n_prompt:]) + collect(main_states, 0) + collect(lead_states, n_prompt)
```

```python
import functools
import math

import numpy as np

import jax
import jax.numpy as jnp
from jax import lax
from jax.experimental import pallas as pl
from jax.experimental.pallas import tpu as pltpu

D_MODEL = 1024
DEPTH = 2
N_META = 16
A_HEADS = 8
A_HEAD_DIM = 64
A_WIDTH = 512
A_DECAY_LORA = 64
A_AAA_LORA = 64
A_GATE_LORA = 128
A_COLS = 1792
B_HEADS = 4
B_HEAD_DIM = 128
B_WIDTH = 512
B_CONV = 4
B_MAIN = 4 * B_WIDTH
GATE_PAD = 128
D_MIX = A_WIDTH + B_WIDTH
D_FF = 2816
FFN_CONV = 3
RMS_EPS = 1e-6
GN_EPS = 64e-5
CARRY = 8
ROW_BLOCK = 16
LANES = 128
MXU_DIM = 256
MIX_STREAMS = 4
FF_BOUNDS = (0, 6 * MXU_DIM, D_FF)
MAIN_TILE = 512
MAIN_PROJ_TILE = 1024
MAIN_CHUNK = 64
MAIN_SUBCHUNKS = 2
VMEM_LIMIT = 56 * 1024 * 1024

F32 = jnp.float32
BF16 = jnp.bfloat16
NT_DIMS = (((1,), (1,)), ((), ()))
TN_DIMS = (((0,), (0,)), ((), ()))


def _dot(a, b):
    return jnp.dot(a, b, preferred_element_type=F32)


def _dot_nt(a, b):
    return lax.dot_general(a, b, NT_DIMS, preferred_element_type=F32)


def _dot_tn(a, b):
    return lax.dot_general(a, b, TN_DIMS, preferred_element_type=F32)


def _sigmoid(x):
    return 0.5 * jnp.tanh(0.5 * x) + 0.5


def _silu(x):
    half = 0.5 * x
    return half + half * jnp.tanh(half)


def _softplus(x):
    return jnp.maximum(x, 0.0) + jnp.log(1.0 + jnp.exp(-jnp.abs(x)))


def _rms(x, g):
    return x * lax.rsqrt(jnp.mean(x * x, axis=-1, keepdims=True) + RMS_EPS) * g


def _tril(n, strict):
    row = lax.broadcasted_iota(jnp.int32, (n, n), 0)
    col = lax.broadcasted_iota(jnp.int32, (n, n), 1)
    return (col < row) if strict else (col <= row)


def _split3(x):
    hi = x.astype(BF16)
    rest = x - hi.astype(F32)
    mid = rest.astype(BF16)
    lo = (rest - mid.astype(F32)).astype(BF16)
    return hi, mid, lo


def _shift_rows(x, prev, s):
    rolled = pltpu.roll(x, s, 0)
    from_prev = lax.broadcasted_iota(jnp.int32, prev.shape, 0) < s
    head = jnp.where(from_prev, pltpu.roll(prev, s, 0), rolled[0:CARRY])
    return jnp.concatenate([head, rolled[CARRY:]], axis=0)


def _rowwise(fn, n_rows, *xs):
    pieces = [fn(*[x[i:i + ROW_BLOCK] for x in xs]) for i in range(0, n_rows, ROW_BLOCK)]
    return [jnp.concatenate(col, axis=0) for col in zip(*pieces)]


def _layer_spec(a, l, resident=False):
    index = lambda *grid: (l,) + (0,) * (a.ndim - 1)
    if resident:
        return pl.BlockSpec((None,) + a.shape[1:], index, pipeline_mode=pl.Buffered(1))
    return pl.BlockSpec((None,) + a.shape[1:], index)


def _state_spec(a, l, nb, skip_first=False):
    block = (lambda i: jnp.maximum(i - 1, 0)) if skip_first else (lambda i: i)
    return pl.BlockSpec((None, nb) + a.shape[2:], lambda i, c: (l, block(i)) + (0,) * (a.ndim - 2))


def _unless(zero_state, load):
    if zero_state is None:
        load()
    else:
        pl.when(jnp.logical_not(zero_state))(load)


def _in_proj_kernel(x_ref, g_ref, w_ref, wg_ref, oa_ref, ob_ref, og_ref):
    h = _rms(x_ref[...], g_ref[...]).astype(BF16)
    oa_ref[...] = _dot(h, w_ref[:, 0:A_COLS])
    ob_ref[...] = _dot(h, w_ref[:, A_COLS:A_COLS + B_MAIN])
    og_ref[...] = _dot(h, wg_ref[...])


def _in_proj(x2d, l, g, w, wg, tm):
    m = x2d.shape[0]
    return pl.pallas_call(
        _in_proj_kernel,
        grid=(m // tm,),
        in_specs=[
            pl.BlockSpec((tm, D_MODEL), lambda i: (i, 0)),
            _layer_spec(g, l), _layer_spec(w, l, resident=True), _layer_spec(wg, l, resident=True),
        ],
        out_specs=[
            pl.BlockSpec((tm, A_COLS), lambda i: (i, 0)),
            pl.BlockSpec((tm, B_MAIN), lambda i: (i, 0)),
            pl.BlockSpec((tm, GATE_PAD), lambda i: (i, 0)),
        ],
        out_shape=[
            jax.ShapeDtypeStruct((m, A_COLS), F32),
            jax.ShapeDtypeStruct((m, B_MAIN), F32),
            jax.ShapeDtypeStruct((m, GATE_PAD), F32),
        ],
        compiler_params=pltpu.CompilerParams(
            dimension_semantics=("arbitrary",), vmem_limit_bytes=VMEM_LIMIT),
        name="in_proj",
    )(x2d, g, w, wg)


def _rwkv_heads_per_group(chunk):
    return min(A_HEADS, MXU_DIM // chunk, MXU_DIM // A_HEAD_DIM)


def _mix_consts(chunk, streams):
    hg = _rwkv_heads_per_group(chunk)
    hl, w, rows = hg * chunk, hg * A_HEAD_DIM, streams * chunk
    ix = lambda n: (np.arange(n)[:, None], np.arange(n)[None, :])
    r, c = ix(MXU_DIM)
    seg = r // A_HEAD_DIM == c // A_HEAD_DIM
    r, c = ix(rows)
    tril = (r // chunk == c // chunk) & (c <= r)
    r, c = np.arange(2 * chunk)[:, None], np.arange(hl)[None, :]
    amask = np.where(r < chunk, c % chunk < r, c % chunk <= r - chunk)
    r, c = np.arange(chunk)[:, None], np.arange(hl)[None, :]
    eye = c % chunk == r
    r, c = np.arange(hl)[:, None], np.arange(w)[None, :]
    bdl = r // chunk == c // A_HEAD_DIM
    r, c = ix(hl)
    bdsq = r // chunk == c // chunk
    r, c = ix(w)
    smask = r // A_HEAD_DIM == c // A_HEAD_DIM

    def pack(dtype, **masks):
        width = max(m.shape[1] for m in masks.values())
        rows = np.cumsum([0] + [m.shape[0] for m in masks.values()])
        slab = np.concatenate([np.pad(m, ((0, 0), (0, width - m.shape[1]))) for m in masks.values()], axis=0)
        return jnp.asarray(slab, dtype), tuple((name, (int(r0),) + m.shape) for (name, m), r0 in zip(masks.items(), rows))

    slab16, where16 = pack(BF16, seg=seg, tril=tril, bdl=bdl, bdsq=bdsq)
    slab32, where32 = pack(F32, amask=amask, eye=eye, smask=smask)
    return (slab16, slab32), (where16, where32)


def _rwkv_stages(p_ref, shift_ref, s0_ref, wts, consts, y_ref, shift_out_ref, s_out_ref, pbuf, bds,
                 *, chunk, hg, row0, first, last, zero_state):
    mu_ref, w0_ref, w2_ref, a0_ref, a2_ref, g2_ref, kk_ref, ka_ref, rk_ref, lnw_ref, lnb_ref = wts
    L = chunk
    nb = p_ref.shape[0]
    rows = nb * L
    hl = hg * L
    wid = hg * A_HEAD_DIM
    groups = A_HEADS // hg
    diag = lambda h: (h // hg, slice((h % hg) * A_HEAD_DIM, (h % hg + 1) * A_HEAD_DIM))
    win = slice(row0, row0 + L)

    if first is not None:
        @pl.when(first)
        def _():
            bds[...] = jnp.zeros(bds.shape, F32)
            pbuf[...] = jnp.zeros(pbuf.shape, F32)

            def load():
                for b in range(nb):
                    pbuf[b, CARRY - 1:CARRY, :] = shift_ref[b]
                    for h in range(A_HEADS):
                        g, sl = diag(h)
                        bds[b, g, sl, sl] = s0_ref[b, h]
            _unless(zero_state, load)

    ps, prevs = [], []
    for b in range(nb):
        pb = p_ref[b, win, :]
        prevs.append(_shift_rows(pb, pbuf[b], 1))
        pbuf[b] = pb[L - CARRY:L, :]
        shift_out_ref[b] = pb[L - 1:L, :]
        ps.append(pb)
    i1, i2, i3 = A_WIDTH, 2 * A_WIDTH, 3 * A_WIDTH
    i4, i5 = i3 + A_DECAY_LORA, i3 + A_DECAY_LORA + A_AAA_LORA
    mu = mu_ref[...]

    def token_shift(p, prev):
        pm = p + (prev - p) * mu
        return (pm[:, :i1], pm[:, i1:i2], pm[:, i2:i3], jnp.tanh(pm[:, i3:i4]).astype(BF16),
                pm[:, i4:i5].astype(BF16), _sigmoid(pm[:, i5:]).astype(BF16))

    r, k, v, wl16, al16, gl16 = _rowwise(token_shift, rows, jnp.concatenate(ps, axis=0), jnp.concatenate(prevs, axis=0))
    yield
    seg = consts["seg"]()
    half = A_WIDTH // 2
    segsum = lambda t16: jnp.concatenate([_dot(t16[:, :half], seg), _dot(t16[:, half:], seg)], axis=1)
    z = w0_ref[...] + _dot(wl16, w2_ref[...])
    a_pre = a0_ref[...] + _dot(al16, a2_ref[...])
    gate = _dot(gl16, g2_ref[...])
    yield
    k_k, k_a, r_k = kk_ref[...], ka_ref[...], rk_ref[...]

    def decay_and_keys(k, z, a_pre):
        lw = -math.exp(-0.5) * _sigmoid(z)
        lw_hi = lw.astype(BF16)
        a = _sigmoid(a_pre)
        kk0 = k * k_k
        return (lw, lw_hi, (lw - lw_hi.astype(F32)).astype(BF16), a, kk0, (kk0 * kk0).astype(BF16),
                k * (1.0 + (a - 1.0) * k_a))

    lw, lw_hi, lw_lo, a, kk0, kk_sq16, kmod = _rowwise(decay_and_keys, rows, k, z, a_pre)
    yield
    tril = consts["tril"]()
    cum = _dot(tril, lw_hi) + _dot(tril, lw_lo)
    kk_ss = segsum(kk_sq16)
    yield

    rs = lambda b: slice(b * L, (b + 1) * L)
    cum_last = [cum[(b + 1) * L - 1:(b + 1) * L] for b in range(nb)]
    g_last = [jnp.exp(cum_last[b]) for b in range(nb)]

    def scaled_operands(b):
        def fn(r, v, kk0, kk_ss, a, kmod, lw, cum):
            kk = kk0 * lax.rsqrt(kk_ss + 1e-12)
            bvec = kk * a
            g_inv = jnp.exp(-cum)
            g_tail = g_last[b] * g_inv
            return ((kk * jnp.exp(cum - lw)).astype(BF16), (r * jnp.exp(cum)).astype(BF16),
                    (kmod * g_inv).astype(BF16), (bvec * g_inv).astype(BF16), v.astype(BF16),
                    (kmod * g_tail).astype(BF16), (bvec * g_tail).astype(BF16), (r * kmod * r_k).astype(BF16))
        return fn

    per_stream = [_rowwise(scaled_operands(b), L, *[t[rs(b)] for t in (r, v, kk0, kk_ss, a, kmod, lw, cum)])
                  for b in range(nb)]
    kkg_s, rg_s, kd_s, bd_s, v_s, kt, bt, bonus_s = zip(*per_stream)
    bonus_ss = segsum(jnp.concatenate(bonus_s, axis=0))
    yield PROLOGUE_DONE

    keep = consts["amask"]() > 0.5
    eye = consts["eye"]()
    bdl = consts["bdl"]()
    bdsq = consts["bdsq"]()
    smask16 = consts["smask"]().astype(BF16)
    def block_diag(t, mask, seg):
        per_tile = max(1, LANES // seg)
        tile = per_tile * seg
        rows = []
        for h in range(hg):
            cols = [t[:, j:j + tile] * mask[h * L:(h + 1) * L, j:j + tile] if j // tile == h // per_tile
                    else jnp.zeros((L, tile), t.dtype) for j in range(0, t.shape[1], tile)]
            rows.append(cols[0] if len(cols) == 1 else jnp.concatenate(cols, axis=1))
        return jnp.concatenate(rows, axis=0)

    lane_bd = lambda t: block_diag(t, bdl, A_HEAD_DIM)
    sq_bd = lambda t: block_diag(t, bdsq, L)

    chains = [(b, g) for b in range(nb) for g in range(groups)]
    ls = lambda g: slice(g * wid, (g + 1) * wid)
    each = lambda f: [f(i, b, g) for i, (b, g) in enumerate(chains)]

    kkg = each(lambda i, b, g: kkg_s[b][:, ls(g)])
    rg = each(lambda i, b, g: rg_s[b][:, ls(g)])
    v16 = each(lambda i, b, g: v_s[b][:, ls(g)])
    lhs = each(lambda i, b, g: jnp.concatenate([kkg[i], rg[i]], axis=0))
    a_k = each(lambda i, b, g: jnp.where(keep, _dot_nt(lhs[i], lane_bd(kd_s[b][:, ls(g)])), 0.0))
    yield
    a_b = each(lambda i, b, g: jnp.where(keep, _dot_nt(lhs[i], lane_bd(bd_s[b][:, ls(g)])), 0.0))
    a_kk = each(lambda i, b, g: a_k[i][:L].astype(BF16))
    a_rk = each(lambda i, b, g: a_k[i][L:].astype(BF16))
    a_rb = each(lambda i, b, g: a_b[i][L:].astype(BF16))
    yield

    npow = each(lambda i, b, g: -a_b[i][:L])
    tinv = each(lambda i, b, g: eye + npow[i])
    n16 = each(lambda i, b, g: npow[i].astype(BF16))
    npow = each(lambda i, b, g: _dot(n16[i], sq_bd(n16[i])))
    yield
    span = 2
    while span < L:
        n16 = each(lambda i, b, g: npow[i].astype(BF16))
        t_bd = each(lambda i, b, g: sq_bd(tinv[i].astype(BF16)))
        if 2 * span < L:
            both = each(lambda i, b, g: _dot(n16[i], jnp.concatenate([sq_bd(n16[i]), t_bd[i]], axis=1)))
            npow = each(lambda i, b, g: both[i][:, :hl])
            prod = each(lambda i, b, g: both[i][:, hl:])
        else:
            prod = each(lambda i, b, g: _dot(n16[i], t_bd[i]))
        tinv = each(lambda i, b, g: tinv[i] + prod[i])
        span *= 2
        yield

    yield READS_STATE
    s16 = each(lambda i, b, g: bds[b, g].astype(BF16) * smask16)
    v_bd = each(lambda i, b, g: lane_bd(v16[i]))
    rhs = each(lambda i, b, g: (_dot_nt(kkg[i], s16[i]) + _dot(a_kk[i], v_bd[i])).astype(BF16))
    yield
    u16 = each(lambda i, b, g: _dot(tinv[i].astype(BF16), lane_bd(rhs[i])).astype(BF16))
    yield
    yc_ = each(lambda i, b, g: _dot_nt(rg[i], s16[i]) + _dot(a_rk[i], v_bd[i]) - _dot(a_rb[i], lane_bd(u16[i])))
    yield
    upd = each(lambda i, b, g: _dot_tn(jnp.concatenate([v16[i], -u16[i]], axis=0),
                                       jnp.concatenate([kt[b][:, ls(g)], bt[b][:, ls(g)]], axis=0)))
    for i, (b, g) in enumerate(chains):
        bds[b, g] = bds[b, g] * g_last[b][:, ls(g)] + upd[i]
    yield WROTE_STATE
    ys = [jnp.concatenate(yc_[b * groups:(b + 1) * groups], axis=1) for b in range(nb)]
    y = jnp.concatenate(ys, axis=0)

    inv_d = 1.0 / A_HEAD_DIM
    y_sum = segsum(y.astype(BF16))

    def centre(y, y_sum):
        yc = y - y_sum * inv_d
        return yc, (yc * yc).astype(BF16)

    yc, yc_sq16 = _rowwise(centre, rows, y, y_sum)
    var_sum = segsum(yc_sq16)
    yield
    ln_w, ln_b = lnw_ref[...], lnb_ref[...]

    def finish(yc, var_sum, bonus_ss, v, gate):
        return ((yc * lax.rsqrt(var_sum * inv_d + GN_EPS) * ln_w + ln_b + bonus_ss * v) * gate,)

    out, = _rowwise(finish, rows, yc, var_sum, bonus_ss, v, gate)
    for b in range(nb):
        y_ref[b, win, 0:A_WIDTH] = out[b * L:(b + 1) * L]

    if last is not None:
        @pl.when(last)
        def _():
            for b in range(nb):
                for h in range(A_HEADS):
                    g, sl = diag(h)
                    s_out_ref[b, h] = bds[b, g, sl, sl]


def _mlstm_stages(pb_ref, pg_ref, conv0_ref, c0_ref, n0_ref, m0_ref, wts, consts,
                  y_ref, conv_out_ref, c_ref, n_ref, m_ref, xbuf, *, chunk, row0, first, zero_state):
    cw_ref, cb_ref, gb_ref, hnw_ref = wts
    win = slice(row0, row0 + chunk)
    L = chunk
    nb = pb_ref.shape[0]
    rows = nb * L
    hist = B_CONV - 1

    if first is not None:
        @pl.when(first)
        def _():
            xbuf[...] = jnp.zeros(xbuf.shape, F32)
            c_ref[...] = jnp.zeros(c_ref.shape, F32)
            n_ref[...] = jnp.zeros(n_ref.shape, F32)
            m_ref[...] = jnp.zeros(m_ref.shape, F32)

            def load():
                for b in range(nb):
                    xbuf[b, CARRY - hist:CARRY, :] = conv0_ref[b]
                c_ref[...] = c0_ref[...]
                n_ref[...] = n0_ref[...]
                m_ref[...] = m0_ref[...]
            _unless(zero_state, load)

    qk = []
    for b in range(nb):
        x = pb_ref[b, win, 0:2 * B_WIDTH]
        prev = xbuf[b]
        acc = cb_ref[...] + cw_ref[hist:hist + 1, :] * x
        for j in range(hist):
            acc = acc + cw_ref[j:j + 1, :] * _shift_rows(x, prev, hist - j)
        xbuf[b] = x[L - CARRY:L, :]
        conv_out_ref[b] = x[L - hist:L, :]
        qk.append(_silu(acc))
        yield

    gates = jnp.concatenate([pg_ref[b, win, :] for b in range(nb)], axis=0) + gb_ref[...]
    lane = lax.broadcasted_iota(jnp.int32, (rows, GATE_PAD), 1)
    tril = consts["tril"]()
    cum = sum(_dot(tril, part) for part in _split3(-_softplus(-gates)))
    gcols = jnp.where(lane < B_HEADS, gates, cum)
    grows = gcols.T
    causal = _tril(L, False)
    yield

    chains = [(b, h) for b in range(nb) for h in range(B_HEADS)]
    each = lambda f: [f(i, b, h) for i, (b, h) in enumerate(chains)]
    rs = lambda b: slice(b * L, (b + 1) * L)
    hs = lambda h, base=0: slice(base + h * B_HEAD_DIM, base + (h + 1) * B_HEAD_DIM)

    q = each(lambda i, b, h: qk[b][:, hs(h)])
    k = each(lambda i, b, h: qk[b][:, hs(h, B_WIDTH)] * (B_HEAD_DIM ** -0.5))
    q16 = each(lambda i, b, h: q[i].astype(BF16))
    v16 = each(lambda i, b, h: pb_ref[b, win, hs(h, 2 * B_WIDTH)].astype(BF16))
    li_col = each(lambda i, b, h: gcols[rs(b), h:h + 1])
    b_col = each(lambda i, b, h: gcols[rs(b), B_HEADS + h:B_HEADS + h + 1])
    li_row = each(lambda i, b, h: grows[h:h + 1, rs(b)])
    b_row = each(lambda i, b, h: grows[B_HEADS + h:B_HEADS + h + 1, rs(b)])
    yield READS_STATE
    c_prev = each(lambda i, b, h: c_ref[b, h])
    n_prev = each(lambda i, b, h: n_ref[b, h:h + 1, :])
    m_prev = each(lambda i, b, h: m_ref[b, h:h + 1, 0:1])

    dm = each(lambda i, b, h: jnp.where(causal, b_col[i] - b_row[i] + li_row[i], -jnp.inf))
    inter = each(lambda i, b, h: b_col[i] + m_prev[i])
    mt = each(lambda i, b, h: jnp.maximum(inter[i], jnp.max(dm[i], axis=-1, keepdims=True)))
    wo = each(lambda i, b, h: jnp.exp(inter[i] - mt[i]))
    yield
    s = each(lambda i, b, h: _dot_nt(q16[i], k[i].astype(BF16)) * jnp.exp(dm[i] - mt[i]))
    yield
    num = each(lambda i, b, h: wo[i] * _dot(q16[i], c_prev[i].astype(BF16)) + _dot(s[i].astype(BF16), v16[i]))
    yield
    den = each(lambda i, b, h: wo[i] * jnp.sum(q[i] * n_prev[i], axis=-1, keepdims=True)
               + jnp.sum(s[i], axis=-1, keepdims=True))
    hh = each(lambda i, b, h: num[i] * (1.0 / jnp.maximum(jnp.abs(den[i]), jnp.exp(-mt[i]))))
    yield

    m_new = each(lambda i, b, h: mt[i][L - 1:L, :])
    b_last = each(lambda i, b, h: b_col[i][L - 1:L, :])
    kw = each(lambda i, b, h: k[i] * jnp.exp(b_last[i] - b_col[i] + li_col[i] - m_new[i]))
    dec = each(lambda i, b, h: jnp.exp(b_last[i] + m_prev[i] - m_new[i]))
    yield
    c_new = each(lambda i, b, h: dec[i] * c_prev[i] + _dot_tn(kw[i].astype(BF16), v16[i]))
    n_new = each(lambda i, b, h: dec[i] * n_prev[i] + jnp.sum(kw[i], axis=0, keepdims=True))
    yield
    out = each(lambda i, b, h: hh[i] * lax.rsqrt(jnp.mean(hh[i] * hh[i], axis=-1, keepdims=True) + RMS_EPS)
               * hnw_ref[:, hs(h)] * _sigmoid(pb_ref[b, win, hs(h, 3 * B_WIDTH)]))
    for i, (b, h) in enumerate(chains):
        c_ref[b, h] = c_new[i]
        n_ref[b, h:h + 1, :] = n_new[i]
        m_ref[b, h:h + 1, :] = jnp.broadcast_to(m_new[i], (1, B_HEAD_DIM))
        y_ref[b, win, hs(h, A_WIDTH)] = out[i]


PROLOGUE_DONE, READS_STATE, WROTE_STATE, DONE = "prologue_done", "reads_state", "wrote_state", "done"
N_RWKV_W = 11
N_MLSTM_W = 4
N_CONSTS = 2


def _mix_kernel(*refs, chunk, hg, where, zero_first):
    it = iter(refs)
    take = lambda n: [next(it) for _ in range(n)]
    pa_ref, pb_ref, pg_ref, shift_ref, s0_ref, conv0_ref, c0_ref, n0_ref, m0_ref = take(9)
    rwkv_w, mlstm_w, slabs = take(N_RWKV_W), take(N_MLSTM_W), take(N_CONSTS)
    consts = {name: functools.partial(lambda ref, r0, rows, cols: ref[r0:r0 + rows, 0:cols], ref, *at)
              for ref, masks in zip(slabs, where) for name, at in masks}
    y_ref, shift_out_ref, s_out_ref, conv_out_ref, c_ref, n_ref, m_ref, pbuf, bds, xbuf = take(10)
    first = pl.program_id(1) == 0
    last = pl.program_id(1) == pl.num_programs(1) - 1
    zero_state = pl.program_id(0) == 0 if zero_first else None
    subs = pa_ref.shape[1] // chunk
    bodies = []
    for sub in range(subs):
        is_first = first if sub == 0 else None
        bodies.append([
            _rwkv_stages(pa_ref, shift_ref, s0_ref, rwkv_w, consts, y_ref, shift_out_ref, s_out_ref, pbuf, bds,
                         chunk=chunk, hg=hg, row0=sub * chunk, first=is_first,
                         last=last if sub == subs - 1 else None, zero_state=zero_state),
            _mlstm_stages(pb_ref, pg_ref, conv0_ref, c0_ref, n0_ref, m0_ref, mlstm_w, consts,
                          y_ref, conv_out_ref, c_ref, n_ref, m_ref, xbuf, chunk=chunk, row0=sub * chunk,
                          first=is_first, zero_state=zero_state),
        ])
    live = [(sub, kind) for sub in range(subs) for kind in range(2)]
    wrote, waiting, started = set(), set(), {(0, 0), (0, 1)}
    while live:
        for key in list(live):
            sub, kind = key
            if key not in started or (key in waiting and (sub - 1, kind) not in wrote):
                continue
            waiting.discard(key)
            mark = next(bodies[sub][kind], DONE)
            if mark == PROLOGUE_DONE:
                started.add((sub + 1, 0))
                if sub > 0:
                    started.add((sub, 1))
            elif mark == READS_STATE and sub > 0:
                waiting.add(key)
            elif mark in (WROTE_STATE, DONE):
                wrote.add(key)
                if mark == DONE:
                    live.remove(key)


def _mix(pa, pb, pg, l, sl, states, rwkv_w, mlstm_w, chunk, subs, zero_first):
    bn, t, _ = pa.shape
    nb = MIX_STREAMS
    hg = _rwkv_heads_per_group(chunk)
    hist = B_CONV - 1
    consts, where = _mix_consts(chunk, nb)
    full = lambda a: pl.BlockSpec(a.shape, lambda i, c: (0,) * a.ndim)
    tok = lambda n: pl.BlockSpec((nb, subs * chunk, n), lambda i, c: (i, c, 0))
    per_stream = lambda *dims: pl.BlockSpec((nb,) + dims, lambda i, c: (i,) + (0,) * len(dims))
    state_shapes = [(1, A_COLS), (A_HEADS, A_HEAD_DIM, A_HEAD_DIM), (hist, 2 * B_WIDTH),
                    (B_HEADS, B_HEAD_DIM, B_HEAD_DIM), (B_HEADS, B_HEAD_DIM), (B_HEADS, B_HEAD_DIM)]
    return pl.pallas_call(
        functools.partial(_mix_kernel, chunk=chunk, hg=hg, where=where, zero_first=zero_first),
        grid=(bn // nb, t // (subs * chunk)),
        in_specs=[tok(A_COLS), tok(B_MAIN), tok(GATE_PAD)]
        + [_state_spec(a, sl, nb, zero_first) for a in states]
        + [_layer_spec(a, l) for a in rwkv_w + mlstm_w] + [full(a) for a in consts],
        out_specs=[tok(D_MIX)] + [per_stream(*dims) for dims in state_shapes],
        out_shape=[jax.ShapeDtypeStruct((bn, t, D_MIX), F32)]
        + [jax.ShapeDtypeStruct((bn,) + dims, F32) for dims in state_shapes],
        scratch_shapes=[
            pltpu.VMEM((nb, CARRY, A_COLS), F32),
            pltpu.VMEM((nb, A_HEADS // hg, hg * A_HEAD_DIM, hg * A_HEAD_DIM), F32),
            pltpu.VMEM((nb, CARRY, 2 * B_WIDTH), F32),
        ],
        compiler_params=pltpu.CompilerParams(
            dimension_semantics=("arbitrary", "arbitrary"), vmem_limit_bytes=VMEM_LIMIT),
        name="mixers",
    )(pa, pb, pg, *states, *rwkv_w, *mlstm_w, *consts)


def _ffn_kernel(x_ref, y_ref, f0_ref, wout_ref, g_ref, wup_ref, cw_ref, cb_ref,
                wdown_ref, gfin_ref, o_ref, fout_ref, ubuf, *, tile, final, zero_streams):
    nb = x_ref.shape[0]
    hist = FFN_CONV - 1
    rows_of = lambda ref: ref[0] if nb == 1 else jnp.concatenate([ref[b] for b in range(nb)], axis=0)

    @pl.when(pl.program_id(1) == 0)
    def _():
        ubuf[...] = jnp.zeros(ubuf.shape, F32)
        for b in range(zero_streams, nb):
            ubuf[b, CARRY - hist:CARRY, :] = f0_ref[b - zero_streams]

    x1 = rows_of(x_ref) + _dot(rows_of(y_ref).astype(BF16), wout_ref[...])
    h2 = _rms(x1, g_ref[...]).astype(BF16)
    acts = []
    for lo, hi in zip(FF_BOUNDS[:-1], FF_BOUNDS[1:]):
        cols = slice(lo, hi)
        u = _dot(h2, wup_ref[:, cols])
        gate = _dot(h2, wup_ref[:, D_FF + lo:D_FF + hi])
        convs = []
        for b in range(nb):
            ub = u[b * tile:(b + 1) * tile]
            prev = ubuf[b, :, cols]
            acc = cb_ref[:, cols] + cw_ref[hist:hist + 1, cols] * ub
            for j in range(hist):
                acc = acc + cw_ref[j:j + 1, cols] * _shift_rows(ub, prev, hist - j)
            ubuf[b, :, cols] = ub[tile - CARRY:tile]
            fout_ref[b, :, cols] = ub[tile - hist:tile]
            convs.append(acc)
        conv = convs[0] if nb == 1 else jnp.concatenate(convs, axis=0)
        acts.append((_silu(conv) * gate).astype(BF16))
    x2 = x1 + _dot(jnp.concatenate(acts, axis=1), wdown_ref[...])
    out = _rms(x2, gfin_ref[...]) if final else x2
    for b in range(nb):
        o_ref[b] = out[b * tile:(b + 1) * tile]


def _ffn(x, y, l, sl, f0, wts, gfin, nb, tile, final, zero_streams):
    bn, t, _ = x.shape
    hist = FFN_CONV - 1
    tok = pl.BlockSpec((nb, tile, D_MODEL), lambda i, c: (i, c, 0))
    fspec = pl.BlockSpec((nb, hist, D_FF), lambda i, c: (i, 0, 0))
    return pl.pallas_call(
        functools.partial(_ffn_kernel, tile=tile, final=final, zero_streams=zero_streams),
        grid=(bn // nb, t // tile),
        in_specs=[tok, tok, _state_spec(f0, sl, nb - zero_streams)]
        + [_layer_spec(a, l, resident=True) for a in wts] + [_layer_spec(gfin, 0, resident=True)],
        out_specs=[tok, fspec],
        out_shape=[
            jax.ShapeDtypeStruct((bn, t, D_MODEL), F32),
            jax.ShapeDtypeStruct((bn, hist, D_FF), F32),
        ],
        scratch_shapes=[pltpu.VMEM((nb, CARRY, D_FF), F32)],
        compiler_params=pltpu.CompilerParams(
            dimension_semantics=("arbitrary", "arbitrary"), vmem_limit_bytes=VMEM_LIMIT),
        name="out_ffn",
    )(x, y, f0, *wts, gfin)


def _layer(x, l, sl, st, w, *, in_tile, mix_chunk, mix_subs, ffn_streams, ffn_tile, final, zero_streams):
    bn, t, _ = x.shape
    pa, pb, pg = _in_proj(x.reshape(bn * t, D_MODEL), l, w["norm_mix"], w["w_in"], w["w_gate"], in_tile)
    pa = pa.reshape(bn, t, A_COLS)
    pb = pb.reshape(bn, t, B_MAIN)
    pg = pg.reshape(bn, t, GATE_PAD)
    y, *mix_states = _mix(pa, pb, pg, l, sl, st[:6], w["rwkv"], w["mlstm"], mix_chunk, mix_subs,
                          zero_streams > 0)
    x, fconv1 = _ffn(x, y, l, sl, st[6], w["ffn"], w["norm_final"], ffn_streams, ffn_tile, final, zero_streams)
    return x, tuple(mix_states) + (fconv1,)


def kernel(x_prompt, x_sample, state_rwkv_shift, state_rwkv_wkv, state_mlstm_conv, state_mlstm_C,
           state_mlstm_n, state_mlstm_m, state_ffn_conv, meta_tokens, norm_mix, w_in, a_mu, a_w0, a_w2,
           a_a0, a_a2, a_g2, a_k_k, a_k_a, a_r_k, a_ln_w, a_ln_b, b_conv_w, b_conv_b, b_i_bias, b_f_bias,
           b_hn_w, w_out, norm_ffn, w_up, ffn_conv_w, ffn_conv_b, w_down, norm_final):
    n_prompt = x_prompt.shape[0]
    n_sample = x_sample.shape[0]
    n_lead = n_prompt + n_sample
    assert x_sample.shape[1] == N_META
    assert n_prompt == MIX_STREAMS and n_lead % MIX_STREAMS == 0

    bf = lambda a: a.astype(BF16)
    vec = lambda a: a[:, None, :]
    n_gate = 2 * B_HEADS
    gate_pad = ((0, 0), (0, GATE_PAD - n_gate))
    w = {
        "norm_mix": vec(norm_mix),
        "w_in": bf(w_in),
        "w_gate": bf(jnp.pad(w_in[:, :, A_COLS + B_MAIN:], ((0, 0),) + gate_pad)),
        "rwkv": [vec(a_mu), vec(a_w0), bf(a_w2), vec(a_a0), bf(a_a2), bf(a_g2), vec(a_k_k), vec(a_k_a),
                 vec(a_r_k), vec(a_ln_w), vec(a_ln_b)],
        "mlstm": [b_conv_w, vec(b_conv_b), vec(jnp.pad(jnp.concatenate([b_i_bias, b_f_bias], axis=1), gate_pad)),
                  vec(b_hn_w)],
        "ffn": (bf(w_out), vec(norm_ffn), bf(w_up), ffn_conv_w, vec(ffn_conv_b), bf(w_down)),
        "norm_final": norm_final.reshape(1, 1, D_MODEL),
    }
    assert len(w["rwkv"]) == N_RWKV_W and len(w["mlstm"]) == N_MLSTM_W

    st_lead_in = (
        state_rwkv_shift[:, :, None, :], state_rwkv_wkv, state_mlstm_conv, state_mlstm_C, state_mlstm_n,
        jnp.broadcast_to(state_mlstm_m[..., None], state_mlstm_m.shape + (B_HEAD_DIM,)), state_ffn_conv,
    )
    meta = jnp.broadcast_to(meta_tokens[None].astype(x_prompt.dtype), (n_prompt, N_META, D_MODEL))
    x_lead = jnp.concatenate([meta, x_sample], axis=0)
    x_main = x_prompt

    main_states, lead_states = [], []
    for l in range(DEPTH):
        final = l == DEPTH - 1
        x_lead, st_lead = _layer(x_lead, l, l, st_lead_in, w, in_tile=n_lead * N_META, mix_chunk=N_META,
                                 mix_subs=1, ffn_streams=n_lead, ffn_tile=N_META, final=final,
                                 zero_streams=n_prompt)
        x_main, st_main = _layer(x_main, l, 0, tuple(s[None] for s in st_lead), w, in_tile=MAIN_PROJ_TILE,
                                 mix_chunk=MAIN_CHUNK, mix_subs=MAIN_SUBCHUNKS, ffn_streams=1, ffn_tile=MAIN_TILE, final=final,
                                 zero_streams=0)
        main_states.append(st_main)
        lead_states.append(st_lead)

    def collect(per_layer, first):
        shift, wkv, bconv, c, n, m, fconv = (jnp.stack([st[i] for st in per_layer])[:, first:] for i in range(7))
        return (shift[:, :, 0, :], wkv, bconv, c, n, m[..., 0], fconv)

    return (x_main, x_lead[n_prompt:]) + collect(main_states, 0) + collect(lead_states, n_prompt)
```

```python
import functools
import math

import numpy as np

import jax
import jax.numpy as jnp
from jax import lax
from jax.experimental import pallas as pl
from jax.experimental.pallas import tpu as pltpu

D_MODEL = 1024
DEPTH = 2
N_META = 16
A_HEADS = 8
A_HEAD_DIM = 64
A_WIDTH = 512
A_DECAY_LORA = 64
A_AAA_LORA = 64
A_GATE_LORA = 128
A_COLS = 1792
B_HEADS = 4
B_HEAD_DIM = 128
B_WIDTH = 512
B_CONV = 4
B_MAIN = 4 * B_WIDTH
GATE_PAD = 128
D_MIX = A_WIDTH + B_WIDTH
D_FF = 2816
FFN_CONV = 3
RMS_EPS = 1e-6
GN_EPS = 64e-5
CARRY = 8
ROW_BLOCK = 16
LANES = 128
MXU_DIM = 256
MIX_STREAMS = 4
FF_BOUNDS = (0, 6 * MXU_DIM, D_FF)
MAIN_TILE = 512
MAIN_PROJ_TILE = 1024
MAIN_CHUNK = 64
MAIN_SUBCHUNKS = 2
VMEM_LIMIT = 56 * 1024 * 1024

F32 = jnp.float32
BF16 = jnp.bfloat16
NT_DIMS = (((1,), (1,)), ((), ()))
TN_DIMS = (((0,), (0,)), ((), ()))


def _dot(a, b):
    return jnp.dot(a, b, preferred_element_type=F32)


def _dot_nt(a, b):
    return lax.dot_general(a, b, NT_DIMS, preferred_element_type=F32)


def _dot_tn(a, b):
    return lax.dot_general(a, b, TN_DIMS, preferred_element_type=F32)


def _sigmoid(x):
    return 0.5 * jnp.tanh(0.5 * x) + 0.5


def _silu(x):
    half = 0.5 * x
    return half + half * jnp.tanh(half)


def _softplus(x):
    return jnp.maximum(x, 0.0) + jnp.log(1.0 + jnp.exp(-jnp.abs(x)))


def _rms(x, g):
    return x * lax.rsqrt(jnp.mean(x * x, axis=-1, keepdims=True) + RMS_EPS) * g


def _tril(n, strict):
    row = lax.broadcasted_iota(jnp.int32, (n, n), 0)
    col = lax.broadcasted_iota(jnp.int32, (n, n), 1)
    return (col < row) if strict else (col <= row)


def _split3(x):
    hi = x.astype(BF16)
    rest = x - hi.astype(F32)
    mid = rest.astype(BF16)
    lo = (rest - mid.astype(F32)).astype(BF16)
    return hi, mid, lo


def _shift_rows(x, prev, s):
    rolled = pltpu.roll(x, s, 0)
    from_prev = lax.broadcasted_iota(jnp.int32, prev.shape, 0) < s
    head = jnp.where(from_prev, pltpu.roll(prev, s, 0), rolled[0:CARRY])
    return jnp.concatenate([head, rolled[CARRY:]], axis=0)


def _rowwise(fn, n_rows, *xs):
    pieces = [fn(*[x[i:i + ROW_BLOCK] for x in xs]) for i in range(0, n_rows, ROW_BLOCK)]
    return [jnp.concatenate(col, axis=0) for col in zip(*pieces)]


def _layer_spec(a, l, resident=False):
    index = lambda *grid: (l,) + (0,) * (a.ndim - 1)
    if resident:
        return pl.BlockSpec((None,) + a.shape[1:], index, pipeline_mode=pl.Buffered(1))
    return pl.BlockSpec((None,) + a.shape[1:], index)


def _state_spec(a, l, nb, skip_first=False):
    block = (lambda i: jnp.maximum(i - 1, 0)) if skip_first else (lambda i: i)
    return pl.BlockSpec((None, nb) + a.shape[2:], lambda i, c: (l, block(i)) + (0,) * (a.ndim - 2))


def _unless(zero_state, load):
    if zero_state is None:
        load()
    else:
        pl.when(jnp.logical_not(zero_state))(load)


def _in_proj_kernel(x_ref, g_ref, w_ref, wg_ref, oa_ref, ob_ref, og_ref):
    h = _rms(x_ref[...], g_ref[...]).astype(BF16)
    oa_ref[...] = _dot(h, w_ref[:, 0:A_COLS])
    ob_ref[...] = _dot(h, w_ref[:, A_COLS:A_COLS + B_MAIN])
    og_ref[...] = _dot(h, wg_ref[...])


def _in_proj(x2d, l, g, w, wg, tm):
    m = x2d.shape[0]
    return pl.pallas_call(
        _in_proj_kernel,
        grid=(m // tm,),
        in_specs=[
            pl.BlockSpec((tm, D_MODEL), lambda i: (i, 0)),
            _layer_spec(g, l), _layer_spec(w, l, resident=True), _layer_spec(wg, l, resident=True),
        ],
        out_specs=[
            pl.BlockSpec((tm, A_COLS), lambda i: (i, 0)),
            pl.BlockSpec((tm, B_MAIN), lambda i: (i, 0)),
            pl.BlockSpec((tm, GATE_PAD), lambda i: (i, 0)),
        ],
        out_shape=[
            jax.ShapeDtypeStruct((m, A_COLS), F32),
            jax.ShapeDtypeStruct((m, B_MAIN), F32),
            jax.ShapeDtypeStruct((m, GATE_PAD), F32),
        ],
        compiler_params=pltpu.CompilerParams(
            dimension_semantics=("arbitrary",), vmem_limit_bytes=VMEM_LIMIT),
        name="in_proj",
    )(x2d, g, w, wg)


def _rwkv_heads_per_group(chunk):
    return min(A_HEADS, MXU_DIM // chunk, MXU_DIM // A_HEAD_DIM)


def _mix_consts(chunk, streams):
    hg = _rwkv_heads_per_group(chunk)
    hl, w, rows = hg * chunk, hg * A_HEAD_DIM, streams * chunk
    ix = lambda n: (np.arange(n)[:, None], np.arange(n)[None, :])
    r, c = ix(MXU_DIM)
    seg = r // A_HEAD_DIM == c // A_HEAD_DIM
    r, c = ix(rows)
    tril = (r // chunk == c // chunk) & (c <= r)
    r, c = np.arange(2 * chunk)[:, None], np.arange(hl)[None, :]
    amask = np.where(r < chunk, c % chunk < r, c % chunk <= r - chunk)
    r, c = np.arange(chunk)[:, None], np.arange(hl)[None, :]
    eye = c % chunk == r
    r, c = np.arange(hl)[:, None], np.arange(w)[None, :]
    bdl = r // chunk == c // A_HEAD_DIM
    r, c = ix(hl)
    bdsq = r // chunk == c // chunk
    r, c = ix(w)
    smask = r // A_HEAD_DIM == c // A_HEAD_DIM

    def pack(dtype, **masks):
        width = max(m.shape[1] for m in masks.values())
        rows = np.cumsum([0] + [m.shape[0] for m in masks.values()])
        slab = np.concatenate([np.pad(m, ((0, 0), (0, width - m.shape[1]))) for m in masks.values()], axis=0)
        return jnp.asarray(slab, dtype), tuple((name, (int(r0),) + m.shape) for (name, m), r0 in zip(masks.items(), rows))

    slab16, where16 = pack(BF16, seg=seg, tril=tril, bdl=bdl, bdsq=bdsq)
    slab32, where32 = pack(F32, amask=amask, eye=eye, smask=smask)
    return (slab16, slab32), (where16, where32)


def _rwkv_stages(p_ref, shift_ref, s0_ref, wts, consts, y_ref, shift_out_ref, s_out_ref, pbuf, bds,
                 *, chunk, hg, row0, first, last, zero_state):
    mu_ref, w0_ref, w2_ref, a0_ref, a2_ref, g2_ref, kk_ref, ka_ref, rk_ref, lnw_ref, lnb_ref = wts
    L = chunk
    nb = p_ref.shape[0]
    rows = nb * L
    hl = hg * L
    wid = hg * A_HEAD_DIM
    groups = A_HEADS // hg
    diag = lambda h: (h // hg, slice((h % hg) * A_HEAD_DIM, (h % hg + 1) * A_HEAD_DIM))
    win = slice(row0, row0 + L)

    if first is not None:
        @pl.when(first)
        def _():
            bds[...] = jnp.zeros(bds.shape, F32)
            pbuf[...] = jnp.zeros(pbuf.shape, F32)

            def load():
                for b in range(nb):
                    pbuf[b, CARRY - 1:CARRY, :] = shift_ref[b]
                    for h in range(A_HEADS):
                        g, sl = diag(h)
                        bds[b, g, sl, sl] = s0_ref[b, h]
            _unless(zero_state, load)

    ps, prevs = [], []
    for b in range(nb):
        pb = p_ref[b, win, :]
        prevs.append(_shift_rows(pb, pbuf[b], 1))
        pbuf[b] = pb[L - CARRY:L, :]
        shift_out_ref[b] = pb[L - 1:L, :]
        ps.append(pb)
    i1, i2, i3 = A_WIDTH, 2 * A_WIDTH, 3 * A_WIDTH
    i4, i5 = i3 + A_DECAY_LORA, i3 + A_DECAY_LORA + A_AAA_LORA
    mu = mu_ref[...]

    def token_shift(p, prev):
        pm = p + (prev - p) * mu
        return (pm[:, :i1], pm[:, i1:i2], pm[:, i2:i3], jnp.tanh(pm[:, i3:i4]).astype(BF16),
                pm[:, i4:i5].astype(BF16), _sigmoid(pm[:, i5:]).astype(BF16))

    r, k, v, wl16, al16, gl16 = _rowwise(token_shift, rows, jnp.concatenate(ps, axis=0), jnp.concatenate(prevs, axis=0))
    yield
    seg = consts["seg"]()
    half = A_WIDTH // 2
    segsum = lambda t16: jnp.concatenate([_dot(t16[:, :half], seg), _dot(t16[:, half:], seg)], axis=1)
    z = w0_ref[...] + _dot(wl16, w2_ref[...])
    a_pre = a0_ref[...] + _dot(al16, a2_ref[...])
    gate = _dot(gl16, g2_ref[...])
    yield
    k_k, k_a, r_k = kk_ref[...], ka_ref[...], rk_ref[...]

    def decay_and_keys(k, z, a_pre):
        lw = -math.exp(-0.5) * _sigmoid(z)
        lw_hi = lw.astype(BF16)
        a = _sigmoid(a_pre)
        kk0 = k * k_k
        return (lw, lw_hi, (lw - lw_hi.astype(F32)).astype(BF16), a, kk0, (kk0 * kk0).astype(BF16),
                k * (1.0 + (a - 1.0) * k_a))

    lw, lw_hi, lw_lo, a, kk0, kk_sq16, kmod = _rowwise(decay_and_keys, rows, k, z, a_pre)
    yield
    tril = consts["tril"]()
    cum = _dot(tril, lw_hi) + _dot(tril, lw_lo)
    kk_ss = segsum(kk_sq16)
    yield

    rs = lambda b: slice(b * L, (b + 1) * L)
    cum_last = [cum[(b + 1) * L - 1:(b + 1) * L] for b in range(nb)]
    g_last = [jnp.exp(cum_last[b]) for b in range(nb)]

    def scaled_operands(b):
        def fn(r, v, kk0, kk_ss, a, kmod, lw, cum):
            kk = kk0 * lax.rsqrt(kk_ss + 1e-12)
            bvec = kk * a
            g_inv = jnp.exp(-cum)
            g_tail = g_last[b] * g_inv
            return ((kk * jnp.exp(cum - lw)).astype(BF16), (r * jnp.exp(cum)).astype(BF16),
                    (kmod * g_inv).astype(BF16), (bvec * g_inv).astype(BF16), v.astype(BF16),
                    (kmod * g_tail).astype(BF16), (bvec * g_tail).astype(BF16), (r * kmod * r_k).astype(BF16))
        return fn

    per_stream = [_rowwise(scaled_operands(b), L, *[t[rs(b)] for t in (r, v, kk0, kk_ss, a, kmod, lw, cum)])
                  for b in range(nb)]
    kkg_s, rg_s, kd_s, bd_s, v_s, kt, bt, bonus_s = zip(*per_stream)
    bonus_ss = segsum(jnp.concatenate(bonus_s, axis=0))
    yield PROLOGUE_DONE

    keep = consts["amask"]() > 0.5
    eye = consts["eye"]()
    bdl = consts["bdl"]()
    bdsq = consts["bdsq"]()
    smask = consts["smask"]()
    def block_diag(t, mask, seg):
        per_tile = max(1, LANES // seg)
        tile = per_tile * seg
        rows = []
        for h in range(hg):
            cols = [t[:, j:j + tile] * mask[h * L:(h + 1) * L, j:j + tile] if j // tile == h // per_tile
                    else jnp.zeros((L, tile), t.dtype) for j in range(0, t.shape[1], tile)]
            rows.append(cols[0] if len(cols) == 1 else jnp.concatenate(cols, axis=1))
        return jnp.concatenate(rows, axis=0)

    lane_bd = lambda t: block_diag(t, bdl, A_HEAD_DIM)
    sq_bd = lambda t: block_diag(t, bdsq, L)

    chains = [(b, g) for b in range(nb) for g in range(groups)]
    ls = lambda g: slice(g * wid, (g + 1) * wid)
    each = lambda f: [f(i, b, g) for i, (b, g) in enumerate(chains)]

    kkg = each(lambda i, b, g: kkg_s[b][:, ls(g)])
    rg = each(lambda i, b, g: rg_s[b][:, ls(g)])
    v16 = each(lambda i, b, g: v_s[b][:, ls(g)])
    lhs = each(lambda i, b, g: jnp.concatenate([kkg[i], rg[i]], axis=0))
    a_k = each(lambda i, b, g: jnp.where(keep, _dot_nt(lhs[i], lane_bd(kd_s[b][:, ls(g)])), 0.0))
    yield
    a_b = each(lambda i, b, g: jnp.where(keep, _dot_nt(lhs[i], lane_bd(bd_s[b][:, ls(g)])), 0.0))
    a_kk = each(lambda i, b, g: a_k[i][:L].astype(BF16))
    a_rk = each(lambda i, b, g: a_k[i][L:].astype(BF16))
    a_rb = each(lambda i, b, g: a_b[i][L:].astype(BF16))
    yield

    npow = each(lambda i, b, g: -a_b[i][:L])
    tinv = each(lambda i, b, g: eye + npow[i])
    n16 = each(lambda i, b, g: npow[i].astype(BF16))
    npow = each(lambda i, b, g: _dot(n16[i], sq_bd(n16[i])))
    yield
    span = 2
    while span < L:
        n16 = each(lambda i, b, g: npow[i].astype(BF16))
        t_bd = each(lambda i, b, g: sq_bd(tinv[i].astype(BF16)))
        if 2 * span < L:
            both = each(lambda i, b, g: _dot(n16[i], jnp.concatenate([sq_bd(n16[i]), t_bd[i]], axis=1)))
            npow = each(lambda i, b, g: both[i][:, :hl])
            prod = each(lambda i, b, g: both[i][:, hl:])
        else:
            prod = each(lambda i, b, g: _dot(n16[i], t_bd[i]))
        tinv = each(lambda i, b, g: tinv[i] + prod[i])
        span *= 2
        yield

    yield READS_STATE
    s16 = each(lambda i, b, g: bds[b, g].astype(BF16))
    v_bd = each(lambda i, b, g: lane_bd(v16[i]))
    rhs = each(lambda i, b, g: (_dot_nt(kkg[i], s16[i]) + _dot(a_kk[i], v_bd[i])).astype(BF16))
    yield
    u16 = each(lambda i, b, g: _dot(tinv[i].astype(BF16), lane_bd(rhs[i])).astype(BF16))
    yield
    yc_ = each(lambda i, b, g: _dot_nt(rg[i], s16[i]) + _dot(a_rk[i], v_bd[i]) - _dot(a_rb[i], lane_bd(u16[i])))
    yield
    upd = each(lambda i, b, g: _dot_tn(jnp.concatenate([v16[i], -u16[i]], axis=0),
                                       jnp.concatenate([kt[b][:, ls(g)], bt[b][:, ls(g)]], axis=0)))
    for i, (b, g) in enumerate(chains):
        bds[b, g] = bds[b, g] * g_last[b][:, ls(g)] + upd[i] * smask
    yield WROTE_STATE
    ys = [jnp.concatenate(yc_[b * groups:(b + 1) * groups], axis=1) for b in range(nb)]
    y = jnp.concatenate(ys, axis=0)

    inv_d = 1.0 / A_HEAD_DIM
    y_sum = segsum(y.astype(BF16))

    def centre(y, y_sum):
        yc = y - y_sum * inv_d
        return yc, (yc * yc).astype(BF16)

    yc, yc_sq16 = _rowwise(centre, rows, y, y_sum)
    var_sum = segsum(yc_sq16)
    yield
    ln_w, ln_b = lnw_ref[...], lnb_ref[...]

    def finish(yc, var_sum, bonus_ss, v, gate):
        return ((yc * lax.rsqrt(var_sum * inv_d + GN_EPS) * ln_w + ln_b + bonus_ss * v) * gate,)

    out, = _rowwise(finish, rows, yc, var_sum, bonus_ss, v, gate)
    for b in range(nb):
        y_ref[b, win, 0:A_WIDTH] = out[b * L:(b + 1) * L]

    if last is not None:
        @pl.when(last)
        def _():
            for b in range(nb):
                for h in range(A_HEADS):
                    g, sl = diag(h)
                    s_out_ref[b, h] = bds[b, g, sl, sl]


def _mlstm_stages(pb_ref, pg_ref, conv0_ref, c0_ref, n0_ref, m0_ref, wts, consts,
                  y_ref, conv_out_ref, c_ref, n_ref, m_ref, xbuf, *, chunk, row0, first, zero_state):
    cw_ref, cb_ref, gb_ref, hnw_ref = wts
    win = slice(row0, row0 + chunk)
    L = chunk
    nb = pb_ref.shape[0]
    rows = nb * L
    hist = B_CONV - 1

    if first is not None:
        @pl.when(first)
        def _():
            xbuf[...] = jnp.zeros(xbuf.shape, F32)
            c_ref[...] = jnp.zeros(c_ref.shape, F32)
            n_ref[...] = jnp.zeros(n_ref.shape, F32)
            m_ref[...] = jnp.zeros(m_ref.shape, F32)

            def load():
                for b in range(nb):
                    xbuf[b, CARRY - hist:CARRY, :] = conv0_ref[b]
                c_ref[...] = c0_ref[...]
                n_ref[...] = n0_ref[...]
                m_ref[...] = m0_ref[...]
            _unless(zero_state, load)

    qk = []
    for b in range(nb):
        x = pb_ref[b, win, 0:2 * B_WIDTH]
        prev = xbuf[b]
        acc = cb_ref[...] + cw_ref[hist:hist + 1, :] * x
        for j in range(hist):
            acc = acc + cw_ref[j:j + 1, :] * _shift_rows(x, prev, hist - j)
        xbuf[b] = x[L - CARRY:L, :]
        conv_out_ref[b] = x[L - hist:L, :]
        qk.append(_silu(acc))
        yield

    gates = jnp.concatenate([pg_ref[b, win, :] for b in range(nb)], axis=0) + gb_ref[...]
    lane = lax.broadcasted_iota(jnp.int32, (rows, GATE_PAD), 1)
    tril = consts["tril"]()
    cum = sum(_dot(tril, part) for part in _split3(-_softplus(-gates)))
    gcols = jnp.where(lane < B_HEADS, gates, cum)
    grows = gcols.T
    causal = _tril(L, False)
    yield

    chains = [(b, h) for b in range(nb) for h in range(B_HEADS)]
    each = lambda f: [f(i, b, h) for i, (b, h) in enumerate(chains)]
    rs = lambda b: slice(b * L, (b + 1) * L)
    hs = lambda h, base=0: slice(base + h * B_HEAD_DIM, base + (h + 1) * B_HEAD_DIM)

    q = each(lambda i, b, h: qk[b][:, hs(h)])
    k = each(lambda i, b, h: qk[b][:, hs(h, B_WIDTH)] * (B_HEAD_DIM ** -0.5))
    q16 = each(lambda i, b, h: q[i].astype(BF16))
    li_col = each(lambda i, b, h: gcols[rs(b), h:h + 1])
    b_col = each(lambda i, b, h: gcols[rs(b), B_HEADS + h:B_HEADS + h + 1])
    li_row = each(lambda i, b, h: grows[h:h + 1, rs(b)])
    b_row = each(lambda i, b, h: grows[B_HEADS + h:B_HEADS + h + 1, rs(b)])
    yield READS_STATE
    c_prev = each(lambda i, b, h: c_ref[b, h])
    n_prev = each(lambda i, b, h: n_ref[b, h:h + 1, :])
    m_prev = each(lambda i, b, h: m_ref[b, h:h + 1, 0:1])

    dm = each(lambda i, b, h: jnp.where(causal, b_col[i] - b_row[i] + li_row[i], -jnp.inf))
    inter = each(lambda i, b, h: b_col[i] + m_prev[i])
    mt = each(lambda i, b, h: jnp.maximum(inter[i], jnp.max(dm[i], axis=-1, keepdims=True)))
    wo = each(lambda i, b, h: jnp.exp(inter[i] - mt[i]))
    yield
    s = each(lambda i, b, h: _dot_nt(q16[i], k[i].astype(BF16)) * jnp.exp(dm[i] - mt[i]))
    yield
    v16 = each(lambda i, b, h: pb_ref[b, win, hs(h, 2 * B_WIDTH)].astype(BF16))
    num = each(lambda i, b, h: wo[i] * _dot(q16[i], c_prev[i].astype(BF16)) + _dot(s[i].astype(BF16), v16[i]))
    yield
    den = each(lambda i, b, h: wo[i] * jnp.sum(q[i] * n_prev[i], axis=-1, keepdims=True)
               + jnp.sum(s[i], axis=-1, keepdims=True))
    hh = each(lambda i, b, h: num[i] * (1.0 / jnp.maximum(jnp.abs(den[i]), jnp.exp(-mt[i]))))
    yield

    m_new = each(lambda i, b, h: mt[i][L - 1:L, :])
    b_last = each(lambda i, b, h: b_col[i][L - 1:L, :])
    kw = each(lambda i, b, h: k[i] * jnp.exp(b_last[i] - b_col[i] + li_col[i] - m_new[i]))
    dec = each(lambda i, b, h: jnp.exp(b_last[i] + m_prev[i] - m_new[i]))
    yield
    c_new = each(lambda i, b, h: dec[i] * c_prev[i] + _dot_tn(kw[i].astype(BF16), v16[i]))
    n_new = each(lambda i, b, h: dec[i] * n_prev[i] + jnp.sum(kw[i], axis=0, keepdims=True))
    yield
    out = each(lambda i, b, h: hh[i] * lax.rsqrt(jnp.mean(hh[i] * hh[i], axis=-1, keepdims=True) + RMS_EPS)
               * hnw_ref[:, hs(h)] * _sigmoid(pb_ref[b, win, hs(h, 3 * B_WIDTH)]))
    for i, (b, h) in enumerate(chains):
        c_ref[b, h] = c_new[i]
        n_ref[b, h:h + 1, :] = n_new[i]
        m_ref[b, h:h + 1, :] = jnp.broadcast_to(m_new[i], (1, B_HEAD_DIM))
        y_ref[b, win, hs(h, A_WIDTH)] = out[i]


PROLOGUE_DONE, READS_STATE, WROTE_STATE, DONE = "prologue_done", "reads_state", "wrote_state", "done"
N_RWKV_W = 11
N_MLSTM_W = 4
N_CONSTS = 2


def _mix_kernel(*refs, chunk, hg, where, zero_first):
    it = iter(refs)
    take = lambda n: [next(it) for _ in range(n)]
    pa_ref, pb_ref, pg_ref, shift_ref, s0_ref, conv0_ref, c0_ref, n0_ref, m0_ref = take(9)
    rwkv_w, mlstm_w, slabs = take(N_RWKV_W), take(N_MLSTM_W), take(N_CONSTS)
    consts = {name: functools.partial(lambda ref, r0, rows, cols: ref[r0:r0 + rows, 0:cols], ref, *at)
              for ref, masks in zip(slabs, where) for name, at in masks}
    y_ref, shift_out_ref, s_out_ref, conv_out_ref, c_ref, n_ref, m_ref, pbuf, bds, xbuf = take(10)
    first = pl.program_id(1) == 0
    last = pl.program_id(1) == pl.num_programs(1) - 1
    zero_state = pl.program_id(0) == 0 if zero_first else None
    subs = pa_ref.shape[1] // chunk
    bodies = []
    for sub in range(subs):
        is_first = first if sub == 0 else None
        bodies.append([
            _rwkv_stages(pa_ref, shift_ref, s0_ref, rwkv_w, consts, y_ref, shift_out_ref, s_out_ref, pbuf, bds,
                         chunk=chunk, hg=hg, row0=sub * chunk, first=is_first,
                         last=last if sub == subs - 1 else None, zero_state=zero_state),
            _mlstm_stages(pb_ref, pg_ref, conv0_ref, c0_ref, n0_ref, m0_ref, mlstm_w, consts,
                          y_ref, conv_out_ref, c_ref, n_ref, m_ref, xbuf, chunk=chunk, row0=sub * chunk,
                          first=is_first, zero_state=zero_state),
        ])
    live = [(sub, kind) for sub in range(subs) for kind in range(2)]
    wrote, waiting, started = set(), set(), {(0, 0), (0, 1)}
    while live:
        for key in list(live):
            sub, kind = key
            if key not in started or (key in waiting and (sub - 1, kind) not in wrote):
                continue
            waiting.discard(key)
            mark = next(bodies[sub][kind], DONE)
            if mark == PROLOGUE_DONE:
                started.add((sub + 1, 0))
                if sub > 0:
                    started.add((sub, 1))
            elif mark == READS_STATE and sub > 0:
                waiting.add(key)
            elif mark in (WROTE_STATE, DONE):
                wrote.add(key)
                if mark == DONE:
                    live.remove(key)


def _mix(pa, pb, pg, l, sl, states, rwkv_w, mlstm_w, chunk, subs, zero_first):
    bn, t, _ = pa.shape
    nb = MIX_STREAMS
    hg = _rwkv_heads_per_group(chunk)
    hist = B_CONV - 1
    consts, where = _mix_consts(chunk, nb)
    full = lambda a: pl.BlockSpec(a.shape, lambda i, c: (0,) * a.ndim)
    tok = lambda n: pl.BlockSpec((nb, subs * chunk, n), lambda i, c: (i, c, 0))
    per_stream = lambda *dims: pl.BlockSpec((nb,) + dims, lambda i, c: (i,) + (0,) * len(dims))
    state_shapes = [(1, A_COLS), (A_HEADS, A_HEAD_DIM, A_HEAD_DIM), (hist, 2 * B_WIDTH),
                    (B_HEADS, B_HEAD_DIM, B_HEAD_DIM), (B_HEADS, B_HEAD_DIM), (B_HEADS, B_HEAD_DIM)]
    return pl.pallas_call(
        functools.partial(_mix_kernel, chunk=chunk, hg=hg, where=where, zero_first=zero_first),
        grid=(bn // nb, t // (subs * chunk)),
        in_specs=[tok(A_COLS), tok(B_MAIN), tok(GATE_PAD)]
        + [_state_spec(a, sl, nb, zero_first) for a in states]
        + [_layer_spec(a, l) for a in rwkv_w + mlstm_w] + [full(a) for a in consts],
        out_specs=[tok(D_MIX)] + [per_stream(*dims) for dims in state_shapes],
        out_shape=[jax.ShapeDtypeStruct((bn, t, D_MIX), F32)]
        + [jax.ShapeDtypeStruct((bn,) + dims, F32) for dims in state_shapes],
        scratch_shapes=[
            pltpu.VMEM((nb, CARRY, A_COLS), F32),
            pltpu.VMEM((nb, A_HEADS // hg, hg * A_HEAD_DIM, hg * A_HEAD_DIM), F32),
            pltpu.VMEM((nb, CARRY, 2 * B_WIDTH), F32),
        ],
        compiler_params=pltpu.CompilerParams(
            dimension_semantics=("arbitrary", "arbitrary"), vmem_limit_bytes=VMEM_LIMIT),
        name="mixers",
    )(pa, pb, pg, *states, *rwkv_w, *mlstm_w, *consts)


def _ffn_kernel(x_ref, y_ref, f0_ref, wout_ref, g_ref, wup_ref, cw_ref, cb_ref,
                wdown_ref, gfin_ref, o_ref, fout_ref, ubuf, *, tile, final, zero_streams):
    nb = x_ref.shape[0]
    hist = FFN_CONV - 1
    rows_of = lambda ref: ref[0] if nb == 1 else jnp.concatenate([ref[b] for b in range(nb)], axis=0)

    @pl.when(pl.program_id(1) == 0)
    def _():
        ubuf[...] = jnp.zeros(ubuf.shape, F32)
        for b in range(zero_streams, nb):
            ubuf[b, CARRY - hist:CARRY, :] = f0_ref[b - zero_streams]

    x1 = rows_of(x_ref) + _dot(rows_of(y_ref).astype(BF16), wout_ref[...])
    h2 = _rms(x1, g_ref[...]).astype(BF16)
    acts = []
    for lo, hi in zip(FF_BOUNDS[:-1], FF_BOUNDS[1:]):
        cols = slice(lo, hi)
        u = _dot(h2, wup_ref[:, cols])
        gate = _dot(h2, wup_ref[:, D_FF + lo:D_FF + hi])
        convs = []
        for b in range(nb):
            ub = u[b * tile:(b + 1) * tile]
            prev = ubuf[b, :, cols]
            acc = cb_ref[:, cols] + cw_ref[hist:hist + 1, cols] * ub
            for j in range(hist):
                acc = acc + cw_ref[j:j + 1, cols] * _shift_rows(ub, prev, hist - j)
            ubuf[b, :, cols] = ub[tile - CARRY:tile]
            fout_ref[b, :, cols] = ub[tile - hist:tile]
            convs.append(acc)
        conv = convs[0] if nb == 1 else jnp.concatenate(convs, axis=0)
        acts.append((_silu(conv) * gate).astype(BF16))
    x2 = x1 + _dot(jnp.concatenate(acts, axis=1), wdown_ref[...])
    out = _rms(x2, gfin_ref[...]) if final else x2
    for b in range(nb):
        o_ref[b] = out[b * tile:(b + 1) * tile]


def _ffn(x, y, l, sl, f0, wts, gfin, nb, tile, final, zero_streams):
    bn, t, _ = x.shape
    hist = FFN_CONV - 1
    tok = pl.BlockSpec((nb, tile, D_MODEL), lambda i, c: (i, c, 0))
    fspec = pl.BlockSpec((nb, hist, D_FF), lambda i, c: (i, 0, 0))
    return pl.pallas_call(
        functools.partial(_ffn_kernel, tile=tile, final=final, zero_streams=zero_streams),
        grid=(bn // nb, t // tile),
        in_specs=[tok, tok, _state_spec(f0, sl, nb - zero_streams)]
        + [_layer_spec(a, l, resident=True) for a in wts] + [_layer_spec(gfin, 0, resident=True)],
        out_specs=[tok, fspec],
        out_shape=[
            jax.ShapeDtypeStruct((bn, t, D_MODEL), F32),
            jax.ShapeDtypeStruct((bn, hist, D_FF), F32),
        ],
        scratch_shapes=[pltpu.VMEM((nb, CARRY, D_FF), F32)],
        compiler_params=pltpu.CompilerParams(
            dimension_semantics=("arbitrary", "arbitrary"), vmem_limit_bytes=VMEM_LIMIT),
        name="out_ffn",
    )(x, y, f0, *wts, gfin)


def _layer(x, l, sl, st, w, *, in_tile, mix_chunk, mix_subs, ffn_streams, ffn_tile, final, zero_streams):
    bn, t, _ = x.shape
    pa, pb, pg = _in_proj(x.reshape(bn * t, D_MODEL), l, w["norm_mix"], w["w_in"], w["w_gate"], in_tile)
    pa = pa.reshape(bn, t, A_COLS)
    pb = pb.reshape(bn, t, B_MAIN)
    pg = pg.reshape(bn, t, GATE_PAD)
    y, *mix_states = _mix(pa, pb, pg, l, sl, st[:6], w["rwkv"], w["mlstm"], mix_chunk, mix_subs,
                          zero_streams > 0)
    x, fconv1 = _ffn(x, y, l, sl, st[6], w["ffn"], w["norm_final"], ffn_streams, ffn_tile, final, zero_streams)
    return x, tuple(mix_states) + (fconv1,)


def kernel(x_prompt, x_sample, state_rwkv_shift, state_rwkv_wkv, state_mlstm_conv, state_mlstm_C,
           state_mlstm_n, state_mlstm_m, state_ffn_conv, meta_tokens, norm_mix, w_in, a_mu, a_w0, a_w2,
           a_a0, a_a2, a_g2, a_k_k, a_k_a, a_r_k, a_ln_w, a_ln_b, b_conv_w, b_conv_b, b_i_bias, b_f_bias,
           b_hn_w, w_out, norm_ffn, w_up, ffn_conv_w, ffn_conv_b, w_down, norm_final):
    n_prompt = x_prompt.shape[0]
    n_sample = x_sample.shape[0]
    n_lead = n_prompt + n_sample
    assert x_sample.shape[1] == N_META
    assert n_prompt == MIX_STREAMS and n_lead % MIX_STREAMS == 0

    bf = lambda a: a.astype(BF16)
    vec = lambda a: a[:, None, :]
    n_gate = 2 * B_HEADS
    gate_pad = ((0, 0), (0, GATE_PAD - n_gate))
    w = {
        "norm_mix": vec(norm_mix),
        "w_in": bf(w_in),
        "w_gate": bf(jnp.pad(w_in[:, :, A_COLS + B_MAIN:], ((0, 0),) + gate_pad)),
        "rwkv": [vec(a_mu), vec(a_w0), bf(a_w2), vec(a_a0), bf(a_a2), bf(a_g2), vec(a_k_k), vec(a_k_a),
                 vec(a_r_k), vec(a_ln_w), vec(a_ln_b)],
        "mlstm": [b_conv_w, vec(b_conv_b), vec(jnp.pad(jnp.concatenate([b_i_bias, b_f_bias], axis=1), gate_pad)),
                  vec(b_hn_w)],
        "ffn": (bf(w_out), vec(norm_ffn), bf(w_up), ffn_conv_w, vec(ffn_conv_b), bf(w_down)),
        "norm_final": norm_final.reshape(1, 1, D_MODEL),
    }
    assert len(w["rwkv"]) == N_RWKV_W and len(w["mlstm"]) == N_MLSTM_W

    st_lead_in = (
        state_rwkv_shift[:, :, None, :], state_rwkv_wkv, state_mlstm_conv, state_mlstm_C, state_mlstm_n,
        jnp.broadcast_to(state_mlstm_m[..., None], state_mlstm_m.shape + (B_HEAD_DIM,)), state_ffn_conv,
    )
    meta = jnp.broadcast_to(meta_tokens[None].astype(x_prompt.dtype), (n_prompt, N_META, D_MODEL))
    x_lead = jnp.concatenate([meta, x_sample], axis=0)
    x_main = x_prompt

    main_states, lead_states = [], []
    for l in range(DEPTH):
        final = l == DEPTH - 1
        x_lead, st_lead = _layer(x_lead, l, l, st_lead_in, w, in_tile=n_lead * N_META, mix_chunk=N_META,
                                 mix_subs=1, ffn_streams=n_lead, ffn_tile=N_META, final=final,
                                 zero_streams=n_prompt)
        x_main, st_main = _layer(x_main, l, 0, tuple(s[None] for s in st_lead), w, in_tile=MAIN_PROJ_TILE,
                                 mix_chunk=MAIN_CHUNK, mix_subs=MAIN_SUBCHUNKS, ffn_streams=1, ffn_tile=MAIN_TILE, final=final,
                                 zero_streams=0)
        main_states.append(st_main)
        lead_states.append(st_lead)

    def collect(per_layer, first):
        shift, wkv, bconv, c, n, m, fconv = (jnp.stack([st[i] for st in per_layer])[:, first:] for i in range(7))
        return (shift[:, :, 0, :], wkv, bconv, c, n, m[..., 0], fconv)

    return (x_main, x_lead[n_prompt:]) + collect(main_states, 0) + collect(lead_states, n_prompt)
```
